```python
import math
import jax
import jax.numpy as jnp
from jax import lax
import numpy as np

D_MODEL = 1024
BATCH = 8
SEQ = 2048
DEPTH = 2

GRID_W = 64
CTX_LEN = 256
ROPE_THETA = 10000.0
NORM_EPS = 1e-6
LB_FLOOR = 1e-30
Q_BLOCK = 128
CHUNK = 64

DA_HEADS = 4
DA_QK_DIM = 64
DA_V_DIM = 2 * DA_QK_DIM
HG_HEADS = 4
HG_K_DIM = 64
HG_V_DIM = 64
GD_HEADS = 4
GD_K_DIM = 64
GD_V_DIM = 64
GD_CONV = 3
FFN_CONV = 3
D_FF = 2816

DA_QK_W = DA_HEADS * 2 * DA_QK_DIM
DA_V_W = DA_HEADS * DA_V_DIM
HG_K_W = HG_HEADS * HG_K_DIM
HG_V_W = HG_HEADS * HG_V_DIM
GD_K_W = GD_HEADS * GD_K_DIM
GD_V_W = GD_HEADS * GD_V_DIM
MIX_W = DA_V_W + HG_V_W + GD_V_W
IN_SPLIT = (DA_QK_W, DA_QK_W, DA_V_W, HG_K_W, HG_V_W, 2 * HG_K_W, HG_V_W,
            2 * GD_K_W + GD_V_W, 2 * GD_HEADS, 2 * GD_HEADS, GD_V_W)
IN_COLS = sum(IN_SPLIT)
F32 = jnp.float32

kernel_name = "hybrid_diffattn_hgrn2_gdn_convffn_dit"


def rms_norm(x, g):
    xf = x.astype(F32)
    y = xf * lax.rsqrt(jnp.mean(xf * xf, axis=-1, keepdims=True) + NORM_EPS)
    return (y * g.astype(F32)).astype(x.dtype)


def l2_normalize(x):
    return x * lax.rsqrt(jnp.sum(x * x, axis=-1, keepdims=True) + NORM_EPS)


def modulate(h, shift, scale):
    return h * (1.0 + scale) + shift


def split_cols(p):
    return jnp.split(p, np.cumsum(IN_SPLIT)[:-1].tolist(), axis=-1)


def to_heads(x, n_heads):
    b, l, _ = x.shape
    return x.reshape(b, l, n_heads, -1).transpose(0, 2, 1, 3)


def from_heads(x):
    b, h, l, d = x.shape
    return x.transpose(0, 2, 1, 3).reshape(b, l, h * d)


def depthwise_conv(x, w):
    k, l = w.shape[0], x.shape[1]
    xp = jnp.pad(x, ((0, 0), (k // 2, k // 2), (0, 0)))
    return sum(xp[:, j:j + l] * w[j] for j in range(k))


def masked_decay(diff, mask):
    return jnp.where(mask, jnp.exp(jnp.minimum(diff, 0.0)), 0.0)


def axial_rope_tables(n_rows):
    n_freq = DA_QK_DIM // 4
    inv = ROPE_THETA ** (-jnp.arange(n_freq, dtype=F32) / n_freq)
    rows = jnp.repeat(jnp.arange(n_rows, dtype=F32), GRID_W)
    cols = jnp.tile(jnp.arange(GRID_W, dtype=F32), n_rows)
    ang = jnp.concatenate([rows[:, None] * inv, cols[:, None] * inv], axis=-1)
    return jnp.cos(ang), jnp.sin(ang)


def apply_rope(x, cos, sin):
    xa, xb = jnp.split(x, 2, axis=-1)
    return jnp.concatenate([xa * cos - xb * sin, xb * cos + xa * sin], axis=-1).astype(x.dtype)


def diff_softmax_attend(q, k, v, lam):
    s = jnp.einsum('bhmqd,bhmkd->bhmqk', q, k).astype(F32) * DA_QK_DIM ** -0.5
    p = jax.nn.softmax(s, axis=-1)
    w = p[:, :, 0] - lam * p[:, :, 1]
    return jnp.einsum('bhqk,bhkd->bhqd', w.astype(v.dtype), v)


def blocked_diff_attend(q, k, v, lam):
    b, h, m, l, d = q.shape
    nb = l // Q_BLOCK
    qb = jnp.moveaxis(q.reshape(b, h, m, nb, Q_BLOCK, d), 3, 0)
    o = lax.map(lambda qi: diff_softmax_attend(qi, k, v, lam), qb)
    return jnp.moveaxis(o, 0, 2).reshape(b, h, l, -1)


def diff_attention_mixer(q_l, k_l, v_l, q_c, k_c, v_c, lam_p, subln_g, lam_init, cos, sin, need_ctx):
    def qk_heads(p):
        b, l, _ = p.shape
        return p.reshape(b, l, DA_HEADS, 2, DA_QK_DIM).transpose(0, 2, 3, 1, 4)

    ql = apply_rope(qk_heads(q_l), cos, sin)
    kl = apply_rope(qk_heads(k_l), cos, sin)
    kc = qk_heads(k_c)
    vl, vc = to_heads(v_l, DA_HEADS), to_heads(v_c, DA_HEADS)
    lp = lam_p.astype(F32)
    lam = jnp.exp(jnp.sum(lp[0] * lp[1])) - jnp.exp(jnp.sum(lp[2] * lp[3])) + lam_init

    def finish(o):
        return from_heads(rms_norm(o, subln_g) * (1.0 - lam_init))

    k_all = jnp.concatenate([kc, kl], axis=3)
    v_all = jnp.concatenate([vc, vl], axis=2)
    o_lat = finish(blocked_diff_attend(ql, k_all, v_all, lam))
    o_ctx = finish(diff_softmax_attend(qk_heads(q_c), kc, vc, lam)) if need_ctx else None
    return o_lat, o_ctx


def chunk_scan(step, s0, seqs):
    l = seqs[0].shape[2]
    n = l // CHUNK

    def to_chunks(a):
        return jnp.moveaxis(a.reshape(a.shape[:2] + (n, CHUNK) + a.shape[3:]), 2, 0)

    s_final, o = lax.scan(step, s0, tuple(to_chunks(a) for a in seqs))
    o = jnp.moveaxis(o, 0, 2)
    return o.reshape(o.shape[:2] + (l,) + o.shape[4:]), s_final


def flip_time(seqs):
    return tuple(jnp.flip(a, axis=2) for a in seqs)


def bidirectional_scan(step, s0, ctx_fwd, lat_fwd, ctx_bwd, lat_bwd):
    o_cf, s_cf = chunk_scan(step, s0, ctx_fwd)
    o_lf, _ = chunk_scan(step, s_cf, lat_fwd)
    o_cb, s_cb = chunk_scan(step, s0, flip_time(ctx_bwd))
    o_lb, _ = chunk_scan(step, s_cb, flip_time(lat_bwd))
    return o_lf + jnp.flip(o_lb, axis=2), o_cf + jnp.flip(o_cb, axis=2)


def hgrn2_chunk_step(state, inp):
    q, k, v, log_f = inp
    c = q.shape[2]
    lower = jnp.tril(jnp.ones((c, c), dtype=bool))[:, :, None]
    b = jnp.cumsum(log_f, axis=2)
    decay = masked_decay(b[:, :, :, None, :] - b[:, :, None, :, :], lower)
    scores = jnp.einsum('bhtd,bhsd,bhtsd->bhts', q, k, decay)
    out = jnp.einsum('bhtd,bhde->bhte', q * jnp.exp(b), state) + jnp.einsum('bhts,bhse->bhte', scores, v)
    b_last = b[:, :, -1:, :]
    state = jnp.exp(b_last[:, :, 0, :])[..., None] * state + jnp.einsum('bhsd,bhse->bhde', k * jnp.exp(b_last - b), v)
    return state, out


def hgrn2_mixer(q_l, i_l, f_l, g_l, q_c, i_c, f_c, g_c, lb, norm_g, need_ctx):
    lbh = lb.astype(F32).reshape(1, 2, HG_HEADS, 1, HG_K_DIM)
    log_lb = jnp.log(jnp.maximum(lbh, LB_FLOOR))
    log_1m_lb = jnp.log1p(-lbh)

    def prep(q, i, f):
        qh = to_heads(jax.nn.silu(q), HG_HEADS).astype(F32)
        vh = to_heads(i, HG_HEADS).astype(F32)
        z = to_heads(f, 2 * HG_HEADS).astype(F32).reshape(f.shape[0], 2, HG_HEADS, f.shape[1], HG_K_DIM)
        k = (1.0 - lbh) * jax.nn.sigmoid(-z)
        log_f = jnp.logaddexp(log_lb, log_1m_lb + jax.nn.log_sigmoid(z))
        return tuple((qh, k[:, d], vh, log_f[:, d]) for d in range(2))

    fwd_c, bwd_c = prep(q_c, i_c, f_c)
    fwd_l, bwd_l = prep(q_l, i_l, f_l)
    s0 = jnp.zeros((q_l.shape[0], HG_HEADS, HG_K_DIM, HG_V_DIM), F32)
    o_l, o_c = bidirectional_scan(hgrn2_chunk_step, s0, fwd_c, fwd_l, bwd_c, bwd_l)

    def finish(o, g):
        return from_heads(rms_norm(o, norm_g)).astype(g.dtype) * jax.nn.silu(g)

    return finish(o_l, g_l), (finish(o_c, g_c) if need_ctx else None)


def gdn_chunk_step(state, inp):
    q, k, v, log_alpha, beta = inp
    c = q.shape[2]
    lower = jnp.tril(jnp.ones((c, c), dtype=bool))
    strict = jnp.tril(jnp.ones((c, c), dtype=bool), -1)
    gc = jnp.cumsum(log_alpha, axis=-1)
    decay = masked_decay(gc[..., :, None] - gc[..., None, :], lower)
    kk = jnp.einsum('bhtd,bhsd->bhts', k, k)
    tri = jnp.eye(c, dtype=q.dtype) + jnp.where(strict, beta[..., :, None] * kk * decay, 0.0)
    rhs = jnp.concatenate([v * beta[..., None], k * (beta * jnp.exp(gc))[..., None]], axis=-1)
    sol = lax.linalg.triangular_solve(tri, rhs, left_side=True, lower=True, unit_diagonal=True)
    dv = v.shape[-1]
    u, w = sol[..., :dv], sol[..., dv:]
    v_new = u - jnp.einsum('bhtd,bhde->bhte', w, state)
    scores = jnp.einsum('bhtd,bhsd->bhts', q, k) * decay
    out = jnp.einsum('bhtd,bhde->bhte', q * jnp.exp(gc)[..., None], state) + jnp.einsum('bhts,bhse->bhte', scores, v_new)
    g_last = gc[..., -1:]
    state = jnp.exp(g_last)[..., None] * state + jnp.einsum('bhsd,bhse->bhde', k * jnp.exp(g_last - gc)[..., None], v_new)
    return state, out


def gated_deltanet_mixer(qkv_l, a_l, b_l, g_l, qkv_c, a_c, b_c, g_c, conv_w, a_log, dt_bias, norm_g, need_ctx):
    def prep(qkv, a_in, b_in):
        qkv = jax.nn.silu(depthwise_conv(qkv, conv_w))
        q, k, v = jnp.split(qkv, [GD_K_W, 2 * GD_K_W], axis=-1)
        qh = l2_normalize(to_heads(q, GD_HEADS).astype(F32)) * GD_K_DIM ** -0.5
        kh = l2_normalize(to_heads(k, GD_HEADS).astype(F32))
        vh = to_heads(v, GD_HEADS).astype(F32)
        bsz, l = a_in.shape[0], a_in.shape[1]
        a_t = jnp.swapaxes(a_in.astype(F32), 1, 2).reshape(bsz, 2, GD_HEADS, l)
        beta = jax.nn.sigmoid(jnp.swapaxes(b_in.astype(F32), 1, 2).reshape(bsz, 2, GD_HEADS, l))
        log_alpha = -jnp.exp(a_log.astype(F32))[None, :, :, None] * jax.nn.softplus(a_t + dt_bias.astype(F32)[None, :, :, None])
        return tuple((qh, kh, vh, log_alpha[:, d], beta[:, d]) for d in range(2))

    fwd_c, bwd_c = prep(qkv_c, a_c, b_c)
    fwd_l, bwd_l = prep(qkv_l, a_l, b_l)
    s0 = jnp.zeros((qkv_l.shape[0], GD_HEADS, GD_K_DIM, GD_V_DIM), F32)
    o_l, o_c = bidirectional_scan(gdn_chunk_step, s0, fwd_c, fwd_l, bwd_c, bwd_l)

    def finish(o, g):
        return from_heads(rms_norm(o, norm_g)).astype(g.dtype) * jax.nn.silu(g)

    return finish(o_l, g_l), (finish(o_c, g_c) if need_ctx else None)


def conv_ffn(h, w_up, conv_w, conv_b, w_down):
    u = depthwise_conv(h @ w_up, conv_w) + conv_b
    gate, val = jnp.split(u, 2, axis=-1)
    return (jax.nn.silu(gate) * val) @ w_down


def setup_inputs(seed: int = 0) -> dict:
    key = jax.random.key(seed)
    ks = jax.random.split(key, 24)

    def nrm(k, shape, s):
        return jax.random.normal(k, shape, F32) * s

    dt = jnp.exp(jax.random.uniform(ks[15], (DEPTH, 2, GD_HEADS), F32, math.log(1e-3), math.log(1e-1)))
    return {
        'x': nrm(ks[0], (BATCH, SEQ, D_MODEL), 1.0),
        'c': nrm(ks[1], (BATCH, D_MODEL), 1.0),
        'ctx': nrm(ks[2], (BATCH, CTX_LEN, D_MODEL), 1.0),
        'c_ctx': nrm(ks[3], (D_MODEL,), 1.0),
        'ada_w': nrm(ks[4], (DEPTH, D_MODEL, 6 * D_MODEL), 0.5 * D_MODEL ** -0.5),
        'ada_b': nrm(ks[5], (DEPTH, 6 * D_MODEL), 0.01),
        'norm_g': 1.0 + nrm(ks[6], (DEPTH, 4, D_MODEL), 0.02),
        'w_in': nrm(ks[7], (DEPTH, D_MODEL, IN_COLS), D_MODEL ** -0.5),
        'w_out': nrm(ks[8], (DEPTH, MIX_W, D_MODEL), MIX_W ** -0.5),
        'da_lambda': nrm(ks[9], (DEPTH, 4, DA_QK_DIM), 0.1),
        'da_subln_g': 1.0 + nrm(ks[10], (DEPTH, DA_V_DIM), 0.02),
        'hg_lb_logits': nrm(ks[11], (DEPTH, 2, HG_K_W), 0.1),
        'hg_norm_g': 1.0 + nrm(ks[12], (DEPTH, HG_V_DIM), 0.02),
        'gd_conv_w': nrm(ks[13], (DEPTH, GD_CONV, 2 * GD_K_W + GD_V_W), GD_CONV ** -0.5),
        'gd_a_log': jnp.log(jax.random.uniform(ks[14], (DEPTH, 2, GD_HEADS), F32, 1.0, 16.0)),
        'gd_dt_bias': dt + jnp.log(-jnp.expm1(-dt)),
        'gd_norm_g': 1.0 + nrm(ks[16], (DEPTH, GD_V_DIM), 0.02),
        'ffn_w_up': nrm(ks[17], (DEPTH, D_MODEL, 2 * D_FF), D_MODEL ** -0.5),
        'ffn_conv_w': nrm(ks[18], (DEPTH, FFN_CONV, 2 * D_FF), FFN_CONV ** -0.5),
        'ffn_conv_b': nrm(ks[19], (DEPTH, 2 * D_FF), 0.01),
        'ffn_w_down': nrm(ks[20], (DEPTH, D_FF, D_MODEL), D_FF ** -0.5),
    }


def reference(x, c, ctx, c_ctx, ada_w, ada_b, norm_g, w_in, w_out, da_lambda, da_subln_g,
              hg_lb_logits, hg_norm_g, gd_conv_w, gd_a_log, gd_dt_bias, gd_norm_g,
              ffn_w_up, ffn_conv_w, ffn_conv_b, ffn_w_down):
    n_rows = x.shape[1] // GRID_W
    cos, sin = axial_rope_tables(n_rows)
    lb_w = jax.nn.softmax(hg_lb_logits.astype(F32), axis=0)
    lb_all = jnp.cumsum(lb_w, axis=0) - lb_w[0]
    cond_lat = jax.nn.silu(c)[:, None, :]
    cond_ctx = jax.nn.silu(c_ctx)
    h_ctx = ctx
    for layer in range(DEPTH):
        need_ctx = layer < DEPTH - 1
        lam_init = 0.8 - 0.6 * math.exp(-0.3 * layer)
        mod_l = jnp.split(cond_lat @ ada_w[layer] + ada_b[layer], 6, axis=-1)
        mod_c = jnp.split(cond_ctx @ ada_w[layer] + ada_b[layer], 6, axis=-1)

        p_l = split_cols(modulate(rms_norm(x, norm_g[layer, 0]), mod_l[0], mod_l[1]) @ w_in[layer])
        p_c = split_cols(modulate(rms_norm(h_ctx, norm_g[layer, 0]), mod_c[0], mod_c[1]) @ w_in[layer])
        oa_l, oa_c = diff_attention_mixer(p_l[0], p_l[1], p_l[2], p_c[0], p_c[1], p_c[2],
                                          da_lambda[layer], da_subln_g[layer], lam_init, cos, sin, need_ctx)
        ob_l, ob_c = hgrn2_mixer(p_l[3], p_l[4], p_l[5], p_l[6], p_c[3], p_c[4], p_c[5], p_c[6],
                                 lb_all[layer], hg_norm_g[layer], need_ctx)
        oc_l, oc_c = gated_deltanet_mixer(p_l[7], p_l[8], p_l[9], p_l[10], p_c[7], p_c[8], p_c[9], p_c[10],
                                          gd_conv_w[layer], gd_a_log[layer], gd_dt_bias[layer], gd_norm_g[layer], need_ctx)
        mix_l = jnp.concatenate([oa_l, ob_l, oc_l], axis=-1) @ w_out[layer]
        x = x + mod_l[2] * rms_norm(mix_l, norm_g[layer, 1])

        ff_l = conv_ffn(modulate(rms_norm(x, norm_g[layer, 2]), mod_l[3], mod_l[4]),
                        ffn_w_up[layer], ffn_conv_w[layer], ffn_conv_b[layer], ffn_w_down[layer])
        x = x + mod_l[5] * rms_norm(ff_l, norm_g[layer, 3])

        if need_ctx:
            mix_c = jnp.concatenate([oa_c, ob_c, oc_c], axis=-1) @ w_out[layer]
            h_ctx = h_ctx + mod_c[2] * rms_norm(mix_c, norm_g[layer, 1])
            ff_c = conv_ffn(modulate(rms_norm(h_ctx, norm_g[layer, 2]), mod_c[3], mod_c[4]),
                            ffn_w_up[layer], ffn_conv_w[layer], ffn_conv_b[layer], ffn_w_down[layer])
            h_ctx = h_ctx + mod_c[5] * rms_norm(ff_c, norm_g[layer, 3])
    return x
```

```python
import functools
import math

import jax
import jax.numpy as jnp
import numpy as np
from jax import lax
from jax.experimental import pallas as pl
from jax.experimental.pallas import tpu as pltpu

F32 = jnp.float32
BF16 = jnp.bfloat16

D = 1024
CTX = 256
SEQ = 2048
T = CTX + SEQ
GRID_W = 64
ROPE_THETA = 10000.0
EPS = 1e-6
LB_FLOOR = 1e-30
NH = 4
HD = 64
DV = 128
HW = NH * HD
D_FF = 2816
TM = 256
NT = T // TM
HALO = 16
TQ = 128
BLK = 16
CH = 64
FF_CW = 256
VMEM_LIMIT = 56 * 1024 * 1024

C_DAQ, C_DAK, C_DAV = 0, 512, 1024
C_HGQ, C_HGI, C_HGF, C_HGG = 1536, 1792, 2048, 2560
C_GDQKV, C_GDA, C_GDB, C_GDG = 2816, 3584, 3592, 3600
IN_COLS = 3856


def _cparams(sem):
    return pltpu.CompilerParams(dimension_semantics=sem, vmem_limit_bytes=VMEM_LIMIT)


def _const_spec(shape):
    n = len(shape)
    return pl.BlockSpec(shape, lambda *_: (0,) * n)


def _sigmoid(x):
    return 1.0 / (1.0 + jnp.exp(-x))


def _silu(x):
    return x * _sigmoid(x)


def _softplus(x):
    return jnp.maximum(x, 0.0) + jnp.log(1.0 + jnp.exp(-jnp.abs(x)))


def _dot(a, b):
    return jnp.dot(a, b, preferred_element_type=F32)


def _dot_nt(a, b):
    return lax.dot_general(a, b, (((1,), (1,)), ((), ())), preferred_element_type=F32)


def _dot_tn(a, b):
    return lax.dot_general(a, b, (((0,), (0,)), ((), ())), preferred_element_type=F32)


def _dot_hi(a, b):
    return jnp.dot(a, b, preferred_element_type=F32, precision=lax.Precision.HIGHEST)


def _head_sum(x, ones_bd):
    hi = x.astype(BF16)
    lo = (x - hi.astype(F32)).astype(BF16)
    return _dot(hi, ones_bd) + _dot(lo, ones_bd)


def _ones_bd():
    r = lax.broadcasted_iota(jnp.int32, (HW, HW), 0) // HD
    c = lax.broadcasted_iota(jnp.int32, (HW, HW), 1) // HD
    return jnp.where(r == c, 1.0, 0.0).astype(BF16)


def _bd_mask():
    r = lax.broadcasted_iota(jnp.int32, (HW, HW), 0) // HD
    c = lax.broadcasted_iota(jnp.int32, (HW, HW), 1) // HD
    return r == c


def _normed(x, a, s):
    ms = jnp.mean(x * x, axis=-1, keepdims=True)
    return (x * lax.rsqrt(ms + EPS)) * a + s


def _ada_kernel(c_ref, w_ref, b_ref, o_ref):
    o_ref[0] = _dot(c_ref[...].astype(BF16), w_ref[0].astype(BF16)) + b_ref[0]


def _ada_call(cond, ada_w, ada_b):
    depth = ada_w.shape[0]
    nc = 6 * D
    cw = 1536
    return pl.pallas_call(
        _ada_kernel,
        grid=(depth, nc // cw),
        in_specs=[pl.BlockSpec((16, D), lambda l, j: (0, 0)),
                  pl.BlockSpec((1, D, cw), lambda l, j: (l, 0, j)),
                  pl.BlockSpec((1, 1, cw), lambda l, j: (l, 0, j))],
        out_specs=pl.BlockSpec((1, 16, cw), lambda l, j: (l, 0, j)),
        out_shape=jax.ShapeDtypeStruct((depth, 16, nc), F32),
        compiler_params=_cparams(("arbitrary", "arbitrary")),
    )(cond, ada_w, ada_b.reshape(depth, 1, nc))


def _sel(i, nct):
    return jnp.where(i >= nct, 1, 0)


def _prev_ok(i, nct):
    return i > nct


def _next_ok(i, nct, nt):
    return jnp.logical_and(i >= nct, i <= nt - 2)


def _halo_specs(width, nct, nt):
    per = TM // HALO
    return [pl.BlockSpec((1, HALO, width), lambda b, i: (b, i * per - jnp.where(_prev_ok(i, nct), 1, 0), 0)),
            pl.BlockSpec((1, TM, width), lambda b, i: (b, i, 0)),
            pl.BlockSpec((1, HALO, width), lambda b, i: (b, (i + 1) * per - jnp.where(_next_ok(i, nct, nt), 0, 1), 0))]


def _mod_spec(nct):
    return pl.BlockSpec((1, 1, 1, D), lambda b, i: (b, _sel(i, nct), 0, 0))


def _fill_h(h_scr, xp_ref, xm_ref, xn_ref, a, s, i, nct, nt):
    hp = _normed(xp_ref[0], a, s)
    hn = _normed(xn_ref[0], a, s)
    h_scr[0:HALO, :] = jnp.where(_prev_ok(i, nct), hp, 0.0).astype(BF16)
    h_scr[HALO:HALO + TM, :] = _normed(xm_ref[0], a, s).astype(BF16)
    h_scr[HALO + TM:, :] = jnp.where(_next_ok(i, nct, nt), hn, 0.0).astype(BF16)


def _conv3(u_scr, w_ref):
    return (u_scr[HALO - 1:HALO - 1 + TM, :] * w_ref[0:1, :]
            + u_scr[HALO:HALO + TM, :] * w_ref[1:2, :]
            + u_scr[HALO + 1:HALO + 1 + TM, :] * w_ref[2:3, :])


def _inproj_kernel(xp_ref, xm_ref, xn_ref, a_ref, s_ref, cos_ref, sin_ref,
                   wqk_ref, wv_ref, whg_ref, wgq_ref, wab_ref, wgg_ref,
                   lb1m_ref, loglb_ref, log1mlb_ref, gconv_ref, nega_ref, dtb_ref,
                   q_ref, k_ref, v_ref, hq_ref, hv_ref, hk_ref, hlf_ref, hsg_ref,
                   gq_ref, gk_ref, gv_ref, gab_ref, gsg_ref,
                   h_scr, u_scr):
    i = pl.program_id(1)
    _fill_h(h_scr, xp_ref, xm_ref, xn_ref, a_ref[0, 0], s_ref[0, 0], i, 1, NT)
    h = h_scr[HALO:HALO + TM, :]

    z = _dot(h, wqk_ref[...])
    lane = lax.broadcasted_iota(jnp.int32, (TM, 128), 1)
    first_half = (lane % HD) < (HD // 2)
    cs, sn = cos_ref[...], sin_ref[...]
    for j in range(8):
        xj = z[:, j * 128:(j + 1) * 128]
        sw = jnp.where(first_half, pltpu.roll(xj, 128 - HD // 2, 1), pltpu.roll(xj, HD // 2, 1))
        r = xj * cs + sw * sn
        if j < 4:
            q_ref[0, :, j * 128:(j + 1) * 128] = (r * (HD ** -0.5)).astype(BF16)
        else:
            k_ref[0, :, (j - 4) * 128:(j - 3) * 128] = r.astype(BF16)
    v_ref[0] = _dot(h, wv_ref[...]).astype(BF16)

    z = _dot(h, whg_ref[...])
    hq_ref[0] = _silu(z[:, 0:HW])
    hv_ref[0] = z[:, HW:2 * HW]
    f = z[:, 2 * HW:4 * HW]
    hk_ref[0] = lb1m_ref[...] * _sigmoid(-f)
    a = loglb_ref[...]
    b = log1mlb_ref[...] - _softplus(-f)
    hlf_ref[0] = jnp.maximum(a, b) + jnp.log(1.0 + jnp.exp(-jnp.abs(a - b)))
    hsg_ref[0] = _silu(z[:, 4 * HW:5 * HW])

    u_scr[...] = _dot(h_scr[...], wgq_ref[...])
    y = _silu(_conv3(u_scr, gconv_ref))
    ones_bd = _ones_bd()
    qg, kg = y[:, 0:HW], y[:, HW:2 * HW]
    gq_ref[0] = qg * lax.rsqrt(_head_sum(qg * qg, ones_bd) + EPS) * (HD ** -0.5)
    gk_ref[0] = kg * lax.rsqrt(_head_sum(kg * kg, ones_bd) + EPS)
    gv_ref[0] = y[:, 2 * HW:3 * HW]
    z = _dot(h, wab_ref[...])
    gab_ref[0] = jnp.where(lane < 2 * NH, nega_ref[...] * _softplus(z + dtb_ref[...]), _sigmoid(z))
    gsg_ref[0] = _silu(_dot(h, wgg_ref[...]))


def _inproj_call(xs, a_in, s_in, cos_t, sin_t, w, consts):
    bsz = xs.shape[0]
    row = lambda wd: pl.BlockSpec((1, TM, wd), lambda b, i: (b, i, 0))
    tab = pl.BlockSpec((TM, 128), lambda b, i: (i, 0))
    out_w = [(512, BF16), (512, BF16), (512, BF16), (HW, F32), (HW, F32), (2 * HW, F32), (2 * HW, F32),
             (HW, F32), (HW, F32), (HW, F32), (HW, F32), (128, F32), (HW, F32)]
    return pl.pallas_call(
        _inproj_kernel,
        grid=(bsz, NT),
        in_specs=_halo_specs(D, 1, NT) + [_mod_spec(1), _mod_spec(1), tab, tab]
        + [_const_spec(x.shape) for x in w] + [_const_spec(x.shape) for x in consts],
        out_specs=[row(wd) for wd, _ in out_w],
        out_shape=[jax.ShapeDtypeStruct((bsz, T, wd), dt) for wd, dt in out_w],
        scratch_shapes=[pltpu.VMEM((TM + 2 * HALO, D), BF16), pltpu.VMEM((TM + 2 * HALO, 3 * HW), F32)],
        compiler_params=_cparams(("parallel", "arbitrary")),
    )(xs, xs, xs, a_in, s_in, cos_t, sin_t, *w, *consts)


def _attn_kernel(q_ref, k_ref, v_ref, lam_ref, g_ref, o_ref):
    q = q_ref[0]
    lane = lax.broadcasted_iota(jnp.int32, q.shape, 1)
    zero = jnp.zeros_like(q)
    qs = jnp.concatenate([jnp.where(lane < HD, q, zero), jnp.where(lane >= HD, q, zero)], axis=0)
    st = _dot_nt(k_ref[0], qs)
    m = jnp.max(st, axis=0, keepdims=True)
    p = jnp.exp(st - m)
    l = jnp.sum(p, axis=0, keepdims=True)
    ot = _dot_tn(v_ref[0], p.astype(BF16)) * (1.0 / l)
    d = ot[:, :TQ] - lam_ref[0:1, :] * ot[:, TQ:]
    ms = jnp.mean(d * d, axis=0, keepdims=True)
    dn = d * lax.rsqrt(ms + EPS)
    o_ref[0] = (dn.T * g_ref[...]).astype(BF16)


def _attn_call(q, k, v, lam_arr, g_arr, q_row0, n_q, tk):
    bsz = q.shape[0]
    qoff = q_row0 // TQ
    return pl.pallas_call(
        _attn_kernel,
        grid=(bsz, NH, n_q),
        in_specs=[pl.BlockSpec((1, TQ, 128), lambda b, h, i: (b, qoff + i, h)),
                  pl.BlockSpec((1, tk, 128), lambda b, h, i: (b, 0, h)),
                  pl.BlockSpec((1, tk, 128), lambda b, h, i: (b, 0, h)),
                  _const_spec((8, 128)), _const_spec((1, 128))],
        out_specs=pl.BlockSpec((1, TQ, 128), lambda b, h, i: (b, i, h)),
        out_shape=jax.ShapeDtypeStruct((bsz, n_q * TQ, 512), BF16),
        compiler_params=_cparams(("parallel", "parallel", "arbitrary")),
    )(q, k, v, lam_arr, g_arr)


def _bwd_tile(i):
    return jnp.where(i == 0, 0, NT - i)


def _shift_rows(x, n, reverse):
    if n == 0:
        return x
    rows = x.shape[0]
    return pltpu.roll(x, (rows - n) if reverse else n, 0)


def _hg_dir(q, k, v, logf, s_ref, o_ref, reverse):
    r = lax.broadcasted_iota(jnp.int32, (TM, TM), 0)
    c = lax.broadcasted_iota(jnp.int32, (TM, TM), 1)
    same = (r // BLK) == (c // BLK)
    tri = jnp.where(jnp.logical_and(same, (c >= r) if reverse else (c <= r)), 1.0, 0.0)
    blk = jnp.where(same, 1.0, 0.0)
    bl = _dot_hi(tri, logf)
    tot = _dot_hi(blk, logf)
    qd = q * jnp.exp(bl)
    kd = (k * jnp.exp(tot - bl)).astype(BF16)
    e_blk = jnp.exp(tot)
    f = jnp.exp(logf)

    pos = lax.broadcasted_iota(jnp.int32, (TM, HW), 0) % BLK
    ones_bd = _ones_bd()
    e = None
    o_band = jnp.zeros((TM, HW), F32)
    for n in range(BLK):
        if n == 1:
            e = f
        elif n > 1:
            e = e * _shift_rows(f, n - 1, reverse)
        pn = q * _shift_rows(k, n, reverse)
        if n > 0:
            pn = pn * e
        ok = (pos + n < BLK) if reverse else (pos >= n)
        pn = jnp.where(ok, pn, 0.0)
        o_band = o_band + _head_sum(pn, ones_bd) * _shift_rows(v, n, reverse)

    bd = _bd_mask()
    vb = v.astype(BF16)
    qdb = qd.astype(BF16)
    order = range(TM // BLK - 1, -1, -1) if reverse else range(TM // BLK)
    s = s_ref[...]
    for ib in order:
        rows = slice(ib * BLK, (ib + 1) * BLK)
        o_ref[0, rows, :] = o_band[rows] + _dot_nt(qdb[rows], s.astype(BF16))
        u = _dot_tn(vb[rows], kd[rows])
        s = s * e_blk[ib * BLK:ib * BLK + 1, :] + jnp.where(bd, u, 0.0)
    s_ref[...] = s


def _hg_kernel(qf_ref, vf_ref, kf_ref, lf_ref, qb_ref, vb_ref, kb_ref, lb_ref,
               of_ref, ob_ref, sf_scr, sb_scr):
    @pl.when(pl.program_id(1) == 0)
    def _():
        sf_scr[...] = jnp.zeros_like(sf_scr)
        sb_scr[...] = jnp.zeros_like(sb_scr)

    _hg_dir(qf_ref[0], kf_ref[0], vf_ref[0], lf_ref[0], sf_scr, of_ref, False)
    _hg_dir(qb_ref[0], kb_ref[0], vb_ref[0], lb_ref[0], sb_scr, ob_ref, True)


def _hg_call(hq, hv, hk, hlf):
    bsz = hq.shape[0]
    f = lambda b, i: (b, i, 0)
    g0 = lambda b, i: (b, _bwd_tile(i), 0)
    g1 = lambda b, i: (b, _bwd_tile(i), 1)
    blk = lambda m: pl.BlockSpec((1, TM, HW), m)
    return pl.pallas_call(
        _hg_kernel,
        grid=(bsz, NT),
        in_specs=[blk(f), blk(f), blk(f), blk(f), blk(g0), blk(g0), blk(g1), blk(g1)],
        out_specs=[blk(f), blk(g0)],
        out_shape=[jax.ShapeDtypeStruct((bsz, T, HW), F32)] * 2,
        scratch_shapes=[pltpu.VMEM((HW, HW), F32)] * 2,
        compiler_params=_cparams(("parallel", "arbitrary")),
    )(hq, hv, hk, hlf, hq, hv, hk, hlf)


def _bd4(x, bd):
    return jnp.where(bd, jnp.concatenate([x] * NH, axis=0), 0.0).astype(BF16)


def _gd_chunk(q, k, v, kk, qk, ab, d, s, reverse):
    bd = _bd_mask()
    grp = lax.broadcasted_iota(jnp.int32, (CH, HW), 1) // HD
    la = jnp.zeros((CH, HW), F32)
    beta = jnp.zeros((CH, HW), F32)
    for h in range(NH):
        ca = d * NH + h
        cb = 2 * NH + d * NH + h
        la = jnp.where(grp == h, jnp.broadcast_to(ab[:, ca:ca + 1], (CH, HW)), la)
        beta = jnp.where(grp == h, jnp.broadcast_to(ab[:, cb:cb + 1], (CH, HW)), beta)
    r = lax.broadcasted_iota(jnp.int32, (CH, CH), 0)
    c = lax.broadcasted_iota(jnp.int32, (CH, CH), 1)
    tri = jnp.where((c >= r) if reverse else (c <= r), 1.0, 0.0)
    t_i = lax.broadcasted_iota(jnp.int32, (CH, HW), 0)
    s_i = lax.broadcasted_iota(jnp.int32, (CH, HW), 1) % HD
    later = (t_i < s_i) if reverse else (t_i > s_i)
    valid = (s_i >= t_i) if reverse else (s_i <= t_i)
    strict = (s_i > t_i) if reverse else (s_i < t_i)
    gc = _dot_hi(tri, la)
    diff = _dot_hi(tri, jnp.where(later, la, 0.0))
    dm = jnp.where(valid, jnp.exp(jnp.minimum(diff, 0.0)), 0.0)
    n = jnp.where(strict, beta * kk * dm, 0.0)
    eye = jnp.where(t_i == s_i, 1.0, 0.0)

    def off_block(m):
        t_blk, s_blk = t_i // m, s_i // m
        pair = (t_blk // 2) == (s_blk // 2)
        if reverse:
            return jnp.logical_and(pair, jnp.logical_and(t_blk % 2 == 0, s_blk % 2 == 1))
        return jnp.logical_and(pair, jnp.logical_and(t_blk % 2 == 1, s_blk % 2 == 0))

    tinv = eye - jnp.where(off_block(1), n, 0.0)
    m = 2
    while m < CH:
        y = _dot(tinv.astype(BF16), _bd4(jnp.where(off_block(m), n, 0.0), bd))
        tinv = tinv - _dot(y.astype(BF16), _bd4(tinv, bd))
        m *= 2
    tb = tinv.astype(BF16)
    eg = jnp.exp(gc)
    u = _dot(tb, _bd4(v * beta, bd))
    w = _dot(tb, _bd4(k * beta * eg, bd))
    sc = (qk * dm).astype(BF16)
    qeff = q * eg - _dot(sc, _bd4(w, bd))
    oc = _dot(sc, _bd4(u, bd))
    last = 0 if reverse else CH - 1
    gl = gc[last:last + 1, :]
    kd = (k * jnp.exp(gl - gc)).astype(BF16)
    kw = jnp.where(bd, _dot_tn(kd, w.astype(BF16)), 0.0)
    ku = jnp.where(bd, _dot_tn(kd, u.astype(BF16)), 0.0)
    sb = s.astype(BF16)
    out = _dot(qeff.astype(BF16), sb) + oc
    s_new = s * jnp.exp(gl) - _dot(kw.astype(BF16), sb) + ku
    return out, s_new


def _gd_kernel(qf_ref, kf_ref, vf_ref, abf_ref, qb_ref, kb_ref, vb_ref, abb_ref,
               of_ref, ob_ref, sf_scr, sb_scr):
    @pl.when(pl.program_id(1) == 0)
    def _():
        sf_scr[...] = jnp.zeros_like(sf_scr)
        sb_scr[...] = jnp.zeros_like(sb_scr)

    bd = _bd_mask()
    for d, (q_ref, k_ref, v_ref, ab_ref, o_ref, s_scr) in enumerate(
            ((qf_ref, kf_ref, vf_ref, abf_ref, of_ref, sf_scr),
             (qb_ref, kb_ref, vb_ref, abb_ref, ob_ref, sb_scr))):
        reverse = d == 1
        order = range(TM // CH - 1, -1, -1) if reverse else range(TM // CH)
        s = s_scr[...]
        for ic in order:
            rows = slice(ic * CH, (ic + 1) * CH)
            q, k, v = q_ref[0, rows, :], k_ref[0, rows, :], v_ref[0, rows, :]
            kb = _bd4(k, bd)
            kk = _dot_nt(k.astype(BF16), kb)
            qk = _dot_nt(q.astype(BF16), kb)
            out, s = _gd_chunk(q, k, v, kk, qk, ab_ref[0, rows, :], d, s, reverse)
            o_ref[0, rows, :] = out
        s_scr[...] = s


def _gd_call(gq, gk, gv, gab):
    bsz = gq.shape[0]
    f = lambda b, i: (b, i, 0)
    g0 = lambda b, i: (b, _bwd_tile(i), 0)
    blk = lambda m: pl.BlockSpec((1, TM, HW), m)
    abs_ = lambda m: pl.BlockSpec((1, TM, 128), m)
    return pl.pallas_call(
        _gd_kernel,
        grid=(bsz, NT),
        in_specs=[blk(f), blk(f), blk(f), abs_(f), blk(g0), blk(g0), blk(g0), abs_(g0)],
        out_specs=[blk(f), blk(g0)],
        out_shape=[jax.ShapeDtypeStruct((bsz, T, HW), F32)] * 2,
        scratch_shapes=[pltpu.VMEM((HW, HW), F32)] * 2,
        compiler_params=_cparams(("parallel", "arbitrary")),
    )(gq, gk, gv, gab, gq, gk, gv, gab)


def _outproj_kernel(x_ref, oal_ref, oac_ref, hof_ref, hob_ref, hsg_ref, gof_ref, gob_ref, gsg_ref,
                    wo_ref, hng_ref, gng_ref, g1_ref, gate_ref, o_ref, *, with_ctx):
    ones_bd = _ones_bd()
    oa = oal_ref[0]
    if with_ctx:
        oa = jnp.where(pl.program_id(1) == 0, oac_ref[0], oa)

    def finish(of_ref, ob_ref, sg_ref, ng_ref):
        o = of_ref[0] + ob_ref[0]
        ms = _head_sum(o * o, ones_bd) * (1.0 / HD)
        return (o * lax.rsqrt(ms + EPS) * ng_ref[...] * sg_ref[0]).astype(BF16)

    ob = finish(hof_ref, hob_ref, hsg_ref, hng_ref)
    oc = finish(gof_ref, gob_ref, gsg_ref, gng_ref)
    mix = (_dot(oa, wo_ref[0:512, :]) + _dot(ob, wo_ref[512:768, :]) + _dot(oc, wo_ref[768:1024, :]))
    ms = jnp.mean(mix * mix, axis=-1, keepdims=True)
    o_ref[0] = x_ref[0] + gate_ref[0, 0] * (mix * lax.rsqrt(ms + EPS) * g1_ref[...])


def _outproj_call(xs, oa_lat, oa_ctx, hof, hob, hsg, gof, gob, gsg, wo, hng, gng, g1, gate, with_ctx):
    bsz = xs.shape[0]
    t0 = 0 if with_ctx else 1
    row = lambda wd: pl.BlockSpec((1, TM, wd), lambda b, i: (b, i + t0, 0))
    oal = pl.BlockSpec((1, TM, 512), lambda b, i: (b, jnp.maximum(i + t0 - 1, 0), 0))
    oac = pl.BlockSpec((1, TM, 512), lambda b, i: (b, 0, 0))
    mod = pl.BlockSpec((1, 1, 1, D), lambda b, i: (b, _sel(i + t0, 1), 0, 0))
    return pl.pallas_call(
        functools.partial(_outproj_kernel, with_ctx=with_ctx),
        grid=(bsz, NT - t0),
        in_specs=[row(D), oal, oac, row(HW), row(HW), row(HW), row(HW), row(HW), row(HW),
                  _const_spec((D, D)), _const_spec((1, HW)), _const_spec((1, HW)), _const_spec((1, D)), mod],
        out_specs=pl.BlockSpec((1, TM, D), lambda b, i: (b, i, 0)),
        out_shape=jax.ShapeDtypeStruct((bsz, (NT - t0) * TM, D), F32),
        compiler_params=_cparams(("parallel", "arbitrary")),
    )(xs, oa_lat, oa_ctx, hof, hob, hsg, gof, gob, gsg, wo, hng, gng, g1, gate)


def _ffn_kernel(xp_ref, xm_ref, xn_ref, a_ref, s_ref, wup_ref, cw_ref, cb_ref, wdn_ref, g3_ref, gate_ref,
                o_ref, h_scr, u_scr, act_scr, *, nct, nt):
    _fill_h(h_scr, xp_ref, xm_ref, xn_ref, a_ref[0, 0], s_ref[0, 0], pl.program_id(1), nct, nt)
    h = h_scr[...]
    for cidx in range(D_FF // FF_CW):
        lo = cidx * FF_CW
        u_scr[:, 0:FF_CW] = _dot(h, wup_ref[:, lo:lo + FF_CW])
        u_scr[:, FF_CW:2 * FF_CW] = _dot(h, wup_ref[:, D_FF + lo:D_FF + lo + FF_CW])
        wg = cw_ref[:, lo:lo + FF_CW]
        wv = cw_ref[:, D_FF + lo:D_FF + lo + FF_CW]
        g = (u_scr[HALO - 1:HALO - 1 + TM, 0:FF_CW] * wg[0:1] + u_scr[HALO:HALO + TM, 0:FF_CW] * wg[1:2]
             + u_scr[HALO + 1:HALO + 1 + TM, 0:FF_CW] * wg[2:3] + cb_ref[:, lo:lo + FF_CW])
        vv = (u_scr[HALO - 1:HALO - 1 + TM, FF_CW:] * wv[0:1] + u_scr[HALO:HALO + TM, FF_CW:] * wv[1:2]
              + u_scr[HALO + 1:HALO + 1 + TM, FF_CW:] * wv[2:3] + cb_ref[:, D_FF + lo:D_FF + lo + FF_CW])
        act_scr[:, lo:lo + FF_CW] = (_silu(g) * vv).astype(BF16)
    ff = _dot(act_scr[...], wdn_ref[...])
    ms = jnp.mean(ff * ff, axis=-1, keepdims=True)
    o_ref[0] = xm_ref[0] + gate_ref[0, 0] * (ff * lax.rsqrt(ms + EPS) * g3_ref[...])


def _ffn_call(xs, a_ff, s_ff, wup, cw, cb, wdn, g3, gate, nct):
    bsz, rows, _ = xs.shape
    nt = rows // TM
    mod = _mod_spec(nct)
    return pl.pallas_call(
        functools.partial(_ffn_kernel, nct=nct, nt=nt),
        grid=(bsz, nt),
        in_specs=_halo_specs(D, nct, nt) + [
            mod, mod, _const_spec((D, 2 * D_FF)), _const_spec((3, 2 * D_FF)), _const_spec((1, 2 * D_FF)),
            _const_spec((D_FF, D)), _const_spec((1, D)), mod],
        out_specs=pl.BlockSpec((1, TM, D), lambda b, i: (b, i, 0)),
        out_shape=jax.ShapeDtypeStruct((bsz, rows, D), F32),
        scratch_shapes=[pltpu.VMEM((TM + 2 * HALO, D), BF16), pltpu.VMEM((TM + 2 * HALO, 2 * FF_CW), F32),
                        pltpu.VMEM((TM, D_FF), BF16)],
        compiler_params=_cparams(("parallel", "arbitrary")),
    )(xs, xs, xs, a_ff, s_ff, wup, cw, cb, wdn, g3, gate)


def _rope_tables():
    n_freq = HD // 4
    inv = ROPE_THETA ** (-jnp.arange(n_freq, dtype=F32) / n_freq)
    rows = jnp.repeat(jnp.arange(SEQ // GRID_W, dtype=F32), GRID_W)
    cols = jnp.tile(jnp.arange(GRID_W, dtype=F32), SEQ // GRID_W)
    ang = jnp.concatenate([rows[:, None] * inv, cols[:, None] * inv], axis=-1)
    cos, sin = jnp.cos(ang), jnp.sin(ang)
    cos_l = jnp.tile(jnp.concatenate([cos, cos], axis=-1), (1, 2))
    sin_l = jnp.tile(jnp.concatenate([-sin, sin], axis=-1), (1, 2))
    cos_t = jnp.concatenate([jnp.ones((CTX, 128), F32), cos_l], axis=0)
    sin_t = jnp.concatenate([jnp.zeros((CTX, 128), F32), sin_l], axis=0)
    return cos_t, sin_t


def kernel(x, c, ctx, c_ctx, ada_w, ada_b, norm_g, w_in, w_out, da_lambda, da_subln_g, hg_lb_logits, hg_norm_g,
           gd_conv_w, gd_a_log, gd_dt_bias, gd_norm_g, ffn_w_up, ffn_conv_w, ffn_conv_b, ffn_w_down):
    bsz = x.shape[0]
    depth = ada_w.shape[0]
    assert x.shape == (bsz, SEQ, D) and ctx.shape == (bsz, CTX, D) and bsz <= 8
    cos_t, sin_t = _rope_tables()
    lb_w = jax.nn.softmax(hg_lb_logits.astype(F32), axis=0)
    lb_all = jnp.cumsum(lb_w, axis=0) - lb_w[0]

    cond = jnp.zeros((16, D), F32).at[:bsz].set(jax.nn.silu(c)).at[8].set(jax.nn.silu(c_ctx))
    mods = _ada_call(cond, ada_w, ada_b)

    xs = jnp.concatenate([ctx, x], axis=1)
    for layer in range(depth):
        need_ctx = layer < depth - 1
        lam_init = 0.8 - 0.6 * math.exp(-0.3 * layer)
        m = mods[layer].reshape(16, 6, D)
        mod = jnp.stack([jnp.broadcast_to(m[8], (bsz, 6, D)), m[:bsz]], axis=1)
        g = norm_g[layer].astype(F32)

        def vec(v):
            return v.reshape(bsz, 2, 1, D)

        a_in, s_in = vec(g[0] * (1.0 + mod[:, :, 1])), vec(mod[:, :, 0])
        a_ff, s_ff = vec(g[2] * (1.0 + mod[:, :, 4])), vec(mod[:, :, 3])
        gate1, gate2 = vec(mod[:, :, 2]), vec(mod[:, :, 5])

        wl = w_in[layer]
        wab = jnp.zeros((D, 128), F32).at[:, :4 * NH].set(wl[:, C_GDA:C_GDG])
        w = [wl[:, C_DAQ:C_DAV], wl[:, C_DAV:C_HGQ], wl[:, C_HGQ:C_GDQKV], wl[:, C_GDQKV:C_GDA], wab,
             wl[:, C_GDG:IN_COLS]]
        w = [t.astype(BF16) for t in w]
        lb = lb_all[layer].reshape(1, 2 * HW)
        nega = jnp.zeros((1, 128), F32).at[0, :2 * NH].set(-jnp.exp(gd_a_log[layer].astype(F32)).reshape(-1))
        dtb = jnp.zeros((1, 128), F32).at[0, :2 * NH].set(gd_dt_bias[layer].astype(F32).reshape(-1))
        consts = [1.0 - lb, jnp.log(jnp.maximum(lb, LB_FLOOR)), jnp.log1p(-lb),
                  gd_conv_w[layer].astype(F32), nega, dtb]
        (q, k, v, hq, hv, hk, hlf, hsg, gq, gk, gv, gab, gsg) = _inproj_call(
            xs, a_in, s_in, cos_t, sin_t, w, consts)

        lp = da_lambda[layer].astype(F32)
        lam = jnp.exp(jnp.sum(lp[0] * lp[1])) - jnp.exp(jnp.sum(lp[2] * lp[3])) + lam_init
        lam_arr = jnp.full((8, 128), lam, F32)
        g_arr = (da_subln_g[layer].astype(F32) * (1.0 - lam_init)).reshape(1, DV)
        oa_lat = _attn_call(q, k, v, lam_arr, g_arr, CTX, SEQ // TQ, T)
        oa_ctx = _attn_call(q, k, v, lam_arr, g_arr, 0, CTX // TQ, CTX) if need_ctx else oa_lat

        hof, hob = _hg_call(hq, hv, hk, hlf)
        gof, gob = _gd_call(gq, gk, gv, gab)

        hng = jnp.tile(hg_norm_g[layer].astype(F32), NH).reshape(1, HW)
        gng = jnp.tile(gd_norm_g[layer].astype(F32), NH).reshape(1, HW)
        xs = _outproj_call(xs, oa_lat, oa_ctx, hof, hob, hsg, gof, gob, gsg, w_out[layer].astype(BF16), hng, gng,
                           g[1].reshape(1, D), gate1, need_ctx)
        xs = _ffn_call(xs, a_ff, s_ff, ffn_w_up[layer].astype(BF16), ffn_conv_w[layer].astype(F32),
                       ffn_conv_b[layer].astype(F32).reshape(1, 2 * D_FF), ffn_w_down[layer].astype(BF16),
                       g[3].reshape(1, D), gate2, 1 if need_ctx else 0)
    return xs
```

```python
import functools
import math

import jax
import jax.numpy as jnp
import numpy as np
from jax import lax
from jax.experimental import pallas as pl
from jax.experimental.pallas import tpu as pltpu

F32 = jnp.float32
BF16 = jnp.bfloat16

D = 1024
CTX = 256
SEQ = 2048
T = CTX + SEQ
GRID_W = 64
ROPE_THETA = 10000.0
EPS = 1e-6
LB_FLOOR = 1e-30
NH = 4
HD = 64
DV = 128
HW = NH * HD
D_FF = 2816
TM = 256
NT = T // TM
HALO = 16
TQ = 128
ATT_SUB = 2
Q_SCALE = HD ** -0.5 * math.log2(math.e)
BLK = 16
CH = 64
FF_CW = 256
VMEM_LIMIT = 56 * 1024 * 1024

C_DAQ, C_DAK, C_DAV = 0, 512, 1024
C_HGQ, C_HGI, C_HGF, C_HGG = 1536, 1792, 2048, 2560
C_GDQKV, C_GDA, C_GDB, C_GDG = 2816, 3584, 3592, 3600
IN_COLS = 3856


def _cparams(sem):
    return pltpu.CompilerParams(dimension_semantics=sem, vmem_limit_bytes=VMEM_LIMIT)


def _const_spec(shape):
    n = len(shape)
    return pl.BlockSpec(shape, lambda *_: (0,) * n)


def _sigmoid(x):
    return 1.0 / (1.0 + jnp.exp(-x))


def _silu(x):
    return x * _sigmoid(x)


def _softplus(x):
    return jnp.maximum(x, 0.0) + jnp.log(1.0 + jnp.exp(-jnp.abs(x)))


def _dot(a, b):
    return jnp.dot(a, b, preferred_element_type=F32)


def _dot_nt(a, b):
    return lax.dot_general(a, b, (((1,), (1,)), ((), ())), preferred_element_type=F32)


def _dot_tn(a, b):
    return lax.dot_general(a, b, (((0,), (0,)), ((), ())), preferred_element_type=F32)


def _dot_hi(a, b):
    return jnp.dot(a, b, preferred_element_type=F32, precision=lax.Precision.HIGHEST)


def _head_sum(x, ones_bd):
    hi = x.astype(BF16)
    lo = (x - hi.astype(F32)).astype(BF16)
    return _dot(hi, ones_bd) + _dot(lo, ones_bd)


def _ones_bd():
    r = lax.broadcasted_iota(jnp.int32, (HW, HW), 0) // HD
    c = lax.broadcasted_iota(jnp.int32, (HW, HW), 1) // HD
    return jnp.where(r == c, 1.0, 0.0).astype(BF16)


def _bd_mask():
    r = lax.broadcasted_iota(jnp.int32, (HW, HW), 0) // HD
    c = lax.broadcasted_iota(jnp.int32, (HW, HW), 1) // HD
    return r == c


def _normed(x, a, s):
    ms = jnp.mean(x * x, axis=-1, keepdims=True)
    return (x * lax.rsqrt(ms + EPS)) * a + s


def _ada_kernel(c_ref, w_ref, b_ref, o_ref):
    o_ref[0] = _dot(c_ref[...].astype(BF16), w_ref[0].astype(BF16)) + b_ref[0]


def _ada_call(cond, ada_w, ada_b):
    depth = ada_w.shape[0]
    nc = 6 * D
    cw = 1536
    return pl.pallas_call(
        _ada_kernel,
        grid=(depth, nc // cw),
        in_specs=[pl.BlockSpec((16, D), lambda l, j: (0, 0)),
                  pl.BlockSpec((1, D, cw), lambda l, j: (l, 0, j)),
                  pl.BlockSpec((1, 1, cw), lambda l, j: (l, 0, j))],
        out_specs=pl.BlockSpec((1, 16, cw), lambda l, j: (l, 0, j)),
        out_shape=jax.ShapeDtypeStruct((depth, 16, nc), F32),
        compiler_params=_cparams(("arbitrary", "arbitrary")),
    )(cond, ada_w, ada_b.reshape(depth, 1, nc))


def _sel(i, nct):
    return jnp.where(i >= nct, 1, 0)


def _prev_ok(i, nct):
    return i > nct


def _next_ok(i, nct, nt):
    return jnp.logical_and(i >= nct, i <= nt - 2)


def _halo_specs(width, nct, nt):
    per = TM // HALO
    return [pl.BlockSpec((1, HALO, width), lambda b, i: (b, i * per - jnp.where(_prev_ok(i, nct), 1, 0), 0)),
            pl.BlockSpec((1, TM, width), lambda b, i: (b, i, 0)),
            pl.BlockSpec((1, HALO, width), lambda b, i: (b, (i + 1) * per - jnp.where(_next_ok(i, nct, nt), 0, 1), 0))]


def _mod_spec(nct):
    return pl.BlockSpec((1, 1, 1, D), lambda b, i: (b, _sel(i, nct), 0, 0))


def _fill_h(h_scr, xp_ref, xm_ref, xn_ref, a, s, i, nct, nt):
    hp = _normed(xp_ref[0], a, s)
    hn = _normed(xn_ref[0], a, s)
    h_scr[0:HALO, :] = jnp.where(_prev_ok(i, nct), hp, 0.0).astype(BF16)
    h_scr[HALO:HALO + TM, :] = _normed(xm_ref[0], a, s).astype(BF16)
    h_scr[HALO + TM:, :] = jnp.where(_next_ok(i, nct, nt), hn, 0.0).astype(BF16)


def _conv3(u_scr, w_ref):
    return (u_scr[HALO - 1:HALO - 1 + TM, :] * w_ref[0:1, :]
            + u_scr[HALO:HALO + TM, :] * w_ref[1:2, :]
            + u_scr[HALO + 1:HALO + 1 + TM, :] * w_ref[2:3, :])


def _inproj_kernel(xp_ref, xm_ref, xn_ref, a_ref, s_ref, cos_ref, sin_ref,
                   wqk_ref, wv_ref, whg_ref, wgq_ref, wab_ref, wgg_ref,
                   lb1m_ref, loglb_ref, log1mlb_ref, gconv_ref, nega_ref, dtb_ref,
                   q_ref, k_ref, v_ref, hq_ref, hv_ref, hk_ref, hlf_ref, hsg_ref,
                   gq_ref, gk_ref, gv_ref, gab_ref, gsg_ref,
                   h_scr, u_scr):
    i = pl.program_id(1)
    _fill_h(h_scr, xp_ref, xm_ref, xn_ref, a_ref[0, 0], s_ref[0, 0], i, 1, NT)
    h = h_scr[HALO:HALO + TM, :]

    z = _dot(h, wqk_ref[...])
    lane = lax.broadcasted_iota(jnp.int32, (TM, 128), 1)
    first_half = (lane % HD) < (HD // 2)
    cs, sn = cos_ref[...], sin_ref[...]
    for j in range(8):
        xj = z[:, j * 128:(j + 1) * 128]
        sw = jnp.where(first_half, pltpu.roll(xj, 128 - HD // 2, 1), pltpu.roll(xj, HD // 2, 1))
        r = xj * cs + sw * sn
        if j < 4:
            q_ref[0, :, j * 128:(j + 1) * 128] = (r * Q_SCALE).astype(BF16)
        else:
            k_ref[0, :, (j - 4) * 128:(j - 3) * 128] = r.astype(BF16)
    v_ref[0] = _dot(h, wv_ref[...]).astype(BF16)

    z = _dot(h, whg_ref[...])
    hq_ref[0] = _silu(z[:, 0:HW])
    hv_ref[0] = z[:, HW:2 * HW]
    f = z[:, 2 * HW:4 * HW]
    hk_ref[0] = lb1m_ref[...] * _sigmoid(-f)
    a = loglb_ref[...]
    b = log1mlb_ref[...] - _softplus(-f)
    hlf_ref[0] = jnp.maximum(a, b) + jnp.log(1.0 + jnp.exp(-jnp.abs(a - b)))
    hsg_ref[0] = _silu(z[:, 4 * HW:5 * HW])

    u_scr[...] = _dot(h_scr[...], wgq_ref[...])
    y = _silu(_conv3(u_scr, gconv_ref))
    ones_bd = _ones_bd()
    qg, kg = y[:, 0:HW], y[:, HW:2 * HW]
    gq_ref[0] = qg * lax.rsqrt(_head_sum(qg * qg, ones_bd) + EPS) * (HD ** -0.5)
    gk_ref[0] = kg * lax.rsqrt(_head_sum(kg * kg, ones_bd) + EPS)
    gv_ref[0] = y[:, 2 * HW:3 * HW]
    z = _dot(h, wab_ref[...])
    gab_ref[0] = jnp.where(lane < 2 * NH, nega_ref[...] * _softplus(z + dtb_ref[...]), _sigmoid(z))
    gsg_ref[0] = _silu(_dot(h, wgg_ref[...]))


def _inproj_call(xs, a_in, s_in, cos_t, sin_t, w, consts):
    bsz = xs.shape[0]
    row = lambda wd: pl.BlockSpec((1, TM, wd), lambda b, i: (b, i, 0))
    tab = pl.BlockSpec((TM, 128), lambda b, i: (i, 0))
    out_w = [(512, BF16), (512, BF16), (512, BF16), (HW, F32), (HW, F32), (2 * HW, F32), (2 * HW, F32),
             (HW, F32), (HW, F32), (HW, F32), (HW, F32), (128, F32), (HW, F32)]
    return pl.pallas_call(
        _inproj_kernel,
        grid=(bsz, NT),
        in_specs=_halo_specs(D, 1, NT) + [_mod_spec(1), _mod_spec(1), tab, tab]
        + [_const_spec(x.shape) for x in w] + [_const_spec(x.shape) for x in consts],
        out_specs=[row(wd) for wd, _ in out_w],
        out_shape=[jax.ShapeDtypeStruct((bsz, T, wd), dt) for wd, dt in out_w],
        scratch_shapes=[pltpu.VMEM((TM + 2 * HALO, D), BF16), pltpu.VMEM((TM + 2 * HALO, 3 * HW), F32)],
        compiler_params=_cparams(("parallel", "arbitrary")),
    )(xs, xs, xs, a_in, s_in, cos_t, sin_t, *w, *consts)


def _attn_kernel(q_ref, k_ref, v_ref, lam_ref, g_ref, o_ref, *, n_sub):
    k, v = k_ref[0], v_ref[0]
    lane = lax.broadcasted_iota(jnp.int32, (TQ, 128), 1)
    zero = jnp.zeros((TQ, 128), BF16)
    ix = range(n_sub)
    q = [q_ref[0, j * TQ:(j + 1) * TQ, :] for j in ix]
    qs = [jnp.concatenate([jnp.where(lane < HD, q[j], zero), jnp.where(lane >= HD, q[j], zero)], axis=0) for j in ix]
    st = [_dot_nt(k, qs[j]) for j in ix]
    m = [jnp.max(st[j], axis=0, keepdims=True) for j in ix]
    p = [jnp.exp2(st[j] - m[j]) for j in ix]
    l = [jnp.sum(p[j], axis=0, keepdims=True) for j in ix]
    ot = [_dot_tn(v, p[j].astype(BF16)) * (1.0 / l[j]) for j in ix]
    for j in ix:
        d = ot[j][:, :TQ] - lam_ref[0:1, :] * ot[j][:, TQ:]
        ms = jnp.mean(d * d, axis=0, keepdims=True)
        dn = d * lax.rsqrt(ms + EPS)
        o_ref[0, j * TQ:(j + 1) * TQ, :] = (dn.T * g_ref[...]).astype(BF16)


def _attn_call(q, k, v, lam_arr, g_arr, q_row0, n_rows, tk):
    bsz = q.shape[0]
    n_sub = min(ATT_SUB, n_rows // TQ)
    tq = n_sub * TQ
    assert q_row0 % tq == 0 and n_rows % tq == 0
    qoff = q_row0 // tq
    return pl.pallas_call(
        functools.partial(_attn_kernel, n_sub=n_sub),
        grid=(bsz, NH, n_rows // tq),
        in_specs=[pl.BlockSpec((1, tq, 128), lambda b, h, i: (b, qoff + i, h)),
                  pl.BlockSpec((1, tk, 128), lambda b, h, i: (b, 0, h)),
                  pl.BlockSpec((1, tk, 128), lambda b, h, i: (b, 0, h)),
                  _const_spec((8, 128)), _const_spec((1, 128))],
        out_specs=pl.BlockSpec((1, tq, 128), lambda b, h, i: (b, i, h)),
        out_shape=jax.ShapeDtypeStruct((bsz, n_rows, 512), BF16),
        compiler_params=_cparams(("parallel", "parallel", "arbitrary")),
    )(q, k, v, lam_arr, g_arr)


def _bwd_tile(i):
    return jnp.where(i == 0, 0, NT - i)


def _shift_rows(x, n, reverse):
    if n == 0:
        return x
    rows = x.shape[0]
    return pltpu.roll(x, (rows - n) if reverse else n, 0)


def _hg_kernel(qf_ref, vf_ref, kf_ref, lf_ref, qb_ref, vb_ref, kb_ref, lb_ref,
               of_ref, ob_ref, sf_scr, sb_scr):
    @pl.when(pl.program_id(1) == 0)
    def _():
        sf_scr[...] = jnp.zeros_like(sf_scr)
        sb_scr[...] = jnp.zeros_like(sb_scr)

    q = (qf_ref[0], qb_ref[0])
    k = (kf_ref[0], kb_ref[0])
    v = (vf_ref[0], vb_ref[0])
    logf = (lf_ref[0], lb_ref[0])
    o_refs = (of_ref, ob_ref)
    dirs = (0, 1)

    r = lax.broadcasted_iota(jnp.int32, (TM, TM), 0)
    c = lax.broadcasted_iota(jnp.int32, (TM, TM), 1)
    same = (r // BLK) == (c // BLK)
    tri = [jnp.where(jnp.logical_and(same, (c >= r) if d else (c <= r)), 1.0, 0.0).astype(BF16) for d in dirs]
    blk = jnp.where(same, 1.0, 0.0).astype(BF16)
    bl = [_dot_01(tri[d], logf[d]) for d in dirs]
    tot = [_dot_01(blk, logf[d]) for d in dirs]
    qdb = [(q[d] * jnp.exp(bl[d])).astype(BF16) for d in dirs]
    kd = [(k[d] * jnp.exp(tot[d] - bl[d])).astype(BF16) for d in dirs]
    e_blk = [jnp.exp(tot[d]) for d in dirs]
    f = [jnp.exp(logf[d]) for d in dirs]

    pos = lax.broadcasted_iota(jnp.int32, (TM, HW), 0) % BLK
    ones_bd = _ones_bd()
    e = [None, None]
    o_band = [jnp.zeros((TM, HW), F32), jnp.zeros((TM, HW), F32)]
    for n in range(BLK):
        for d in dirs:
            if n == 1:
                e[d] = f[d]
            elif n > 1:
                e[d] = e[d] * _shift_rows(f[d], n - 1, d)
            pn = q[d] * _shift_rows(k[d], n, d)
            if n > 0:
                pn = pn * e[d]
            ok = (pos + n < BLK) if d else (pos >= n)
            pn = jnp.where(ok, pn, 0.0).astype(BF16)
            o_band[d] = o_band[d] + _dot(pn, ones_bd) * _shift_rows(v[d], n, d)

    bd = _bd_mask()
    vb = [v[d].astype(BF16) for d in dirs]
    nb = TM // BLK
    s = [sf_scr[...], sb_scr[...]]
    for j in range(nb):
        for d in dirs:
            ib = nb - 1 - j if d else j
            rows = slice(ib * BLK, (ib + 1) * BLK)
            o_refs[d][0, rows, :] = o_band[d][rows] + _dot_nt(qdb[d][rows], s[d].astype(BF16))
            u = _dot_tn(vb[d][rows], kd[d][rows])
            s[d] = s[d] * e_blk[d][ib * BLK:ib * BLK + 1, :] + jnp.where(bd, u, 0.0)
    sf_scr[...] = s[0]
    sb_scr[...] = s[1]


def _hg_call(hq, hv, hk, hlf):
    bsz = hq.shape[0]
    f = lambda b, i: (b, i, 0)
    g0 = lambda b, i: (b, _bwd_tile(i), 0)
    g1 = lambda b, i: (b, _bwd_tile(i), 1)
    blk = lambda m: pl.BlockSpec((1, TM, HW), m)
    return pl.pallas_call(
        _hg_kernel,
        grid=(bsz, NT),
        in_specs=[blk(f), blk(f), blk(f), blk(f), blk(g0), blk(g0), blk(g1), blk(g1)],
        out_specs=[blk(f), blk(g0)],
        out_shape=[jax.ShapeDtypeStruct((bsz, T, HW), F32)] * 2,
        scratch_shapes=[pltpu.VMEM((HW, HW), F32)] * 2,
        compiler_params=_cparams(("parallel", "arbitrary")),
    )(hq, hv, hk, hlf, hq, hv, hk, hlf)


def _bd4(x, bd):
    return jnp.where(bd, jnp.concatenate([x.astype(BF16)] * NH, axis=0), jnp.zeros((), BF16))


def _split3(x):
    h = x.astype(BF16)
    r1 = x - h.astype(F32)
    m = r1.astype(BF16)
    l = (r1 - m.astype(F32)).astype(BF16)
    return jnp.concatenate([h, m, l], axis=0)


def _dot_01(sel01, x):
    return _dot(jnp.concatenate([sel01] * 3, axis=1), _split3(x))


def _gd_prep(q, k, v, ab, d, bd):
    reverse = d == 1
    n_ch = len(q)
    grp = lax.broadcasted_iota(jnp.int32, (CH, HW), 1) // HD
    r = lax.broadcasted_iota(jnp.int32, (CH, CH), 0)
    c = lax.broadcasted_iota(jnp.int32, (CH, CH), 1)
    tri = jnp.where((c >= r) if reverse else (c <= r), 1.0, 0.0).astype(BF16)
    t_i = lax.broadcasted_iota(jnp.int32, (CH, HW), 0)
    s_i = lax.broadcasted_iota(jnp.int32, (CH, HW), 1) % HD
    later = (t_i < s_i) if reverse else (t_i > s_i)
    valid = (s_i >= t_i) if reverse else (s_i <= t_i)
    strict = (s_i > t_i) if reverse else (s_i < t_i)
    eye = jnp.where(t_i == s_i, 1.0, 0.0)

    def off_block(m):
        t_blk, s_blk = t_i // m, s_i // m
        pair = (t_blk // 2) == (s_blk // 2)
        if reverse:
            return jnp.logical_and(pair, jnp.logical_and(t_blk % 2 == 0, s_blk % 2 == 1))
        return jnp.logical_and(pair, jnp.logical_and(t_blk % 2 == 1, s_blk % 2 == 0))

    def widen(a, col0):
        out = jnp.zeros((CH, HW), F32)
        for h in range(NH):
            out = jnp.where(grp == h, jnp.broadcast_to(a[:, col0 + h:col0 + h + 1], (CH, HW)), out)
        return out

    ix = range(n_ch)
    la = [widen(ab[j], d * NH) for j in ix]
    beta = [widen(ab[j], 2 * NH + d * NH) for j in ix]
    gc = [_dot_01(tri, la[j]) for j in ix]
    diff = [_dot_01(tri, jnp.where(later, la[j], 0.0)) for j in ix]
    kb = [_bd4(k[j], bd) for j in ix]
    kk = [_dot_nt(k[j].astype(BF16), kb[j]) for j in ix]
    qk = [_dot_nt(q[j].astype(BF16), kb[j]) for j in ix]
    dm = [jnp.where(valid, jnp.exp(jnp.minimum(diff[j], 0.0)), 0.0) for j in ix]
    n = [jnp.where(strict, beta[j] * kk[j] * dm[j], 0.0) for j in ix]

    tinv = [eye - jnp.where(off_block(1), n[j], 0.0) for j in ix]
    m = 2
    while m < CH:
        y = [_dot(tinv[j].astype(BF16), _bd4(jnp.where(off_block(m), n[j], 0.0), bd)) for j in ix]
        tinv = [tinv[j] - _dot(y[j].astype(BF16), _bd4(tinv[j], bd)) for j in ix]
        m *= 2
    tb = [t.astype(BF16) for t in tinv]
    eg = [jnp.exp(g) for g in gc]
    u = [_dot(tb[j], _bd4(v[j] * beta[j], bd)) for j in ix]
    w = [_dot(tb[j], _bd4(k[j] * beta[j] * eg[j], bd)) for j in ix]
    sc = [(qk[j] * dm[j]).astype(BF16) for j in ix]
    qeff = [(q[j] * eg[j] - _dot(sc[j], _bd4(w[j], bd))).astype(BF16) for j in ix]
    oc = [_dot(sc[j], _bd4(u[j], bd)) for j in ix]
    last = 0 if reverse else CH - 1
    gl = [g[last:last + 1, :] for g in gc]
    kd = [(k[j] * jnp.exp(gl[j] - gc[j])).astype(BF16) for j in ix]
    kw = [jnp.where(bd, _dot_tn(kd[j], w[j].astype(BF16)), 0.0).astype(BF16) for j in ix]
    ku = [jnp.where(bd, _dot_tn(kd[j], u[j].astype(BF16)), 0.0) for j in ix]
    a = [jnp.exp(g) for g in gl]
    return qeff, oc, kw, ku, a


def _gd_kernel(qf_ref, kf_ref, vf_ref, abf_ref, qb_ref, kb_ref, vb_ref, abb_ref,
               of_ref, ob_ref, sf_scr, sb_scr):
    @pl.when(pl.program_id(1) == 0)
    def _():
        sf_scr[...] = jnp.zeros_like(sf_scr)
        sb_scr[...] = jnp.zeros_like(sb_scr)

    bd = _bd_mask()
    n_ch = TM // CH
    order = (list(range(n_ch)), list(range(n_ch - 1, -1, -1)))
    refs = ((qf_ref, kf_ref, vf_ref, abf_ref, of_ref, sf_scr), (qb_ref, kb_ref, vb_ref, abb_ref, ob_ref, sb_scr))
    prep = []
    for d in range(2):
        q_ref, k_ref, v_ref, ab_ref = refs[d][:4]
        rows = [slice(ic * CH, (ic + 1) * CH) for ic in order[d]]
        prep.append(_gd_prep([q_ref[0, r, :] for r in rows], [k_ref[0, r, :] for r in rows],
                             [v_ref[0, r, :] for r in rows], [ab_ref[0, r, :] for r in rows], d, bd))

    s = [sf_scr[...], sb_scr[...]]
    for j in range(n_ch):
        for d in range(2):
            qeff, oc, kw, ku, a = (t[j] for t in prep[d])
            ic = order[d][j]
            sb = s[d].astype(BF16)
            refs[d][4][0, ic * CH:(ic + 1) * CH, :] = _dot(qeff, sb) + oc
            s[d] = s[d] * a - _dot(kw, sb) + ku
    sf_scr[...] = s[0]
    sb_scr[...] = s[1]


def _gd_call(gq, gk, gv, gab):
    bsz = gq.shape[0]
    f = lambda b, i: (b, i, 0)
    g0 = lambda b, i: (b, _bwd_tile(i), 0)
    blk = lambda m: pl.BlockSpec((1, TM, HW), m)
    abs_ = lambda m: pl.BlockSpec((1, TM, 128), m)
    return pl.pallas_call(
        _gd_kernel,
        grid=(bsz, NT),
        in_specs=[blk(f), blk(f), blk(f), abs_(f), blk(g0), blk(g0), blk(g0), abs_(g0)],
        out_specs=[blk(f), blk(g0)],
        out_shape=[jax.ShapeDtypeStruct((bsz, T, HW), F32)] * 2,
        scratch_shapes=[pltpu.VMEM((HW, HW), F32)] * 2,
        compiler_params=_cparams(("parallel", "arbitrary")),
    )(gq, gk, gv, gab, gq, gk, gv, gab)


def _outproj_kernel(x_ref, oal_ref, oac_ref, hof_ref, hob_ref, hsg_ref, gof_ref, gob_ref, gsg_ref,
                    wo_ref, hng_ref, gng_ref, g1_ref, gate_ref, o_ref, *, with_ctx):
    ones_bd = _ones_bd()
    oa = oal_ref[0]
    if with_ctx:
        oa = jnp.where(pl.program_id(1) == 0, oac_ref[0], oa)

    def finish(of_ref, ob_ref, sg_ref, ng_ref):
        o = of_ref[0] + ob_ref[0]
        ms = _head_sum(o * o, ones_bd) * (1.0 / HD)
        return (o * lax.rsqrt(ms + EPS) * ng_ref[...] * sg_ref[0]).astype(BF16)

    ob = finish(hof_ref, hob_ref, hsg_ref, hng_ref)
    oc = finish(gof_ref, gob_ref, gsg_ref, gng_ref)
    mix = (_dot(oa, wo_ref[0:512, :]) + _dot(ob, wo_ref[512:768, :]) + _dot(oc, wo_ref[768:1024, :]))
    ms = jnp.mean(mix * mix, axis=-1, keepdims=True)
    o_ref[0] = x_ref[0] + gate_ref[0, 0] * (mix * lax.rsqrt(ms + EPS) * g1_ref[...])


def _outproj_call(xs, oa_lat, oa_ctx, hof, hob, hsg, gof, gob, gsg, wo, hng, gng, g1, gate, with_ctx):
    bsz = xs.shape[0]
    t0 = 0 if with_ctx else 1
    row = lambda wd: pl.BlockSpec((1, TM, wd), lambda b, i: (b, i + t0, 0))
    oal = pl.BlockSpec((1, TM, 512), lambda b, i: (b, jnp.maximum(i + t0 - 1, 0), 0))
    oac = pl.BlockSpec((1, TM, 512), lambda b, i: (b, 0, 0))
    mod = pl.BlockSpec((1, 1, 1, D), lambda b, i: (b, _sel(i + t0, 1), 0, 0))
    return pl.pallas_call(
        functools.partial(_outproj_kernel, with_ctx=with_ctx),
        grid=(bsz, NT - t0),
        in_specs=[row(D), oal, oac, row(HW), row(HW), row(HW), row(HW), row(HW), row(HW),
                  _const_spec((D, D)), _const_spec((1, HW)), _const_spec((1, HW)), _const_spec((1, D)), mod],
        out_specs=pl.BlockSpec((1, TM, D), lambda b, i: (b, i, 0)),
        out_shape=jax.ShapeDtypeStruct((bsz, (NT - t0) * TM, D), F32),
        compiler_params=_cparams(("parallel", "arbitrary")),
    )(xs, oa_lat, oa_ctx, hof, hob, hsg, gof, gob, gsg, wo, hng, gng, g1, gate)


def _ffn_kernel(xp_ref, xm_ref, xn_ref, a_ref, s_ref, wup_ref, cw_ref, cb_ref, wdn_ref, g3_ref, gate_ref,
                o_ref, h_scr, u_scr, act_scr, *, nct, nt):
    _fill_h(h_scr, xp_ref, xm_ref, xn_ref, a_ref[0, 0], s_ref[0, 0], pl.program_id(1), nct, nt)
    h = h_scr[...]
    for cidx in range(D_FF // FF_CW):
        lo = cidx * FF_CW
        u_scr[:, 0:FF_CW] = _dot(h, wup_ref[:, lo:lo + FF_CW])
        u_scr[:, FF_CW:2 * FF_CW] = _dot(h, wup_ref[:, D_FF + lo:D_FF + lo + FF_CW])
        wg = cw_ref[:, lo:lo + FF_CW]
        wv = cw_ref[:, D_FF + lo:D_FF + lo + FF_CW]
        g = (u_scr[HALO - 1:HALO - 1 + TM, 0:FF_CW] * wg[0:1] + u_scr[HALO:HALO + TM, 0:FF_CW] * wg[1:2]
             + u_scr[HALO + 1:HALO + 1 + TM, 0:FF_CW] * wg[2:3] + cb_ref[:, lo:lo + FF_CW])
        vv = (u_scr[HALO - 1:HALO - 1 + TM, FF_CW:] * wv[0:1] + u_scr[HALO:HALO + TM, FF_CW:] * wv[1:2]
              + u_scr[HALO + 1:HALO + 1 + TM, FF_CW:] * wv[2:3] + cb_ref[:, D_FF + lo:D_FF + lo + FF_CW])
        act_scr[:, lo:lo + FF_CW] = (_silu(g) * vv).astype(BF16)
    ff = _dot(act_scr[...], wdn_ref[...])
    ms = jnp.mean(ff * ff, axis=-1, keepdims=True)
    o_ref[0] = xm_ref[0] + gate_ref[0, 0] * (ff * lax.rsqrt(ms + EPS) * g3_ref[...])


def _ffn_call(xs, a_ff, s_ff, wup, cw, cb, wdn, g3, gate, nct):
    bsz, rows, _ = xs.shape
    nt = rows // TM
    mod = _mod_spec(nct)
    return pl.pallas_call(
        functools.partial(_ffn_kernel, nct=nct, nt=nt),
        grid=(bsz, nt),
        in_specs=_halo_specs(D, nct, nt) + [
            mod, mod, _const_spec((D, 2 * D_FF)), _const_spec((3, 2 * D_FF)), _const_spec((1, 2 * D_FF)),
            _const_spec((D_FF, D)), _const_spec((1, D)), mod],
        out_specs=pl.BlockSpec((1, TM, D), lambda b, i: (b, i, 0)),
        out_shape=jax.ShapeDtypeStruct((bsz, rows, D), F32),
        scratch_shapes=[pltpu.VMEM((TM + 2 * HALO, D), BF16), pltpu.VMEM((TM + 2 * HALO, 2 * FF_CW), F32),
                        pltpu.VMEM((TM, D_FF), BF16)],
        compiler_params=_cparams(("parallel", "arbitrary")),
    )(xs, xs, xs, a_ff, s_ff, wup, cw, cb, wdn, g3, gate)


def _rope_tables():
    n_freq = HD // 4
    inv = ROPE_THETA ** (-jnp.arange(n_freq, dtype=F32) / n_freq)
    rows = jnp.repeat(jnp.arange(SEQ // GRID_W, dtype=F32), GRID_W)
    cols = jnp.tile(jnp.arange(GRID_W, dtype=F32), SEQ // GRID_W)
    ang = jnp.concatenate([rows[:, None] * inv, cols[:, None] * inv], axis=-1)
    cos, sin = jnp.cos(ang), jnp.sin(ang)
    cos_l = jnp.tile(jnp.concatenate([cos, cos], axis=-1), (1, 2))
    sin_l = jnp.tile(jnp.concatenate([-sin, sin], axis=-1), (1, 2))
    cos_t = jnp.concatenate([jnp.ones((CTX, 128), F32), cos_l], axis=0)
    sin_t = jnp.concatenate([jnp.zeros((CTX, 128), F32), sin_l], axis=0)
    return cos_t, sin_t


def kernel(x, c, ctx, c_ctx, ada_w, ada_b, norm_g, w_in, w_out, da_lambda, da_subln_g, hg_lb_logits, hg_norm_g,
           gd_conv_w, gd_a_log, gd_dt_bias, gd_norm_g, ffn_w_up, ffn_conv_w, ffn_conv_b, ffn_w_down):
    bsz = x.shape[0]
    depth = ada_w.shape[0]
    assert x.shape == (bsz, SEQ, D) and ctx.shape == (bsz, CTX, D) and bsz <= 8
    cos_t, sin_t = _rope_tables()
    lb_w = jax.nn.softmax(hg_lb_logits.astype(F32), axis=0)
    lb_all = jnp.cumsum(lb_w, axis=0) - lb_w[0]

    cond = jnp.zeros((16, D), F32).at[:bsz].set(jax.nn.silu(c)).at[8].set(jax.nn.silu(c_ctx))
    mods = _ada_call(cond, ada_w, ada_b)

    xs = jnp.concatenate([ctx, x], axis=1)
    for layer in range(depth):
        need_ctx = layer < depth - 1
        lam_init = 0.8 - 0.6 * math.exp(-0.3 * layer)
        m = mods[layer].reshape(16, 6, D)
        mod = jnp.stack([jnp.broadcast_to(m[8], (bsz, 6, D)), m[:bsz]], axis=1)
        g = norm_g[layer].astype(F32)

        def vec(v):
            return v.reshape(bsz, 2, 1, D)

        a_in, s_in = vec(g[0] * (1.0 + mod[:, :, 1])), vec(mod[:, :, 0])
        a_ff, s_ff = vec(g[2] * (1.0 + mod[:, :, 4])), vec(mod[:, :, 3])
        gate1, gate2 = vec(mod[:, :, 2]), vec(mod[:, :, 5])

        wl = w_in[layer]
        wab = jnp.zeros((D, 128), F32).at[:, :4 * NH].set(wl[:, C_GDA:C_GDG])
        w = [wl[:, C_DAQ:C_DAV], wl[:, C_DAV:C_HGQ], wl[:, C_HGQ:C_GDQKV], wl[:, C_GDQKV:C_GDA], wab,
             wl[:, C_GDG:IN_COLS]]
        w = [t.astype(BF16) for t in w]
        lb = lb_all[layer].reshape(1, 2 * HW)
        nega = jnp.zeros((1, 128), F32).at[0, :2 * NH].set(-jnp.exp(gd_a_log[layer].astype(F32)).reshape(-1))
        dtb = jnp.zeros((1, 128), F32).at[0, :2 * NH].set(gd_dt_bias[layer].astype(F32).reshape(-1))
        consts = [1.0 - lb, jnp.log(jnp.maximum(lb, LB_FLOOR)), jnp.log1p(-lb),
                  gd_conv_w[layer].astype(F32), nega, dtb]
        (q, k, v, hq, hv, hk, hlf, hsg, gq, gk, gv, gab, gsg) = _inproj_call(
            xs, a_in, s_in, cos_t, sin_t, w, consts)

        lp = da_lambda[layer].astype(F32)
        lam = jnp.exp(jnp.sum(lp[0] * lp[1])) - jnp.exp(jnp.sum(lp[2] * lp[3])) + lam_init
        lam_arr = jnp.full((8, 128), lam, F32)
        g_arr = (da_subln_g[layer].astype(F32) * (1.0 - lam_init)).reshape(1, DV)
        oa_lat = _attn_call(q, k, v, lam_arr, g_arr, CTX, SEQ, T)
        oa_ctx = _attn_call(q, k, v, lam_arr, g_arr, 0, CTX, CTX) if need_ctx else oa_lat

        hof, hob = _hg_call(hq, hv, hk, hlf)
        gof, gob = _gd_call(gq, gk, gv, gab)

        hng = jnp.tile(hg_norm_g[layer].astype(F32), NH).reshape(1, HW)
        gng = jnp.tile(gd_norm_g[layer].astype(F32), NH).reshape(1, HW)
        xs = _outproj_call(xs, oa_lat, oa_ctx, hof, hob, hsg, gof, gob, gsg, w_out[layer].astype(BF16), hng, gng,
                           g[1].reshape(1, D), gate1, need_ctx)
        xs = _ffn_call(xs, a_ff, s_ff, ffn_w_up[layer].astype(BF16), ffn_conv_w[layer].astype(F32),
                       ffn_conv_b[layer].astype(F32).reshape(1, 2 * D_FF), ffn_w_down[layer].astype(BF16),
                       g[3].reshape(1, D), gate2, 1 if need_ctx else 0)
    return xs
```

```python
import functools
import math

import jax
import jax.numpy as jnp
import numpy as np
from jax import lax
from jax.experimental import pallas as pl
from jax.experimental.pallas import tpu as pltpu

F32 = jnp.float32
BF16 = jnp.bfloat16

D = 1024
CTX = 256
SEQ = 2048
T = CTX + SEQ
GRID_W = 64
ROPE_THETA = 10000.0
EPS = 1e-6
LB_FLOOR = 1e-30
NH = 4
HD = 64
DV = 128
HW = NH * HD
D_FF = 2816
TM = 256
NT = T // TM
HALO = 16
TQ = 128
ATT_QB = 256
ATT_QIN = 2
Q_SCALE = HD ** -0.5 * math.log2(math.e)
BLK = 16
CH = 64
FF_CW = 256
VMEM_LIMIT = 56 * 1024 * 1024

C_DAQ, C_DAK, C_DAV = 0, 512, 1024
C_HGQ, C_HGI, C_HGF, C_HGG = 1536, 1792, 2048, 2560
C_GDQKV, C_GDA, C_GDB, C_GDG = 2816, 3584, 3592, 3600
IN_COLS = 3856


def _cparams(sem):
    return pltpu.CompilerParams(dimension_semantics=sem, vmem_limit_bytes=VMEM_LIMIT)


def _const_spec(shape):
    n = len(shape)
    return pl.BlockSpec(shape, lambda *_: (0,) * n)


def _sigmoid(x):
    return 1.0 / (1.0 + jnp.exp(-x))


def _silu(x):
    return x * _sigmoid(x)


def _softplus(x):
    return jnp.maximum(x, 0.0) + jnp.log(1.0 + jnp.exp(-jnp.abs(x)))


def _dot(a, b):
    return jnp.dot(a, b, preferred_element_type=F32)


def _dot_nt(a, b):
    return lax.dot_general(a, b, (((1,), (1,)), ((), ())), preferred_element_type=F32)


def _dot_tn(a, b):
    return lax.dot_general(a, b, (((0,), (0,)), ((), ())), preferred_element_type=F32)


def _dot_hi(a, b):
    return jnp.dot(a, b, preferred_element_type=F32, precision=lax.Precision.HIGHEST)


def _head_sum(x, ones_bd):
    hi = x.astype(BF16)
    lo = (x - hi.astype(F32)).astype(BF16)
    return _dot(hi, ones_bd) + _dot(lo, ones_bd)


def _ones_bd():
    r = lax.broadcasted_iota(jnp.int32, (HW, HW), 0) // HD
    c = lax.broadcasted_iota(jnp.int32, (HW, HW), 1) // HD
    return jnp.where(r == c, 1.0, 0.0).astype(BF16)


def _bd_mask():
    r = lax.broadcasted_iota(jnp.int32, (HW, HW), 0) // HD
    c = lax.broadcasted_iota(jnp.int32, (HW, HW), 1) // HD
    return r == c


def _normed(x, a, s):
    ms = jnp.mean(x * x, axis=-1, keepdims=True)
    return (x * lax.rsqrt(ms + EPS)) * a + s


def _ada_kernel(c_ref, w_ref, b_ref, o_ref):
    o_ref[0] = _dot(c_ref[...].astype(BF16), w_ref[0].astype(BF16)) + b_ref[0]


def _ada_call(cond, ada_w, ada_b):
    depth = ada_w.shape[0]
    nc = 6 * D
    cw = 1536
    return pl.pallas_call(
        _ada_kernel,
        grid=(depth, nc // cw),
        in_specs=[pl.BlockSpec((16, D), lambda l, j: (0, 0)),
                  pl.BlockSpec((1, D, cw), lambda l, j: (l, 0, j)),
                  pl.BlockSpec((1, 1, cw), lambda l, j: (l, 0, j))],
        out_specs=pl.BlockSpec((1, 16, cw), lambda l, j: (l, 0, j)),
        out_shape=jax.ShapeDtypeStruct((depth, 16, nc), F32),
        compiler_params=_cparams(("arbitrary", "arbitrary")),
    )(cond, ada_w, ada_b.reshape(depth, 1, nc))


def _sel(i, nct):
    return jnp.where(i >= nct, 1, 0)


def _prev_ok(i, nct):
    return i > nct


def _next_ok(i, nct, nt):
    return jnp.logical_and(i >= nct, i <= nt - 2)


def _halo_specs(width, nct, nt):
    per = TM // HALO
    return [pl.BlockSpec((1, HALO, width), lambda b, i: (b, i * per - jnp.where(_prev_ok(i, nct), 1, 0), 0)),
            pl.BlockSpec((1, TM, width), lambda b, i: (b, i, 0)),
            pl.BlockSpec((1, HALO, width), lambda b, i: (b, (i + 1) * per - jnp.where(_next_ok(i, nct, nt), 0, 1), 0))]


def _mod_spec(nct):
    return pl.BlockSpec((1, 1, 1, D), lambda b, i: (b, _sel(i, nct), 0, 0))


def _fill_h(h_scr, xp, xm, xn, a, s, i, nct, nt):
    hp = _normed(xp, a, s)
    hn = _normed(xn, a, s)
    h_scr[0:HALO, :] = jnp.where(_prev_ok(i, nct), hp, 0.0).astype(BF16)
    h_scr[HALO:HALO + TM, :] = _normed(xm, a, s).astype(BF16)
    h_scr[HALO + TM:, :] = jnp.where(_next_ok(i, nct, nt), hn, 0.0).astype(BF16)


def _conv3(u_scr, w_ref):
    return (u_scr[HALO - 1:HALO - 1 + TM, :] * w_ref[0:1, :]
            + u_scr[HALO:HALO + TM, :] * w_ref[1:2, :]
            + u_scr[HALO + 1:HALO + 1 + TM, :] * w_ref[2:3, :])


def _inproj_kernel(*refs, split):
    if split:
        ctx_ref, refs = refs[0], refs[1:]
    (xp_ref, xm_ref, xn_ref, a_ref, s_ref, cos_ref, sin_ref,
     wqk_ref, wv_ref, whg_ref, wgq_ref, wab_ref, wgg_ref,
     lb1m_ref, loglb_ref, log1mlb_ref, gconv_ref, nega_ref, dtb_ref,
     q_ref, k_ref, v_ref, hq_ref, hv_ref, hk_ref, hlf_ref, hsg_ref,
     gq_ref, gk_ref, gv_ref, gab_ref, gsg_ref,
     h_scr, u_scr) = refs
    i = pl.program_id(1)
    xm = jnp.where(i == 0, ctx_ref[0], xm_ref[0]) if split else xm_ref[0]
    _fill_h(h_scr, xp_ref[0], xm, xn_ref[0], a_ref[0, 0], s_ref[0, 0], i, 1, NT)
    h = h_scr[HALO:HALO + TM, :]

    z = _dot(h, wqk_ref[...])
    z_v = _dot(h, wv_ref[...])
    z_hg = _dot(h, whg_ref[...])
    u_scr[...] = _dot(h_scr[...], wgq_ref[...])
    z_ab = _dot(h, wab_ref[...])
    z_gg = _dot(h, wgg_ref[...])

    lane = lax.broadcasted_iota(jnp.int32, (TM, 128), 1)
    first_half = (lane % HD) < (HD // 2)
    cs, sn = cos_ref[...], sin_ref[...]
    for j in range(8):
        xj = z[:, j * 128:(j + 1) * 128]
        sw = jnp.where(first_half, pltpu.roll(xj, 128 - HD // 2, 1), pltpu.roll(xj, HD // 2, 1))
        r = xj * cs + sw * sn
        if j < 4:
            q_ref[0, :, j * 128:(j + 1) * 128] = (r * Q_SCALE).astype(BF16)
        else:
            k_ref[0, :, (j - 4) * 128:(j - 3) * 128] = r.astype(BF16)
    v_ref[0] = z_v.astype(BF16)

    z = z_hg
    hq_ref[0] = _silu(z[:, 0:HW]).astype(BF16)
    hv_ref[0] = z[:, HW:2 * HW].astype(BF16)
    f = z[:, 2 * HW:4 * HW]
    hk_ref[0] = (lb1m_ref[...] * _sigmoid(-f)).astype(BF16)
    a = loglb_ref[...]
    b = log1mlb_ref[...] - _softplus(-f)
    hlf_ref[0] = jnp.maximum(a, b) + jnp.log(1.0 + jnp.exp(-jnp.abs(a - b)))
    hsg_ref[0] = _silu(z[:, 4 * HW:5 * HW]).astype(BF16)

    y = _silu(_conv3(u_scr, gconv_ref))
    ones_bd = _ones_bd()
    qg, kg = y[:, 0:HW], y[:, HW:2 * HW]
    gq_ref[0] = (qg * lax.rsqrt(_head_sum(qg * qg, ones_bd) + EPS) * (HD ** -0.5)).astype(BF16)
    gk_ref[0] = (kg * lax.rsqrt(_head_sum(kg * kg, ones_bd) + EPS)).astype(BF16)
    gv_ref[0] = y[:, 2 * HW:3 * HW].astype(BF16)
    gab_ref[0] = jnp.where(lane < 2 * NH, nega_ref[...] * _softplus(z_ab + dtb_ref[...]), _sigmoid(z_ab))
    gsg_ref[0] = _silu(z_gg).astype(BF16)


def _split_x_specs():
    per = TM // HALO
    lat = lambda i: jnp.maximum(i - 1, 0)
    return [pl.BlockSpec((1, TM, D), lambda b, i: (b, 0, 0)),
            pl.BlockSpec((1, HALO, D), lambda b, i: (b, lat(i) * per - jnp.where(_prev_ok(i, 1), 1, 0), 0)),
            pl.BlockSpec((1, TM, D), lambda b, i: (b, lat(i), 0)),
            pl.BlockSpec((1, HALO, D), lambda b, i: (b, (lat(i) + 1) * per - jnp.where(_next_ok(i, 1, NT), 0, 1), 0))]


def _inproj_call(xs, a_in, s_in, cos_t, sin_t, w, consts):
    split = isinstance(xs, tuple)
    x_args = (xs[0], xs[1], xs[1], xs[1]) if split else (xs, xs, xs)
    bsz = x_args[0].shape[0]
    row = lambda wd: pl.BlockSpec((1, TM, wd), lambda b, i: (b, i, 0))
    tab = pl.BlockSpec((TM, 128), lambda b, i: (i, 0))
    out_w = [(512, BF16), (512, BF16), (512, BF16), (HW, BF16), (HW, BF16), (2 * HW, BF16), (2 * HW, F32),
             (HW, BF16), (HW, BF16), (HW, BF16), (HW, BF16), (128, F32), (HW, BF16)]
    return pl.pallas_call(
        functools.partial(_inproj_kernel, split=split),
        grid=(bsz, NT),
        in_specs=(_split_x_specs() if split else _halo_specs(D, 1, NT)) + [_mod_spec(1), _mod_spec(1), tab, tab]
        + [_const_spec(x.shape) for x in w] + [_const_spec(x.shape) for x in consts],
        out_specs=[row(wd) for wd, _ in out_w],
        out_shape=[jax.ShapeDtypeStruct((bsz, T, wd), dt) for wd, dt in out_w],
        scratch_shapes=[pltpu.VMEM((TM + 2 * HALO, D), BF16), pltpu.VMEM((TM + 2 * HALO, 3 * HW), F32)],
        compiler_params=_cparams(("parallel", "arbitrary")),
    )(*x_args, a_in, s_in, cos_t, sin_t, *w, *consts)


def _attn_kernel(*refs, n_qin):
    q_refs = refs[:n_qin]
    k_ref, v_ref, lam_ref, g_ref, o_ref = refs[n_qin:]
    k, v = k_ref[0], v_ref[0]
    lane = lax.broadcasted_iota(jnp.int32, (TQ, 128), 1)
    zero = jnp.zeros((TQ, 128), BF16)
    per = ATT_QB // TQ
    ix = range(n_qin * per)
    q = [q_refs[j // per][0, (j % per) * TQ:(j % per + 1) * TQ, :] for j in ix]
    qs = [jnp.concatenate([jnp.where(lane < HD, q[j], zero), jnp.where(lane >= HD, q[j], zero)], axis=0) for j in ix]
    st = [_dot_nt(k, qs[j]) for j in ix]
    m = [jnp.max(st[j], axis=0, keepdims=True) for j in ix]
    p = [jnp.exp2(st[j] - m[j]) for j in ix]
    l = [jnp.sum(p[j], axis=0, keepdims=True) for j in ix]
    ot = [_dot_tn(v, p[j].astype(BF16)) * (1.0 / l[j]) for j in ix]
    for j in ix:
        d = ot[j][:, :TQ] - lam_ref[0:1, :] * ot[j][:, TQ:]
        ms = jnp.mean(d * d, axis=0, keepdims=True)
        dn = d * lax.rsqrt(ms + EPS)
        o_ref[0, j * TQ:(j + 1) * TQ, :] = (dn.T * g_ref[...]).astype(BF16)


def _attn_call(q, k, v, lam_arr, g_arr, q_row0, n_rows, tk):
    bsz = q.shape[0]
    n_qin = min(ATT_QIN, n_rows // ATT_QB)
    tq = n_qin * ATT_QB
    assert q_row0 % ATT_QB == 0 and n_rows % tq == 0
    qoff = q_row0 // ATT_QB

    def q_spec(j):
        return pl.BlockSpec((1, ATT_QB, 128), lambda b, h, i: (b, qoff + i * n_qin + j, h))

    return pl.pallas_call(
        functools.partial(_attn_kernel, n_qin=n_qin),
        grid=(bsz, NH, n_rows // tq),
        in_specs=[q_spec(j) for j in range(n_qin)]
        + [pl.BlockSpec((1, tk, 128), lambda b, h, i: (b, 0, h)),
           pl.BlockSpec((1, tk, 128), lambda b, h, i: (b, 0, h)),
           _const_spec((8, 128)), _const_spec((1, 128))],
        out_specs=pl.BlockSpec((1, tq, 128), lambda b, h, i: (b, i, h)),
        out_shape=jax.ShapeDtypeStruct((bsz, n_rows, 512), BF16),
        compiler_params=_cparams(("parallel", "parallel", "arbitrary")),
    )(*([q] * n_qin), k, v, lam_arr, g_arr)


def _bwd_tile(i):
    return jnp.where(i == 0, 0, NT - i)


def _shift_rows(x, n, reverse):
    if n == 0:
        return x
    rows = x.shape[0]
    return pltpu.roll(x, (rows - n) if reverse else n, 0)


def _hg_kernel(qf_ref, vf_ref, kf_ref, lf_ref, qb_ref, vb_ref, kb_ref, lb_ref,
               of_ref, ob_ref, sf_scr, sb_scr):
    @pl.when(pl.program_id(1) == 0)
    def _():
        sf_scr[...] = jnp.zeros_like(sf_scr)
        sb_scr[...] = jnp.zeros_like(sb_scr)

    q = (qf_ref[0].astype(F32), qb_ref[0].astype(F32))
    k = (kf_ref[0].astype(F32), kb_ref[0].astype(F32))
    v = (vf_ref[0].astype(F32), vb_ref[0].astype(F32))
    logf = (lf_ref[0], lb_ref[0])
    o_refs = (of_ref, ob_ref)
    dirs = (0, 1)

    r = lax.broadcasted_iota(jnp.int32, (TM, TM), 0)
    c = lax.broadcasted_iota(jnp.int32, (TM, TM), 1)
    same = (r // BLK) == (c // BLK)
    tri = [jnp.where(jnp.logical_and(same, (c >= r) if d else (c <= r)), 1.0, 0.0).astype(BF16) for d in dirs]
    blk = jnp.where(same, 1.0, 0.0).astype(BF16)
    bl = [_dot_01(tri[d], logf[d]) for d in dirs]
    tot = [_dot_01(blk, logf[d]) for d in dirs]
    qdb = [(q[d] * jnp.exp(bl[d])).astype(BF16) for d in dirs]
    kd = [(k[d] * jnp.exp(tot[d] - bl[d])).astype(BF16) for d in dirs]
    e_blk = [jnp.exp(tot[d]) for d in dirs]
    f = [jnp.exp(logf[d]) for d in dirs]

    pos = lax.broadcasted_iota(jnp.int32, (TM, HW), 0) % BLK
    ones_bd = _ones_bd()
    e = [None, None]
    o_band = [jnp.zeros((TM, HW), F32), jnp.zeros((TM, HW), F32)]
    for n in range(BLK):
        for d in dirs:
            if n == 1:
                e[d] = f[d]
            elif n > 1:
                e[d] = e[d] * _shift_rows(f[d], n - 1, d)
            pn = q[d] * _shift_rows(k[d], n, d)
            if n > 0:
                pn = pn * e[d]
            ok = (pos + n < BLK) if d else (pos >= n)
            pn = jnp.where(ok, pn, 0.0).astype(BF16)
            o_band[d] = o_band[d] + _dot(pn, ones_bd) * _shift_rows(v[d], n, d)

    bd = _bd_mask()
    vb = [v[d].astype(BF16) for d in dirs]
    nb = TM // BLK
    s = [sf_scr[...], sb_scr[...]]
    for j in range(nb):
        for d in dirs:
            ib = nb - 1 - j if d else j
            rows = slice(ib * BLK, (ib + 1) * BLK)
            o_refs[d][0, rows, :] = o_band[d][rows] + _dot_nt(qdb[d][rows], s[d].astype(BF16))
            u = _dot_tn(vb[d][rows], kd[d][rows])
            s[d] = s[d] * e_blk[d][ib * BLK:ib * BLK + 1, :] + jnp.where(bd, u, 0.0)
    sf_scr[...] = s[0]
    sb_scr[...] = s[1]


def _hg_call(hq, hv, hk, hlf):
    bsz = hq.shape[0]
    f = lambda b, i: (b, i, 0)
    g0 = lambda b, i: (b, _bwd_tile(i), 0)
    g1 = lambda b, i: (b, _bwd_tile(i), 1)
    blk = lambda m: pl.BlockSpec((1, TM, HW), m)
    return pl.pallas_call(
        _hg_kernel,
        grid=(bsz, NT),
        in_specs=[blk(f), blk(f), blk(f), blk(f), blk(g0), blk(g0), blk(g1), blk(g1)],
        out_specs=[blk(f), blk(g0)],
        out_shape=[jax.ShapeDtypeStruct((bsz, T, HW), F32)] * 2,
        scratch_shapes=[pltpu.VMEM((HW, HW), F32)] * 2,
        compiler_params=_cparams(("parallel", "arbitrary")),
    )(hq, hv, hk, hlf, hq, hv, hk, hlf)


def _bd4(x, bd):
    return jnp.where(bd, jnp.concatenate([x.astype(BF16)] * NH, axis=0), jnp.zeros((), BF16))


def _split3(x):
    h = x.astype(BF16)
    r1 = x - h.astype(F32)
    m = r1.astype(BF16)
    l = (r1 - m.astype(F32)).astype(BF16)
    return jnp.concatenate([h, m, l], axis=0)


def _dot_01(sel01, x):
    return _dot(jnp.concatenate([sel01] * 3, axis=1), _split3(x))


class _GdMasks:
    def __init__(self, reverse):
        r = lax.broadcasted_iota(jnp.int32, (CH, CH), 0)
        c = lax.broadcasted_iota(jnp.int32, (CH, CH), 1)
        self.tri = jnp.where((c >= r) if reverse else (c <= r), 1.0, 0.0).astype(BF16)
        t_i = lax.broadcasted_iota(jnp.int32, (CH, HW), 0)
        s_i = lax.broadcasted_iota(jnp.int32, (CH, HW), 1) % HD
        self.later = (t_i < s_i) if reverse else (t_i > s_i)
        self.valid = (s_i >= t_i) if reverse else (s_i <= t_i)
        self.strict = (s_i > t_i) if reverse else (s_i < t_i)
        self.eye = jnp.where(t_i == s_i, 1.0, 0.0)
        self.last = 0 if reverse else CH - 1
        self.off = {}
        m = 1
        while m < CH:
            t_blk, s_blk = t_i // m, s_i // m
            pair = (t_blk // 2) == (s_blk // 2)
            lo_hi = (t_blk % 2 == 0, s_blk % 2 == 1) if reverse else (t_blk % 2 == 1, s_blk % 2 == 0)
            self.off[m] = jnp.logical_and(pair, jnp.logical_and(*lo_hi))
            m *= 2


def _gd_prep(q, k, v, ab, dirs, masks, bd):
    grp = lax.broadcasted_iota(jnp.int32, (CH, HW), 1) // HD

    def widen(a, col0):
        out = jnp.zeros((CH, HW), F32)
        for h in range(NH):
            out = jnp.where(grp == h, jnp.broadcast_to(a[:, col0 + h:col0 + h + 1], (CH, HW)), out)
        return out

    ix = range(len(q))
    mk = [masks[d] for d in dirs]
    la = [widen(ab[j], dirs[j] * NH) for j in ix]
    beta = [widen(ab[j], 2 * NH + dirs[j] * NH) for j in ix]
    gc = [_dot_01(mk[j].tri, la[j]) for j in ix]
    diff = [_dot_01(mk[j].tri, jnp.where(mk[j].later, la[j], 0.0)) for j in ix]
    kb = [_bd4(k[j], bd) for j in ix]
    kk = [_dot_nt(k[j].astype(BF16), kb[j]) for j in ix]
    qk = [_dot_nt(q[j].astype(BF16), kb[j]) for j in ix]
    dm = [jnp.where(mk[j].valid, jnp.exp(jnp.minimum(diff[j], 0.0)), 0.0) for j in ix]
    n = [jnp.where(mk[j].strict, beta[j] * kk[j] * dm[j], 0.0) for j in ix]

    tinv = [mk[j].eye - jnp.where(mk[j].off[1], n[j], 0.0) for j in ix]
    m = 2
    while m < CH:
        y = [_dot(tinv[j].astype(BF16), _bd4(jnp.where(mk[j].off[m], n[j], 0.0), bd)) for j in ix]
        tinv = [tinv[j] - _dot(y[j].astype(BF16), _bd4(tinv[j], bd)) for j in ix]
        m *= 2
    tb = [t.astype(BF16) for t in tinv]
    eg = [jnp.exp(g) for g in gc]
    u = [_dot(tb[j], _bd4(v[j] * beta[j], bd)) for j in ix]
    w = [_dot(tb[j], _bd4(k[j] * beta[j] * eg[j], bd)) for j in ix]
    sc = [(qk[j] * dm[j]).astype(BF16) for j in ix]
    qeff = [(q[j] * eg[j] - _dot(sc[j], _bd4(w[j], bd))).astype(BF16) for j in ix]
    oc = [_dot(sc[j], _bd4(u[j], bd)) for j in ix]
    gl = [gc[j][mk[j].last:mk[j].last + 1, :] for j in ix]
    kd = [(k[j] * jnp.exp(gl[j] - gc[j])).astype(BF16) for j in ix]
    kw = [jnp.where(bd, _dot_tn(kd[j], w[j].astype(BF16)), 0.0).astype(BF16) for j in ix]
    ku = [jnp.where(bd, _dot_tn(kd[j], u[j].astype(BF16)), 0.0) for j in ix]
    a = [jnp.exp(g) for g in gl]
    return qeff, oc, kw, ku, a


def _gd_kernel(qf_ref, kf_ref, vf_ref, abf_ref, qb_ref, kb_ref, vb_ref, abb_ref,
               of_ref, ob_ref, sf_scr, sb_scr):
    @pl.when(pl.program_id(1) == 0)
    def _():
        sf_scr[...] = jnp.zeros_like(sf_scr)
        sb_scr[...] = jnp.zeros_like(sb_scr)

    bd = _bd_mask()
    n_ch = TM // CH
    refs = ((qf_ref, kf_ref, vf_ref, abf_ref), (qb_ref, kb_ref, vb_ref, abb_ref))
    o_refs = (of_ref, ob_ref)
    items = [(d, (n_ch - 1 - j) if d else j) for j in range(n_ch) for d in range(2)]
    rows = [slice(ic * CH, (ic + 1) * CH) for _, ic in items]
    dirs = [d for d, _ in items]
    load = lambda which: [refs[d][which][0, r, :] for d, r in zip(dirs, rows)]
    qeff, oc, kw, ku, a = _gd_prep(load(0), load(1), load(2), load(3), dirs, (_GdMasks(False), _GdMasks(True)), bd)

    s = [sf_scr[...], sb_scr[...]]
    for j, (d, _) in enumerate(items):
        sb = s[d].astype(BF16)
        o_refs[d][0, rows[j], :] = _dot(qeff[j], sb) + oc[j]
        s[d] = s[d] * a[j] - _dot(kw[j], sb) + ku[j]
    sf_scr[...] = s[0]
    sb_scr[...] = s[1]


def _gd_call(gq, gk, gv, gab):
    bsz = gq.shape[0]
    f = lambda b, i: (b, i, 0)
    g0 = lambda b, i: (b, _bwd_tile(i), 0)
    blk = lambda m: pl.BlockSpec((1, TM, HW), m)
    abs_ = lambda m: pl.BlockSpec((1, TM, 128), m)
    return pl.pallas_call(
        _gd_kernel,
        grid=(bsz, NT),
        in_specs=[blk(f), blk(f), blk(f), abs_(f), blk(g0), blk(g0), blk(g0), abs_(g0)],
        out_specs=[blk(f), blk(g0)],
        out_shape=[jax.ShapeDtypeStruct((bsz, T, HW), F32)] * 2,
        scratch_shapes=[pltpu.VMEM((HW, HW), F32)] * 2,
        compiler_params=_cparams(("parallel", "arbitrary")),
    )(gq, gk, gv, gab, gq, gk, gv, gab)


def _outproj_kernel(*refs, with_ctx, split):
    if split:
        ctx_ref, refs = refs[0], refs[1:]
    (x_ref, oal_ref, oac_ref, hof_ref, hob_ref, hsg_ref, gof_ref, gob_ref, gsg_ref,
     wo_ref, hng_ref, gng_ref, g1_ref, gate_ref, o_ref) = refs
    ones_bd = _ones_bd()
    oa = oal_ref[0]
    x = x_ref[0]
    if with_ctx:
        oa = jnp.where(pl.program_id(1) == 0, oac_ref[0], oa)
    if split:
        x = jnp.where(pl.program_id(1) == 0, ctx_ref[0], x)

    def finish(of_ref, ob_ref, sg_ref, ng_ref):
        o = of_ref[0] + ob_ref[0]
        ms = _head_sum(o * o, ones_bd) * (1.0 / HD)
        return (o * lax.rsqrt(ms + EPS) * ng_ref[...] * sg_ref[0]).astype(BF16)

    ob = finish(hof_ref, hob_ref, hsg_ref, hng_ref)
    oc = finish(gof_ref, gob_ref, gsg_ref, gng_ref)
    mix = (_dot(oa, wo_ref[0:512, :]) + _dot(ob, wo_ref[512:768, :]) + _dot(oc, wo_ref[768:1024, :]))
    ms = jnp.mean(mix * mix, axis=-1, keepdims=True)
    o_ref[0] = x + gate_ref[0, 0] * (mix * lax.rsqrt(ms + EPS) * g1_ref[...])


def _outproj_call(xs, oa_lat, oa_ctx, hof, hob, hsg, gof, gob, gsg, wo, hng, gng, g1, gate, with_ctx):
    split = isinstance(xs, tuple)
    assert with_ctx or not split
    bsz = oa_lat.shape[0]
    t0 = 0 if with_ctx else 1
    row = lambda wd: pl.BlockSpec((1, TM, wd), lambda b, i: (b, i + t0, 0))
    if split:
        x_specs = [pl.BlockSpec((1, TM, D), lambda b, i: (b, 0, 0)),
                   pl.BlockSpec((1, TM, D), lambda b, i: (b, jnp.maximum(i - 1, 0), 0))]
        x_args = list(xs)
    else:
        x_specs, x_args = [row(D)], [xs]
    oal = pl.BlockSpec((1, TM, 512), lambda b, i: (b, jnp.maximum(i + t0 - 1, 0), 0))
    oac = pl.BlockSpec((1, TM, 512), lambda b, i: (b, 0, 0))
    mod = pl.BlockSpec((1, 1, 1, D), lambda b, i: (b, _sel(i + t0, 1), 0, 0))
    return pl.pallas_call(
        functools.partial(_outproj_kernel, with_ctx=with_ctx, split=split),
        grid=(bsz, NT - t0),
        in_specs=x_specs + [oal, oac, row(HW), row(HW), row(HW), row(HW), row(HW), row(HW),
                            _const_spec((D, D)), _const_spec((1, HW)), _const_spec((1, HW)), _const_spec((1, D)),
                            mod],
        out_specs=pl.BlockSpec((1, TM, D), lambda b, i: (b, i, 0)),
        out_shape=jax.ShapeDtypeStruct((bsz, (NT - t0) * TM, D), F32),
        compiler_params=_cparams(("parallel", "arbitrary")),
    )(*x_args, oa_lat, oa_ctx, hof, hob, hsg, gof, gob, gsg, wo, hng, gng, g1, gate)


def _ffn_kernel(xp_ref, xm_ref, xn_ref, a_ref, s_ref, wup_ref, cw_ref, cb_ref, wdn_ref, g3_ref, gate_ref,
                o_ref, h_scr, u_scr, act_scr, *, nct, nt):
    _fill_h(h_scr, xp_ref[0], xm_ref[0], xn_ref[0], a_ref[0, 0], s_ref[0, 0], pl.program_id(1), nct, nt)
    h = h_scr[...]
    for cidx in range(D_FF // FF_CW):
        lo = cidx * FF_CW
        u_scr[:, 0:FF_CW] = _dot(h, wup_ref[:, lo:lo + FF_CW])
        u_scr[:, FF_CW:2 * FF_CW] = _dot(h, wup_ref[:, D_FF + lo:D_FF + lo + FF_CW])
        wg = cw_ref[:, lo:lo + FF_CW]
        wv = cw_ref[:, D_FF + lo:D_FF + lo + FF_CW]
        g = (u_scr[HALO - 1:HALO - 1 + TM, 0:FF_CW] * wg[0:1] + u_scr[HALO:HALO + TM, 0:FF_CW] * wg[1:2]
             + u_scr[HALO + 1:HALO + 1 + TM, 0:FF_CW] * wg[2:3] + cb_ref[:, lo:lo + FF_CW])
        vv = (u_scr[HALO - 1:HALO - 1 + TM, FF_CW:] * wv[0:1] + u_scr[HALO:HALO + TM, FF_CW:] * wv[1:2]
              + u_scr[HALO + 1:HALO + 1 + TM, FF_CW:] * wv[2:3] + cb_ref[:, D_FF + lo:D_FF + lo + FF_CW])
        act_scr[:, lo:lo + FF_CW] = (_silu(g) * vv).astype(BF16)
    ff = _dot(act_scr[...], wdn_ref[...])
    ms = jnp.mean(ff * ff, axis=-1, keepdims=True)
    o_ref[0] = xm_ref[0] + gate_ref[0, 0] * (ff * lax.rsqrt(ms + EPS) * g3_ref[...])


def _ffn_call(xs, a_ff, s_ff, wup, cw, cb, wdn, g3, gate, nct):
    bsz, rows, _ = xs.shape
    nt = rows // TM
    mod = _mod_spec(nct)
    return pl.pallas_call(
        functools.partial(_ffn_kernel, nct=nct, nt=nt),
        grid=(bsz, nt),
        in_specs=_halo_specs(D, nct, nt) + [
            mod, mod, _const_spec((D, 2 * D_FF)), _const_spec((3, 2 * D_FF)), _const_spec((1, 2 * D_FF)),
            _const_spec((D_FF, D)), _const_spec((1, D)), mod],
        out_specs=pl.BlockSpec((1, TM, D), lambda b, i: (b, i, 0)),
        out_shape=jax.ShapeDtypeStruct((bsz, rows, D), F32),
        scratch_shapes=[pltpu.VMEM((TM + 2 * HALO, D), BF16), pltpu.VMEM((TM + 2 * HALO, 2 * FF_CW), F32),
                        pltpu.VMEM((TM, D_FF), BF16)],
        compiler_params=_cparams(("parallel", "arbitrary")),
    )(xs, xs, xs, a_ff, s_ff, wup, cw, cb, wdn, g3, gate)


def _rope_tables():
    n_freq = HD // 4
    inv = ROPE_THETA ** (-jnp.arange(n_freq, dtype=F32) / n_freq)
    rows = jnp.repeat(jnp.arange(SEQ // GRID_W, dtype=F32), GRID_W)
    cols = jnp.tile(jnp.arange(GRID_W, dtype=F32), SEQ // GRID_W)
    ang = jnp.concatenate([rows[:, None] * inv, cols[:, None] * inv], axis=-1)
    cos, sin = jnp.cos(ang), jnp.sin(ang)
    cos_l = jnp.tile(jnp.concatenate([cos, cos], axis=-1), (1, 2))
    sin_l = jnp.tile(jnp.concatenate([-sin, sin], axis=-1), (1, 2))
    cos_t = jnp.concatenate([jnp.ones((CTX, 128), F32), cos_l], axis=0)
    sin_t = jnp.concatenate([jnp.zeros((CTX, 128), F32), sin_l], axis=0)
    return cos_t, sin_t


def kernel(x, c, ctx, c_ctx, ada_w, ada_b, norm_g, w_in, w_out, da_lambda, da_subln_g, hg_lb_logits, hg_norm_g,
           gd_conv_w, gd_a_log, gd_dt_bias, gd_norm_g, ffn_w_up, ffn_conv_w, ffn_conv_b, ffn_w_down):
    bsz = x.shape[0]
    depth = ada_w.shape[0]
    assert x.shape == (bsz, SEQ, D) and ctx.shape == (bsz, CTX, D) and bsz <= 8
    cos_t, sin_t = _rope_tables()
    lb_w = jax.nn.softmax(hg_lb_logits.astype(F32), axis=0)
    lb_all = jnp.cumsum(lb_w, axis=0) - lb_w[0]

    cond = jnp.zeros((16, D), F32).at[:bsz].set(jax.nn.silu(c)).at[8].set(jax.nn.silu(c_ctx))
    mods = _ada_call(cond, ada_w, ada_b)

    xs = (ctx.astype(F32), x.astype(F32))
    for layer in range(depth):
        need_ctx = layer < depth - 1
        lam_init = 0.8 - 0.6 * math.exp(-0.3 * layer)
        m = mods[layer].reshape(16, 6, D)
        mod = jnp.stack([jnp.broadcast_to(m[8], (bsz, 6, D)), m[:bsz]], axis=1)
        g = norm_g[layer].astype(F32)

        def vec(v):
            return v.reshape(bsz, 2, 1, D)

        a_in, s_in = vec(g[0] * (1.0 + mod[:, :, 1])), vec(mod[:, :, 0])
        a_ff, s_ff = vec(g[2] * (1.0 + mod[:, :, 4])), vec(mod[:, :, 3])
        gate1, gate2 = vec(mod[:, :, 2]), vec(mod[:, :, 5])

        wl = w_in[layer]
        wab = jnp.zeros((D, 128), F32).at[:, :4 * NH].set(wl[:, C_GDA:C_GDG])
        w = [wl[:, C_DAQ:C_DAV], wl[:, C_DAV:C_HGQ], wl[:, C_HGQ:C_GDQKV], wl[:, C_GDQKV:C_GDA], wab,
             wl[:, C_GDG:IN_COLS]]
        w = [t.astype(BF16) for t in w]
        lb = lb_all[layer].reshape(1, 2 * HW)
        nega = jnp.zeros((1, 128), F32).at[0, :2 * NH].set(-jnp.exp(gd_a_log[layer].astype(F32)).reshape(-1))
        dtb = jnp.zeros((1, 128), F32).at[0, :2 * NH].set(gd_dt_bias[layer].astype(F32).reshape(-1))
        consts = [1.0 - lb, jnp.log(jnp.maximum(lb, LB_FLOOR)), jnp.log1p(-lb),
                  gd_conv_w[layer].astype(F32), nega, dtb]
        (q, k, v, hq, hv, hk, hlf, hsg, gq, gk, gv, gab, gsg) = _inproj_call(
            xs, a_in, s_in, cos_t, sin_t, w, consts)

        lp = da_lambda[layer].astype(F32)
        lam = jnp.exp(jnp.sum(lp[0] * lp[1])) - jnp.exp(jnp.sum(lp[2] * lp[3])) + lam_init
        lam_arr = jnp.full((8, 128), lam, F32)
        g_arr = (da_subln_g[layer].astype(F32) * (1.0 - lam_init)).reshape(1, DV)
        oa_lat = _attn_call(q, k, v, lam_arr, g_arr, CTX, SEQ, T)
        oa_ctx = _attn_call(q, k, v, lam_arr, g_arr, 0, CTX, CTX) if need_ctx else oa_lat

        hof, hob = _hg_call(hq, hv, hk, hlf)
        gof, gob = _gd_call(gq, gk, gv, gab)

        hng = jnp.tile(hg_norm_g[layer].astype(F32), NH).reshape(1, HW)
        gng = jnp.tile(gd_norm_g[layer].astype(F32), NH).reshape(1, HW)
        xs = _outproj_call(xs, oa_lat, oa_ctx, hof, hob, hsg, gof, gob, gsg, w_out[layer].astype(BF16), hng, gng,
                           g[1].reshape(1, D), gate1, need_ctx)
        xs = _ffn_call(xs, a_ff, s_ff, ffn_w_up[layer].astype(BF16), ffn_conv_w[layer].astype(F32),
                       ffn_conv_b[layer].astype(F32).reshape(1, 2 * D_FF), ffn_w_down[layer].astype(BF16),
                       g[3].reshape(1, D), gate2, 1 if need_ctx else 0)
    return xs
```

```python
import functools
import math

import jax
import jax.numpy as jnp
import numpy as np
from jax import lax
from jax.experimental import pallas as pl
from jax.experimental.pallas import tpu as pltpu

F32 = jnp.float32
BF16 = jnp.bfloat16

D = 1024
CTX = 256
SEQ = 2048
T = CTX + SEQ
GRID_W = 64
ROPE_THETA = 10000.0
EPS = 1e-6
LB_FLOOR = 1e-30
NH = 4
HD = 64
DV = 128
HW = NH * HD
D_FF = 2816
TM = 256
NT = T // TM
HALO = 16
TQ = 128
ATT_QB = 256
ATT_QIN = 4
Q_SCALE = HD ** -0.5 * math.log2(math.e)
BLK = 16
HG_NV = TM // 8
assert HG_NV == 2 * BLK
CH = 64
FF_CW = 256
VMEM_LIMIT = 56 * 1024 * 1024

C_DAQ, C_DAK, C_DAV = 0, 512, 1024
C_HGQ, C_HGI, C_HGF, C_HGG = 1536, 1792, 2048, 2560
C_GDQKV, C_GDA, C_GDB, C_GDG = 2816, 3584, 3592, 3600
IN_COLS = 3856


def _cparams(sem):
    return pltpu.CompilerParams(dimension_semantics=sem, vmem_limit_bytes=VMEM_LIMIT)


def _const_spec(shape):
    n = len(shape)
    return pl.BlockSpec(shape, lambda *_: (0,) * n)


def _sigmoid(x):
    return 1.0 / (1.0 + jnp.exp(-x))


def _silu(x):
    return x * _sigmoid(x)


def _softplus(x):
    return jnp.maximum(x, 0.0) + jnp.log(1.0 + jnp.exp(-jnp.abs(x)))


def _dot(a, b):
    return jnp.dot(a, b, preferred_element_type=F32)


def _dot_nt(a, b):
    return lax.dot_general(a, b, (((1,), (1,)), ((), ())), preferred_element_type=F32)


def _dot_tn(a, b):
    return lax.dot_general(a, b, (((0,), (0,)), ((), ())), preferred_element_type=F32)


def _dot_hi(a, b):
    return jnp.dot(a, b, preferred_element_type=F32, precision=lax.Precision.HIGHEST)


def _head_sum(x, ones_bd):
    hi = x.astype(BF16)
    lo = (x - hi.astype(F32)).astype(BF16)
    return _dot(hi, ones_bd) + _dot(lo, ones_bd)


def _ones_bd():
    r = lax.broadcasted_iota(jnp.int32, (HW, HW), 0) // HD
    c = lax.broadcasted_iota(jnp.int32, (HW, HW), 1) // HD
    return jnp.where(r == c, 1.0, 0.0).astype(BF16)


def _bd_mask():
    r = lax.broadcasted_iota(jnp.int32, (HW, HW), 0) // HD
    c = lax.broadcasted_iota(jnp.int32, (HW, HW), 1) // HD
    return r == c


def _normed(x, a, s):
    ms = jnp.mean(x * x, axis=-1, keepdims=True)
    return (x * lax.rsqrt(ms + EPS)) * a + s


def _ada_kernel(c_ref, w_ref, b_ref, o_ref):
    o_ref[0] = _dot(c_ref[...].astype(BF16), w_ref[0].astype(BF16)) + b_ref[0]


def _ada_call(cond, ada_w, ada_b):
    depth = ada_w.shape[0]
    nc = 6 * D
    cw = 1536
    return pl.pallas_call(
        _ada_kernel,
        grid=(depth, nc // cw),
        in_specs=[pl.BlockSpec((16, D), lambda l, j: (0, 0)),
                  pl.BlockSpec((1, D, cw), lambda l, j: (l, 0, j)),
                  pl.BlockSpec((1, 1, cw), lambda l, j: (l, 0, j))],
        out_specs=pl.BlockSpec((1, 16, cw), lambda l, j: (l, 0, j)),
        out_shape=jax.ShapeDtypeStruct((depth, 16, nc), F32),
        compiler_params=_cparams(("arbitrary", "arbitrary")),
    )(cond, ada_w, ada_b.reshape(depth, 1, nc))


def _sel(i, nct):
    return jnp.where(i >= nct, 1, 0)


def _prev_ok(i, nct):
    return i > nct


def _next_ok(i, nct, nt):
    return jnp.logical_and(i >= nct, i <= nt - 2)


def _halo_specs(width, nct, nt):
    per = TM // HALO
    return [pl.BlockSpec((1, HALO, width), lambda b, i: (b, i * per - jnp.where(_prev_ok(i, nct), 1, 0), 0)),
            pl.BlockSpec((1, TM, width), lambda b, i: (b, i, 0)),
            pl.BlockSpec((1, HALO, width), lambda b, i: (b, (i + 1) * per - jnp.where(_next_ok(i, nct, nt), 0, 1), 0))]


def _mod_spec(nct):
    return pl.BlockSpec((1, 1, 1, D), lambda b, i: (b, _sel(i, nct), 0, 0))


def _fill_h(h_scr, xp, xm, xn, a, s, i, nct, nt):
    hp = _normed(xp, a, s)
    hn = _normed(xn, a, s)
    h_scr[0:HALO, :] = jnp.where(_prev_ok(i, nct), hp, 0.0).astype(BF16)
    h_scr[HALO:HALO + TM, :] = _normed(xm, a, s).astype(BF16)
    h_scr[HALO + TM:, :] = jnp.where(_next_ok(i, nct, nt), hn, 0.0).astype(BF16)


def _conv3(u_scr, w_ref):
    return (u_scr[HALO - 1:HALO - 1 + TM, :] * w_ref[0:1, :]
            + u_scr[HALO:HALO + TM, :] * w_ref[1:2, :]
            + u_scr[HALO + 1:HALO + 1 + TM, :] * w_ref[2:3, :])


def _inproj_kernel(*refs, split):
    if split:
        ctx_ref, refs = refs[0], refs[1:]
    (xp_ref, xm_ref, xn_ref, a_ref, s_ref, cos_ref, sin_ref,
     wqk_ref, wv_ref, whg_ref, wgq_ref, wab_ref, wgg_ref,
     lb1m_ref, loglb_ref, log1mlb_ref, gconv_ref, nega_ref, dtb_ref,
     q_ref, k_ref, v_ref, hq_ref, hv_ref, hk_ref, hlf_ref, hsg_ref,
     gq_ref, gk_ref, gv_ref, gab_ref, gsg_ref,
     h_scr, u_scr) = refs
    i = pl.program_id(1)
    xm = jnp.where(i == 0, ctx_ref[0], xm_ref[0]) if split else xm_ref[0]
    _fill_h(h_scr, xp_ref[0], xm, xn_ref[0], a_ref[0, 0], s_ref[0, 0], i, 1, NT)
    h = h_scr[HALO:HALO + TM, :]

    z = _dot(h, wqk_ref[...])
    z_v = _dot(h, wv_ref[...])
    z_hg = _dot(h, whg_ref[...])
    u_scr[...] = _dot(h_scr[...], wgq_ref[...])
    z_ab = _dot(h, wab_ref[...])
    z_gg = _dot(h, wgg_ref[...])

    lane = lax.broadcasted_iota(jnp.int32, (TM, 128), 1)
    first_half = (lane % HD) < (HD // 2)
    cs, sn = cos_ref[...], sin_ref[...]
    for j in range(8):
        xj = z[:, j * 128:(j + 1) * 128]
        sw = jnp.where(first_half, pltpu.roll(xj, 128 - HD // 2, 1), pltpu.roll(xj, HD // 2, 1))
        r = xj * cs + sw * sn
        if j < 4:
            q_ref[0, :, j * 128:(j + 1) * 128] = (r * Q_SCALE).astype(BF16)
        else:
            k_ref[0, :, (j - 4) * 128:(j - 3) * 128] = r.astype(BF16)
    v_ref[0] = z_v.astype(BF16)

    z = z_hg
    hq_ref[0] = _silu(z[:, 0:HW]).astype(BF16)
    hv_ref[0] = z[:, HW:2 * HW].astype(BF16)
    f = z[:, 2 * HW:4 * HW]
    hk_ref[0] = (lb1m_ref[...] * _sigmoid(-f)).astype(BF16)
    a = loglb_ref[...]
    b = log1mlb_ref[...] - _softplus(-f)
    hlf_ref[0] = jnp.maximum(a, b) + jnp.log(1.0 + jnp.exp(-jnp.abs(a - b)))
    hsg_ref[0] = _silu(z[:, 4 * HW:5 * HW]).astype(BF16)

    y = _silu(_conv3(u_scr, gconv_ref))
    ones_bd = _ones_bd()
    qg, kg = y[:, 0:HW], y[:, HW:2 * HW]
    gq_ref[0] = (qg * lax.rsqrt(_head_sum(qg * qg, ones_bd) + EPS) * (HD ** -0.5)).astype(BF16)
    gk_ref[0] = (kg * lax.rsqrt(_head_sum(kg * kg, ones_bd) + EPS)).astype(BF16)
    gv_ref[0] = y[:, 2 * HW:3 * HW].astype(BF16)
    gab_ref[0] = jnp.where(lane < 2 * NH, nega_ref[...] * _softplus(z_ab + dtb_ref[...]), _sigmoid(z_ab))
    gsg_ref[0] = _silu(z_gg).astype(BF16)


def _split_x_specs():
    per = TM // HALO
    lat = lambda i: jnp.maximum(i - 1, 0)
    return [pl.BlockSpec((1, TM, D), lambda b, i: (b, 0, 0)),
            pl.BlockSpec((1, HALO, D), lambda b, i: (b, lat(i) * per - jnp.where(_prev_ok(i, 1), 1, 0), 0)),
            pl.BlockSpec((1, TM, D), lambda b, i: (b, lat(i), 0)),
            pl.BlockSpec((1, HALO, D), lambda b, i: (b, (lat(i) + 1) * per - jnp.where(_next_ok(i, 1, NT), 0, 1), 0))]


def _inproj_call(xs, a_in, s_in, cos_t, sin_t, w, consts):
    split = isinstance(xs, tuple)
    x_args = (xs[0], xs[1], xs[1], xs[1]) if split else (xs, xs, xs)
    bsz = x_args[0].shape[0]
    row = lambda wd: pl.BlockSpec((1, TM, wd), lambda b, i: (b, i, 0))
    tab = pl.BlockSpec((TM, 128), lambda b, i: (i, 0))
    out_w = [(512, BF16), (512, BF16), (512, BF16), (HW, BF16), (HW, BF16), (2 * HW, BF16), (2 * HW, F32),
             (HW, BF16), (HW, BF16), (HW, BF16), (HW, BF16), (128, F32), (HW, BF16)]
    return pl.pallas_call(
        functools.partial(_inproj_kernel, split=split),
        grid=(bsz, NT),
        in_specs=(_split_x_specs() if split else _halo_specs(D, 1, NT)) + [_mod_spec(1), _mod_spec(1), tab, tab]
        + [_const_spec(x.shape) for x in w] + [_const_spec(x.shape) for x in consts],
        out_specs=[row(wd) for wd, _ in out_w],
        out_shape=[jax.ShapeDtypeStruct((bsz, T, wd), dt) for wd, dt in out_w],
        scratch_shapes=[pltpu.VMEM((TM + 2 * HALO, D), BF16), pltpu.VMEM((TM + 2 * HALO, 3 * HW), F32)],
        compiler_params=_cparams(("parallel", "arbitrary")),
    )(*x_args, a_in, s_in, cos_t, sin_t, *w, *consts)


def _attn_kernel(*refs, n_qin):
    q_refs = refs[:n_qin]
    k_ref, v_ref, lam_ref, g_ref, o_ref = refs[n_qin:]
    k, v = k_ref[0], v_ref[0]
    lane = lax.broadcasted_iota(jnp.int32, (TQ, 128), 1)
    zero = jnp.zeros((TQ, 128), BF16)
    per = ATT_QB // TQ
    ix = range(n_qin * per)
    q = [q_refs[j // per][0, (j % per) * TQ:(j % per + 1) * TQ, :] for j in ix]
    qs = [jnp.concatenate([jnp.where(lane < HD, q[j], zero), jnp.where(lane >= HD, q[j], zero)], axis=0) for j in ix]
    st = [_dot_nt(k, qs[j]) for j in ix]
    m = [jnp.max(st[j], axis=0, keepdims=True) for j in ix]
    p = [jnp.exp2(st[j] - m[j]) for j in ix]
    l = [jnp.sum(p[j], axis=0, keepdims=True) for j in ix]
    ot = [_dot_tn(v, p[j].astype(BF16)) * (1.0 / l[j]) for j in ix]
    for j in ix:
        d = ot[j][:, :TQ] - lam_ref[0:1, :] * ot[j][:, TQ:]
        ms = jnp.mean(d * d, axis=0, keepdims=True)
        dn = d * lax.rsqrt(ms + EPS)
        o_ref[0, j * TQ:(j + 1) * TQ, :] = (dn.T * g_ref[...]).astype(BF16)


def _attn_call(q, k, v, lam_arr, g_arr, q_row0, n_rows, tk):
    bsz = q.shape[0]
    n_qin = min(ATT_QIN, n_rows // ATT_QB)
    tq = n_qin * ATT_QB
    assert q_row0 % ATT_QB == 0 and n_rows % tq == 0
    qoff = q_row0 // ATT_QB

    def q_spec(j):
        return pl.BlockSpec((1, ATT_QB, 128), lambda b, h, i: (b, qoff + i * n_qin + j, h))

    return pl.pallas_call(
        functools.partial(_attn_kernel, n_qin=n_qin),
        grid=(bsz, NH, n_rows // tq),
        in_specs=[q_spec(j) for j in range(n_qin)]
        + [pl.BlockSpec((1, tk, 128), lambda b, h, i: (b, 0, h)),
           pl.BlockSpec((1, tk, 128), lambda b, h, i: (b, 0, h)),
           _const_spec((8, 128)), _const_spec((1, 128))],
        out_specs=pl.BlockSpec((1, tq, 128), lambda b, h, i: (b, i, h)),
        out_shape=jax.ShapeDtypeStruct((bsz, n_rows, 512), BF16),
        compiler_params=_cparams(("parallel", "parallel", "arbitrary")),
    )(*([q] * n_qin), k, v, lam_arr, g_arr)


def _bwd_tile(i):
    return jnp.where(i == 0, 0, NT - i)


def _shift_rows(x, n, reverse):
    if n == 0:
        return x
    rows = x.shape[0]
    return pltpu.roll(x, (rows - n) if reverse else n, 0)


def _hg_kernel(qf_ref, vf_ref, kf_ref, lf_ref, qb_ref, vb_ref, kb_ref, lb_ref,
               of_ref, ob_ref, sf_scr, sb_scr):
    @pl.when(pl.program_id(1) == 0)
    def _():
        sf_scr[...] = jnp.zeros_like(sf_scr)
        sb_scr[...] = jnp.zeros_like(sb_scr)

    q = (qf_ref[0].astype(F32), qb_ref[0].astype(F32))
    k = (kf_ref[0].astype(F32), kb_ref[0].astype(F32))
    v = (vf_ref[0].astype(F32), vb_ref[0].astype(F32))
    logf = (lf_ref[0], lb_ref[0])
    o_refs = (of_ref, ob_ref)
    dirs = (0, 1)

    r = lax.broadcasted_iota(jnp.int32, (TM, TM), 0)
    c = lax.broadcasted_iota(jnp.int32, (TM, TM), 1)
    same = (r // BLK) == (c // BLK)
    tri = [jnp.where(jnp.logical_and(same, (c >= r) if d else (c <= r)), 1.0, 0.0).astype(BF16) for d in dirs]
    blk = jnp.where(same, 1.0, 0.0).astype(BF16)
    bl = [_dot_01(tri[d], logf[d]) for d in dirs]
    tot = [_dot_01(blk, logf[d]) for d in dirs]
    qdb = [(q[d] * jnp.exp(bl[d])).astype(BF16) for d in dirs]
    kd = [(k[d] * jnp.exp(tot[d] - bl[d])).astype(BF16) for d in dirs]
    e_blk = [jnp.exp(tot[d]) for d in dirs]
    f = [jnp.exp(logf[d]) for d in dirs]

    def regroup(x):
        return jnp.swapaxes(x.reshape(8, HG_NV, HW), 0, 1).reshape(2, BLK, 8, HW)

    q4 = [regroup(q[d]) for d in dirs]
    k4 = [regroup(k[d]) for d in dirs]
    v4 = [regroup(v[d]) for d in dirs]
    f4 = [regroup(f[d]) for d in dirs]
    ones_bd = _ones_bd()
    e = [None, None]
    o4 = [None, None]
    for n in range(BLK):
        ln = BLK - n
        for d in dirs:
            qs, ks = (slice(0, ln), slice(n, BLK)) if d else (slice(n, BLK), slice(0, ln))
            if n == 0:
                pn = q4[d] * k4[d]
            else:
                fs = slice(n - 1, n - 1 + ln) if d else slice(1, 1 + ln)
                e[d] = f4[d][:, fs] if n == 1 else (e[d][:, :ln] if d else e[d][:, 1:]) * f4[d][:, fs]
                pn = q4[d][:, qs] * k4[d][:, ks] * e[d]
            a = _dot(pn.reshape(2 * ln * 8, HW).astype(BF16), ones_bd).reshape(2, ln, 8, HW) * v4[d][:, ks]
            if n == 0:
                o4[d] = a
            else:
                pad = jnp.zeros((2, n, 8, HW), F32)
                o4[d] = o4[d] + jnp.concatenate([a, pad] if d else [pad, a], axis=1)
    o_band = [jnp.swapaxes(o4[d].reshape(HG_NV, 8, HW), 0, 1).reshape(TM, HW) for d in dirs]

    lane = lax.broadcasted_iota(jnp.int32, (HD, 128), 1)
    own = [(lane >= HD) if h % 2 else (lane < HD) for h in range(NH)]
    tile = [slice(128 * (h // 2), 128 * (h // 2) + 128) for h in range(NH)]
    zero_t = jnp.zeros((HD, 128), BF16)
    vb = [v[d].astype(BF16) for d in dirs]
    nb = TM // BLK
    s_scr = (sf_scr, sb_scr)
    s = [[s_scr[d][h] for h in range(NH)] for d in dirs]
    for j in range(nb):
        for d in dirs:
            ib = nb - 1 - j if d else j
            rows = slice(ib * BLK, (ib + 1) * BLK)
            s_bd = jnp.concatenate(
                [jnp.concatenate([s[d][h].astype(BF16) if c == h // 2 else zero_t for c in range(HW // 128)], axis=1)
                 for h in range(NH)], axis=0)
            o_refs[d][0, rows, :] = o_band[d][rows] + _dot_nt(qdb[d][rows], s_bd)
            u = _dot_tn(vb[d][rows], kd[d][rows])
            e_row = e_blk[d][ib * BLK:ib * BLK + 1, :]
            for h in range(NH):
                s[d][h] = s[d][h] * e_row[:, tile[h]] + jnp.where(own[h], u[h * HD:(h + 1) * HD, tile[h]], 0.0)
    for d in dirs:
        for h in range(NH):
            s_scr[d][h] = s[d][h]


def _hg_call(hq, hv, hk, hlf):
    bsz = hq.shape[0]
    f = lambda b, i: (b, i, 0)
    g0 = lambda b, i: (b, _bwd_tile(i), 0)
    g1 = lambda b, i: (b, _bwd_tile(i), 1)
    blk = lambda m: pl.BlockSpec((1, TM, HW), m)
    return pl.pallas_call(
        _hg_kernel,
        grid=(bsz, NT),
        in_specs=[blk(f), blk(f), blk(f), blk(f), blk(g0), blk(g0), blk(g1), blk(g1)],
        out_specs=[blk(f), blk(g0)],
        out_shape=[jax.ShapeDtypeStruct((bsz, T, HW), F32)] * 2,
        scratch_shapes=[pltpu.VMEM((NH, HD, 128), F32)] * 2,
        compiler_params=_cparams(("parallel", "arbitrary")),
    )(hq, hv, hk, hlf, hq, hv, hk, hlf)


def _bd4(x, bd):
    return jnp.where(bd, jnp.concatenate([x.astype(BF16)] * NH, axis=0), jnp.zeros((), BF16))


def _split3(x):
    h = x.astype(BF16)
    r1 = x - h.astype(F32)
    m = r1.astype(BF16)
    l = (r1 - m.astype(F32)).astype(BF16)
    return jnp.concatenate([h, m, l], axis=0)


def _dot_01(sel01, x):
    return _dot(jnp.concatenate([sel01] * 3, axis=1), _split3(x))


class _GdMasks:
    def __init__(self, reverse):
        r = lax.broadcasted_iota(jnp.int32, (CH, CH), 0)
        c = lax.broadcasted_iota(jnp.int32, (CH, CH), 1)
        self.tri = jnp.where((c >= r) if reverse else (c <= r), 1.0, 0.0).astype(BF16)
        t_i = lax.broadcasted_iota(jnp.int32, (CH, HW), 0)
        s_i = lax.broadcasted_iota(jnp.int32, (CH, HW), 1) % HD
        self.later = (t_i < s_i) if reverse else (t_i > s_i)
        self.valid = (s_i >= t_i) if reverse else (s_i <= t_i)
        self.strict = (s_i > t_i) if reverse else (s_i < t_i)
        self.eye = jnp.where(t_i == s_i, 1.0, 0.0)
        self.last = 0 if reverse else CH - 1
        self.off = {}
        m = 1
        while m < CH:
            t_blk, s_blk = t_i // m, s_i // m
            pair = (t_blk // 2) == (s_blk // 2)
            lo_hi = (t_blk % 2 == 0, s_blk % 2 == 1) if reverse else (t_blk % 2 == 1, s_blk % 2 == 0)
            self.off[m] = jnp.logical_and(pair, jnp.logical_and(*lo_hi))
            m *= 2


def _gd_prep(q, k, v, ab, dirs, masks, bd):
    grp = lax.broadcasted_iota(jnp.int32, (CH, HW), 1) // HD

    def widen(a, col0):
        out = jnp.zeros((CH, HW), F32)
        for h in range(NH):
            out = jnp.where(grp == h, jnp.broadcast_to(a[:, col0 + h:col0 + h + 1], (CH, HW)), out)
        return out

    ix = range(len(q))
    mk = [masks[d] for d in dirs]
    la = [widen(ab[j], dirs[j] * NH) for j in ix]
    beta = [widen(ab[j], 2 * NH + dirs[j] * NH) for j in ix]
    gc = [_dot_01(mk[j].tri, la[j]) for j in ix]
    diff = [_dot_01(mk[j].tri, jnp.where(mk[j].later, la[j], 0.0)) for j in ix]
    kb = [_bd4(k[j], bd) for j in ix]
    kk = [_dot_nt(k[j].astype(BF16), kb[j]) for j in ix]
    qk = [_dot_nt(q[j].astype(BF16), kb[j]) for j in ix]
    dm = [jnp.where(mk[j].valid, jnp.exp(jnp.minimum(diff[j], 0.0)), 0.0) for j in ix]
    n = [jnp.where(mk[j].strict, beta[j] * kk[j] * dm[j], 0.0) for j in ix]

    tinv = [mk[j].eye - jnp.where(mk[j].off[1], n[j], 0.0) for j in ix]
    m = 2
    while m < CH:
        y = [_dot(tinv[j].astype(BF16), _bd4(jnp.where(mk[j].off[m], n[j], 0.0), bd)) for j in ix]
        tinv = [tinv[j] - _dot(y[j].astype(BF16), _bd4(tinv[j], bd)) for j in ix]
        m *= 2
    tb = [t.astype(BF16) for t in tinv]
    eg = [jnp.exp(g) for g in gc]
    u = [_dot(tb[j], _bd4(v[j] * beta[j], bd)) for j in ix]
    w = [_dot(tb[j], _bd4(k[j] * beta[j] * eg[j], bd)) for j in ix]
    sc = [(qk[j] * dm[j]).astype(BF16) for j in ix]
    qeff = [(q[j] * eg[j] - _dot(sc[j], _bd4(w[j], bd))).astype(BF16) for j in ix]
    oc = [_dot(sc[j], _bd4(u[j], bd)) for j in ix]
    gl = [gc[j][mk[j].last:mk[j].last + 1, :] for j in ix]
    kd = [(k[j] * jnp.exp(gl[j] - gc[j])).astype(BF16) for j in ix]
    kw = [jnp.where(bd, _dot_tn(kd[j], w[j].astype(BF16)), 0.0).astype(BF16) for j in ix]
    ku = [jnp.where(bd, _dot_tn(kd[j], u[j].astype(BF16)), 0.0) for j in ix]
    a = [jnp.exp(g) for g in gl]
    return qeff, oc, kw, ku, a


def _gd_kernel(qf_ref, kf_ref, vf_ref, abf_ref, qb_ref, kb_ref, vb_ref, abb_ref,
               of_ref, ob_ref, sf_scr, sb_scr):
    @pl.when(pl.program_id(1) == 0)
    def _():
        sf_scr[...] = jnp.zeros_like(sf_scr)
        sb_scr[...] = jnp.zeros_like(sb_scr)

    bd = _bd_mask()
    n_ch = TM // CH
    refs = ((qf_ref, kf_ref, vf_ref, abf_ref), (qb_ref, kb_ref, vb_ref, abb_ref))
    o_refs = (of_ref, ob_ref)
    items = [(d, (n_ch - 1 - j) if d else j) for j in range(n_ch) for d in range(2)]
    rows = [slice(ic * CH, (ic + 1) * CH) for _, ic in items]
    dirs = [d for d, _ in items]
    load = lambda which: [refs[d][which][0, r, :] for d, r in zip(dirs, rows)]
    qeff, oc, kw, ku, a = _gd_prep(load(0), load(1), load(2), load(3), dirs, (_GdMasks(False), _GdMasks(True)), bd)

    s = [sf_scr[...], sb_scr[...]]
    for j, (d, _) in enumerate(items):
        sb = s[d].astype(BF16)
        o_refs[d][0, rows[j], :] = _dot(qeff[j], sb) + oc[j]
        s[d] = s[d] * a[j] - _dot(kw[j], sb) + ku[j]
    sf_scr[...] = s[0]
    sb_scr[...] = s[1]


def _gd_call(gq, gk, gv, gab):
    bsz = gq.shape[0]
    f = lambda b, i: (b, i, 0)
    g0 = lambda b, i: (b, _bwd_tile(i), 0)
    blk = lambda m: pl.BlockSpec((1, TM, HW), m)
    abs_ = lambda m: pl.BlockSpec((1, TM, 128), m)
    return pl.pallas_call(
        _gd_kernel,
        grid=(bsz, NT),
        in_specs=[blk(f), blk(f), blk(f), abs_(f), blk(g0), blk(g0), blk(g0), abs_(g0)],
        out_specs=[blk(f), blk(g0)],
        out_shape=[jax.ShapeDtypeStruct((bsz, T, HW), F32)] * 2,
        scratch_shapes=[pltpu.VMEM((HW, HW), F32)] * 2,
        compiler_params=_cparams(("parallel", "arbitrary")),
    )(gq, gk, gv, gab, gq, gk, gv, gab)


def _outproj_kernel(*refs, with_ctx, split):
    if split:
        ctx_ref, refs = refs[0], refs[1:]
    (x_ref, oal_ref, oac_ref, hof_ref, hob_ref, hsg_ref, gof_ref, gob_ref, gsg_ref,
     wo_ref, hng_ref, gng_ref, g1_ref, gate_ref, o_ref) = refs
    ones_bd = _ones_bd()
    oa = oal_ref[0]
    x = x_ref[0]
    if with_ctx:
        oa = jnp.where(pl.program_id(1) == 0, oac_ref[0], oa)
    if split:
        x = jnp.where(pl.program_id(1) == 0, ctx_ref[0], x)

    def finish(of_ref, ob_ref, sg_ref, ng_ref):
        o = of_ref[0] + ob_ref[0]
        ms = _head_sum(o * o, ones_bd) * (1.0 / HD)
        return (o * lax.rsqrt(ms + EPS) * ng_ref[...] * sg_ref[0]).astype(BF16)

    ob = finish(hof_ref, hob_ref, hsg_ref, hng_ref)
    oc = finish(gof_ref, gob_ref, gsg_ref, gng_ref)
    mix = (_dot(oa, wo_ref[0:512, :]) + _dot(ob, wo_ref[512:768, :]) + _dot(oc, wo_ref[768:1024, :]))
    ms = jnp.mean(mix * mix, axis=-1, keepdims=True)
    o_ref[0] = x + gate_ref[0, 0] * (mix * lax.rsqrt(ms + EPS) * g1_ref[...])


def _outproj_call(xs, oa_lat, oa_ctx, hof, hob, hsg, gof, gob, gsg, wo, hng, gng, g1, gate, with_ctx):
    split = isinstance(xs, tuple)
    assert with_ctx or not split
    bsz = oa_lat.shape[0]
    t0 = 0 if with_ctx else 1
    row = lambda wd: pl.BlockSpec((1, TM, wd), lambda b, i: (b, i + t0, 0))
    if split:
        x_specs = [pl.BlockSpec((1, TM, D), lambda b, i: (b, 0, 0)),
                   pl.BlockSpec((1, TM, D), lambda b, i: (b, jnp.maximum(i - 1, 0), 0))]
        x_args = list(xs)
    else:
        x_specs, x_args = [row(D)], [xs]
    oal = pl.BlockSpec((1, TM, 512), lambda b, i: (b, jnp.maximum(i + t0 - 1, 0), 0))
    oac = pl.BlockSpec((1, TM, 512), lambda b, i: (b, 0, 0))
    mod = pl.BlockSpec((1, 1, 1, D), lambda b, i: (b, _sel(i + t0, 1), 0, 0))
    return pl.pallas_call(
        functools.partial(_outproj_kernel, with_ctx=with_ctx, split=split),
        grid=(bsz, NT - t0),
        in_specs=x_specs + [oal, oac, row(HW), row(HW), row(HW), row(HW), row(HW), row(HW),
                            _const_spec((D, D)), _const_spec((1, HW)), _const_spec((1, HW)), _const_spec((1, D)),
                            mod],
        out_specs=pl.BlockSpec((1, TM, D), lambda b, i: (b, i, 0)),
        out_shape=jax.ShapeDtypeStruct((bsz, (NT - t0) * TM, D), F32),
        compiler_params=_cparams(("parallel", "arbitrary")),
    )(*x_args, oa_lat, oa_ctx, hof, hob, hsg, gof, gob, gsg, wo, hng, gng, g1, gate)


def _ffn_kernel(xp_ref, xm_ref, xn_ref, a_ref, s_ref, wup_ref, cw_ref, cb_ref, wdn_ref, g3_ref, gate_ref,
                o_ref, h_scr, u_scr, act_scr, *, nct, nt):
    _fill_h(h_scr, xp_ref[0], xm_ref[0], xn_ref[0], a_ref[0, 0], s_ref[0, 0], pl.program_id(1), nct, nt)
    h = h_scr[...]
    for cidx in range(D_FF // FF_CW):
        lo = cidx * FF_CW
        u_scr[:, 0:FF_CW] = _dot(h, wup_ref[:, lo:lo + FF_CW])
        u_scr[:, FF_CW:2 * FF_CW] = _dot(h, wup_ref[:, D_FF + lo:D_FF + lo + FF_CW])
        wg = cw_ref[:, lo:lo + FF_CW]
        wv = cw_ref[:, D_FF + lo:D_FF + lo + FF_CW]
        g = (u_scr[HALO - 1:HALO - 1 + TM, 0:FF_CW] * wg[0:1] + u_scr[HALO:HALO + TM, 0:FF_CW] * wg[1:2]
             + u_scr[HALO + 1:HALO + 1 + TM, 0:FF_CW] * wg[2:3] + cb_ref[:, lo:lo + FF_CW])
        vv = (u_scr[HALO - 1:HALO - 1 + TM, FF_CW:] * wv[0:1] + u_scr[HALO:HALO + TM, FF_CW:] * wv[1:2]
              + u_scr[HALO + 1:HALO + 1 + TM, FF_CW:] * wv[2:3] + cb_ref[:, D_FF + lo:D_FF + lo + FF_CW])
        act_scr[:, lo:lo + FF_CW] = (_silu(g) * vv).astype(BF16)
    ff = _dot(act_scr[...], wdn_ref[...])
    ms = jnp.mean(ff * ff, axis=-1, keepdims=True)
    o_ref[0] = xm_ref[0] + gate_ref[0, 0] * (ff * lax.rsqrt(ms + EPS) * g3_ref[...])


def _ffn_call(xs, a_ff, s_ff, wup, cw, cb, wdn, g3, gate, nct):
    bsz, rows, _ = xs.shape
    nt = rows // TM
    mod = _mod_spec(nct)
    return pl.pallas_call(
        functools.partial(_ffn_kernel, nct=nct, nt=nt),
        grid=(bsz, nt),
        in_specs=_halo_specs(D, nct, nt) + [
            mod, mod, _const_spec((D, 2 * D_FF)), _const_spec((3, 2 * D_FF)), _const_spec((1, 2 * D_FF)),
            _const_spec((D_FF, D)), _const_spec((1, D)), mod],
        out_specs=pl.BlockSpec((1, TM, D), lambda b, i: (b, i, 0)),
        out_shape=jax.ShapeDtypeStruct((bsz, rows, D), F32),
        scratch_shapes=[pltpu.VMEM((TM + 2 * HALO, D), BF16), pltpu.VMEM((TM + 2 * HALO, 2 * FF_CW), F32),
                        pltpu.VMEM((TM, D_FF), BF16)],
        compiler_params=_cparams(("parallel", "arbitrary")),
    )(xs, xs, xs, a_ff, s_ff, wup, cw, cb, wdn, g3, gate)


def _rope_tables():
    n_freq = HD // 4
    inv = ROPE_THETA ** (-jnp.arange(n_freq, dtype=F32) / n_freq)
    rows = jnp.repeat(jnp.arange(SEQ // GRID_W, dtype=F32), GRID_W)
    cols = jnp.tile(jnp.arange(GRID_W, dtype=F32), SEQ // GRID_W)
    ang = jnp.concatenate([rows[:, None] * inv, cols[:, None] * inv], axis=-1)
    cos, sin = jnp.cos(ang), jnp.sin(ang)
    cos_l = jnp.tile(jnp.concatenate([cos, cos], axis=-1), (1, 2))
    sin_l = jnp.tile(jnp.concatenate([-sin, sin], axis=-1), (1, 2))
    cos_t = jnp.concatenate([jnp.ones((CTX, 128), F32), cos_l], axis=0)
    sin_t = jnp.concatenate([jnp.zeros((CTX, 128), F32), sin_l], axis=0)
    return cos_t, sin_t


def kernel(x, c, ctx, c_ctx, ada_w, ada_b, norm_g, w_in, w_out, da_lambda, da_subln_g, hg_lb_logits, hg_norm_g,
           gd_conv_w, gd_a_log, gd_dt_bias, gd_norm_g, ffn_w_up, ffn_conv_w, ffn_conv_b, ffn_w_down):
    bsz = x.shape[0]
    depth = ada_w.shape[0]
    assert x.shape == (bsz, SEQ, D) and ctx.shape == (bsz, CTX, D) and bsz <= 8
    cos_t, sin_t = _rope_tables()
    lb_w = jax.nn.softmax(hg_lb_logits.astype(F32), axis=0)
    lb_all = jnp.cumsum(lb_w, axis=0) - lb_w[0]

    cond = jnp.zeros((16, D), F32).at[:bsz].set(jax.nn.silu(c)).at[8].set(jax.nn.silu(c_ctx))
    mods = _ada_call(cond, ada_w, ada_b)

    xs = (ctx.astype(F32), x.astype(F32))
    for layer in range(depth):
        need_ctx = layer < depth - 1
        lam_init = 0.8 - 0.6 * math.exp(-0.3 * layer)
        m = mods[layer].reshape(16, 6, D)
        mod = jnp.stack([jnp.broadcast_to(m[8], (bsz, 6, D)), m[:bsz]], axis=1)
        g = norm_g[layer].astype(F32)

        def vec(v):
            return v.reshape(bsz, 2, 1, D)

        a_in, s_in = vec(g[0] * (1.0 + mod[:, :, 1])), vec(mod[:, :, 0])
        a_ff, s_ff = vec(g[2] * (1.0 + mod[:, :, 4])), vec(mod[:, :, 3])
        gate1, gate2 = vec(mod[:, :, 2]), vec(mod[:, :, 5])

        wl = w_in[layer]
        wab = jnp.zeros((D, 128), F32).at[:, :4 * NH].set(wl[:, C_GDA:C_GDG])
        w = [wl[:, C_DAQ:C_DAV], wl[:, C_DAV:C_HGQ], wl[:, C_HGQ:C_GDQKV], wl[:, C_GDQKV:C_GDA], wab,
             wl[:, C_GDG:IN_COLS]]
        w = [t.astype(BF16) for t in w]
        lb = lb_all[layer].reshape(1, 2 * HW)
        nega = jnp.zeros((1, 128), F32).at[0, :2 * NH].set(-jnp.exp(gd_a_log[layer].astype(F32)).reshape(-1))
        dtb = jnp.zeros((1, 128), F32).at[0, :2 * NH].set(gd_dt_bias[layer].astype(F32).reshape(-1))
        consts = [1.0 - lb, jnp.log(jnp.maximum(lb, LB_FLOOR)), jnp.log1p(-lb),
                  gd_conv_w[layer].astype(F32), nega, dtb]
        (q, k, v, hq, hv, hk, hlf, hsg, gq, gk, gv, gab, gsg) = _inproj_call(
            xs, a_in, s_in, cos_t, sin_t, w, consts)

        lp = da_lambda[layer].astype(F32)
        lam = jnp.exp(jnp.sum(lp[0] * lp[1])) - jnp.exp(jnp.sum(lp[2] * lp[3])) + lam_init
        lam_arr = jnp.full((8, 128), lam, F32)
        g_arr = (da_subln_g[layer].astype(F32) * (1.0 - lam_init)).reshape(1, DV)
        oa_lat = _attn_call(q, k, v, lam_arr, g_arr, CTX, SEQ, T)
        oa_ctx = _attn_call(q, k, v, lam_arr, g_arr, 0, CTX, CTX) if need_ctx else oa_lat

        hof, hob = _hg_call(hq, hv, hk, hlf)
        gof, gob = _gd_call(gq, gk, gv, gab)

        hng = jnp.tile(hg_norm_g[layer].astype(F32), NH).reshape(1, HW)
        gng = jnp.tile(gd_norm_g[layer].astype(F32), NH).reshape(1, HW)
        xs = _outproj_call(xs, oa_lat, oa_ctx, hof, hob, hsg, gof, gob, gsg, w_out[layer].astype(BF16), hng, gng,
                           g[1].reshape(1, D), gate1, need_ctx)
        xs = _ffn_call(xs, a_ff, s_ff, ffn_w_up[layer].astype(BF16), ffn_conv_w[layer].astype(F32),
                       ffn_conv_b[layer].astype(F32).reshape(1, 2 * D_FF), ffn_w_down[layer].astype(BF16),
                       g[3].reshape(1, D), gate2, 1 if need_ctx else 0)
    return xs
```

```python
import functools
import math

import jax
import jax.numpy as jnp
import numpy as np
from jax import lax
from jax.experimental import pallas as pl
from jax.experimental.pallas import tpu as pltpu

F32 = jnp.float32
BF16 = jnp.bfloat16

D = 1024
CTX = 256
SEQ = 2048
T = CTX + SEQ
GRID_W = 64
ROPE_THETA = 10000.0
EPS = 1e-6
LB_FLOOR = 1e-30
NH = 4
HD = 64
DV = 128
HW = NH * HD
D_FF = 2816
TM = 256
NT = T // TM
HALO = 16
TQ = 128
ATT_QB = 256
ATT_QIN = 4
Q_SCALE = HD ** -0.5 * math.log2(math.e)
BLK = 16
HG_NV = TM // 8
assert HG_NV == 2 * BLK
CH = 64
FF_CW = 256
VMEM_LIMIT = 56 * 1024 * 1024

C_DAQ, C_DAK, C_DAV = 0, 512, 1024
C_HGQ, C_HGI, C_HGF, C_HGG = 1536, 1792, 2048, 2560
C_GDQKV, C_GDA, C_GDB, C_GDG = 2816, 3584, 3592, 3600
IN_COLS = 3856


def _cparams(sem):
    return pltpu.CompilerParams(dimension_semantics=sem, vmem_limit_bytes=VMEM_LIMIT)


def _const_spec(shape):
    n = len(shape)
    return pl.BlockSpec(shape, lambda *_: (0,) * n)


def _sigmoid(x):
    return 1.0 / (1.0 + jnp.exp(-x))


def _silu(x):
    return x * _sigmoid(x)


def _softplus(x):
    return jnp.maximum(x, 0.0) + jnp.log(1.0 + jnp.exp(-jnp.abs(x)))


def _dot(a, b):
    return jnp.dot(a, b, preferred_element_type=F32)


def _dot_nt(a, b):
    return lax.dot_general(a, b, (((1,), (1,)), ((), ())), preferred_element_type=F32)


def _dot_tn(a, b):
    return lax.dot_general(a, b, (((0,), (0,)), ((), ())), preferred_element_type=F32)


def _dot_hi(a, b):
    return jnp.dot(a, b, preferred_element_type=F32, precision=lax.Precision.HIGHEST)


def _head_sum(x, ones_bd):
    return _dot(x.astype(BF16), ones_bd)


def _ones_bd():
    r = lax.broadcasted_iota(jnp.int32, (HW, HW), 0) // HD
    c = lax.broadcasted_iota(jnp.int32, (HW, HW), 1) // HD
    return jnp.where(r == c, 1.0, 0.0).astype(BF16)


def _bd_mask():
    r = lax.broadcasted_iota(jnp.int32, (HW, HW), 0) // HD
    c = lax.broadcasted_iota(jnp.int32, (HW, HW), 1) // HD
    return r == c


def _normed(x, a, s):
    ms = jnp.mean(x * x, axis=-1, keepdims=True)
    return (x * lax.rsqrt(ms + EPS)) * a + s


def _ada_kernel(c_ref, w_ref, b_ref, o_ref):
    o_ref[0] = _dot(c_ref[...].astype(BF16), w_ref[0].astype(BF16)) + b_ref[0]


def _ada_call(cond, ada_w, ada_b):
    depth = ada_w.shape[0]
    nc = 6 * D
    cw = 1536
    return pl.pallas_call(
        _ada_kernel,
        grid=(depth, nc // cw),
        in_specs=[pl.BlockSpec((16, D), lambda l, j: (0, 0)),
                  pl.BlockSpec((1, D, cw), lambda l, j: (l, 0, j)),
                  pl.BlockSpec((1, 1, cw), lambda l, j: (l, 0, j))],
        out_specs=pl.BlockSpec((1, 16, cw), lambda l, j: (l, 0, j)),
        out_shape=jax.ShapeDtypeStruct((depth, 16, nc), F32),
        compiler_params=_cparams(("arbitrary", "arbitrary")),
    )(cond, ada_w, ada_b.reshape(depth, 1, nc))


def _sel(i, nct):
    return jnp.where(i >= nct, 1, 0)


def _prev_ok(i, nct):
    return i > nct


def _next_ok(i, nct, nt):
    return jnp.logical_and(i >= nct, i <= nt - 2)


def _halo_specs(width, nct, nt):
    per = TM // HALO
    return [pl.BlockSpec((1, HALO, width), lambda b, i: (b, i * per - jnp.where(_prev_ok(i, nct), 1, 0), 0)),
            pl.BlockSpec((1, TM, width), lambda b, i: (b, i, 0)),
            pl.BlockSpec((1, HALO, width), lambda b, i: (b, (i + 1) * per - jnp.where(_next_ok(i, nct, nt), 0, 1), 0))]


def _mod_spec(nct):
    return pl.BlockSpec((1, 1, 1, D), lambda b, i: (b, _sel(i, nct), 0, 0))


def _fill_h(h_scr, xp, xm, xn, a, s, i, nct, nt):
    hp = _normed(xp, a, s)
    hn = _normed(xn, a, s)
    h_scr[0:HALO, :] = jnp.where(_prev_ok(i, nct), hp, 0.0).astype(BF16)
    h_scr[HALO:HALO + TM, :] = _normed(xm, a, s).astype(BF16)
    h_scr[HALO + TM:, :] = jnp.where(_next_ok(i, nct, nt), hn, 0.0).astype(BF16)


def _conv3(u_scr, w_ref):
    return (u_scr[HALO - 1:HALO - 1 + TM, :] * w_ref[0:1, :]
            + u_scr[HALO:HALO + TM, :] * w_ref[1:2, :]
            + u_scr[HALO + 1:HALO + 1 + TM, :] * w_ref[2:3, :])


def _inproj_kernel(*refs, split):
    if split:
        ctx_ref, refs = refs[0], refs[1:]
    (xp_ref, xm_ref, xn_ref, a_ref, s_ref, cos_ref, sin_ref,
     wqk_ref, wv_ref, whg_ref, wgq_ref, wab_ref, wgg_ref,
     lb1m_ref, lbm_ref, gconv_ref, nega_ref, dtb_ref,
     q_ref, k_ref, v_ref, hq_ref, hv_ref, hk_ref, hlf_ref, hsg_ref,
     gq_ref, gk_ref, gv_ref, gab_ref, gsg_ref,
     h_scr, u_scr) = refs
    i = pl.program_id(1)
    xm = jnp.where(i == 0, ctx_ref[0], xm_ref[0]) if split else xm_ref[0]
    _fill_h(h_scr, xp_ref[0], xm, xn_ref[0], a_ref[0, 0], s_ref[0, 0], i, 1, NT)
    h = h_scr[HALO:HALO + TM, :]

    z = _dot(h, wqk_ref[...])
    z_v = _dot(h, wv_ref[...])
    z_hg = _dot(h, whg_ref[...])
    u_scr[...] = _dot(h_scr[...], wgq_ref[...])
    z_ab = _dot(h, wab_ref[...])
    z_gg = _dot(h, wgg_ref[...])

    lane = lax.broadcasted_iota(jnp.int32, (TM, 128), 1)
    first_half = (lane % HD) < (HD // 2)
    cs, sn = cos_ref[...], sin_ref[...]
    for j in range(8):
        xj = z[:, j * 128:(j + 1) * 128]
        sw = jnp.where(first_half, pltpu.roll(xj, 128 - HD // 2, 1), pltpu.roll(xj, HD // 2, 1))
        r = xj * cs + sw * sn
        if j < 4:
            q_ref[0, :, j * 128:(j + 1) * 128] = (r * Q_SCALE).astype(BF16)
        else:
            k_ref[0, :, (j - 4) * 128:(j - 3) * 128] = r.astype(BF16)
    v_ref[0] = z_v.astype(BF16)

    z = z_hg
    hq_ref[0] = _silu(z[:, 0:HW]).astype(BF16)
    hv_ref[0] = z[:, HW:2 * HW].astype(BF16)
    sg = _sigmoid(z[:, 2 * HW:4 * HW])
    hk_ref[0] = (lb1m_ref[...] * (1.0 - sg)).astype(BF16)
    hlf_ref[0] = jnp.log(lbm_ref[...] + lb1m_ref[...] * sg)
    hsg_ref[0] = _silu(z[:, 4 * HW:5 * HW]).astype(BF16)

    y = _silu(_conv3(u_scr, gconv_ref))
    ones_bd = _ones_bd()
    qg, kg = y[:, 0:HW], y[:, HW:2 * HW]
    gq_ref[0] = (qg * lax.rsqrt(_head_sum(qg * qg, ones_bd) + EPS) * (HD ** -0.5)).astype(BF16)
    gk_ref[0] = (kg * lax.rsqrt(_head_sum(kg * kg, ones_bd) + EPS)).astype(BF16)
    gv_ref[0] = y[:, 2 * HW:3 * HW].astype(BF16)
    gab_ref[0] = jnp.where(lane < 2 * NH, nega_ref[...] * _softplus(z_ab + dtb_ref[...]), _sigmoid(z_ab))
    gsg_ref[0] = _silu(z_gg).astype(BF16)


def _split_x_specs():
    per = TM // HALO
    lat = lambda i: jnp.maximum(i - 1, 0)
    return [pl.BlockSpec((1, TM, D), lambda b, i: (b, 0, 0)),
            pl.BlockSpec((1, HALO, D), lambda b, i: (b, lat(i) * per - jnp.where(_prev_ok(i, 1), 1, 0), 0)),
            pl.BlockSpec((1, TM, D), lambda b, i: (b, lat(i), 0)),
            pl.BlockSpec((1, HALO, D), lambda b, i: (b, (lat(i) + 1) * per - jnp.where(_next_ok(i, 1, NT), 0, 1), 0))]


def _inproj_call(xs, a_in, s_in, cos_t, sin_t, w, consts):
    split = isinstance(xs, tuple)
    x_args = (xs[0], xs[1], xs[1], xs[1]) if split else (xs, xs, xs)
    bsz = x_args[0].shape[0]
    row = lambda wd: pl.BlockSpec((1, TM, wd), lambda b, i: (b, i, 0))
    tab = pl.BlockSpec((TM, 128), lambda b, i: (i, 0))
    out_w = [(512, BF16), (512, BF16), (512, BF16), (HW, BF16), (HW, BF16), (2 * HW, BF16), (2 * HW, F32),
             (HW, BF16), (HW, BF16), (HW, BF16), (HW, BF16), (128, F32), (HW, BF16)]
    return pl.pallas_call(
        functools.partial(_inproj_kernel, split=split),
        grid=(bsz, NT),
        in_specs=(_split_x_specs() if split else _halo_specs(D, 1, NT)) + [_mod_spec(1), _mod_spec(1), tab, tab]
        + [_const_spec(x.shape) for x in w] + [_const_spec(x.shape) for x in consts],
        out_specs=[row(wd) for wd, _ in out_w],
        out_shape=[jax.ShapeDtypeStruct((bsz, T, wd), dt) for wd, dt in out_w],
        scratch_shapes=[pltpu.VMEM((TM + 2 * HALO, D), BF16), pltpu.VMEM((TM + 2 * HALO, 3 * HW), F32)],
        compiler_params=_cparams(("parallel", "arbitrary")),
    )(*x_args, a_in, s_in, cos_t, sin_t, *w, *consts)


def _attn_kernel(*refs, n_qin):
    q_refs = refs[:n_qin]
    k_ref, v_ref, lam_ref, g_ref, o_ref = refs[n_qin:]
    k, v = k_ref[0], v_ref[0]
    lane = lax.broadcasted_iota(jnp.int32, (TQ, 128), 1)
    zero = jnp.zeros((TQ, 128), BF16)
    per = ATT_QB // TQ
    ix = range(n_qin * per)
    q = [q_refs[j // per][0, (j % per) * TQ:(j % per + 1) * TQ, :] for j in ix]
    qs = [jnp.concatenate([jnp.where(lane < HD, q[j], zero), jnp.where(lane >= HD, q[j], zero)], axis=0) for j in ix]
    st = [_dot_nt(k, qs[j]) for j in ix]
    m = [jnp.max(st[j], axis=0, keepdims=True) for j in ix]
    p = [jnp.exp2(st[j] - m[j]) for j in ix]
    l = [jnp.sum(p[j], axis=0, keepdims=True) for j in ix]
    ot = [_dot_tn(v, p[j].astype(BF16)) * (1.0 / l[j]) for j in ix]
    for j in ix:
        d = ot[j][:, :TQ] - lam_ref[0:1, :] * ot[j][:, TQ:]
        ms = jnp.mean(d * d, axis=0, keepdims=True)
        dn = d * lax.rsqrt(ms + EPS)
        o_ref[0, j * TQ:(j + 1) * TQ, :] = (dn.T * g_ref[...]).astype(BF16)


def _attn_call(q, k, v, lam_arr, g_arr, q_row0, n_rows, tk):
    bsz = q.shape[0]
    n_qin = min(ATT_QIN, n_rows // ATT_QB)
    tq = n_qin * ATT_QB
    assert q_row0 % ATT_QB == 0 and n_rows % tq == 0
    qoff = q_row0 // ATT_QB

    def q_spec(j):
        return pl.BlockSpec((1, ATT_QB, 128), lambda b, h, i: (b, qoff + i * n_qin + j, h))

    return pl.pallas_call(
        functools.partial(_attn_kernel, n_qin=n_qin),
        grid=(bsz, NH, n_rows // tq),
        in_specs=[q_spec(j) for j in range(n_qin)]
        + [pl.BlockSpec((1, tk, 128), lambda b, h, i: (b, 0, h)),
           pl.BlockSpec((1, tk, 128), lambda b, h, i: (b, 0, h)),
           _const_spec((8, 128)), _const_spec((1, 128))],
        out_specs=pl.BlockSpec((1, tq, 128), lambda b, h, i: (b, i, h)),
        out_shape=jax.ShapeDtypeStruct((bsz, n_rows, 512), BF16),
        compiler_params=_cparams(("parallel", "parallel", "arbitrary")),
    )(*([q] * n_qin), k, v, lam_arr, g_arr)


def _bwd_tile(i):
    return jnp.where(i == 0, 0, NT - i)


def _shift_rows(x, n, reverse):
    if n == 0:
        return x
    rows = x.shape[0]
    return pltpu.roll(x, (rows - n) if reverse else n, 0)


def _hg_kernel(qf_ref, vf_ref, kf_ref, lf_ref, qb_ref, vb_ref, kb_ref, lb_ref,
               of_ref, ob_ref, sf_scr, sb_scr):
    @pl.when(pl.program_id(1) == 0)
    def _():
        sf_scr[...] = jnp.zeros_like(sf_scr)
        sb_scr[...] = jnp.zeros_like(sb_scr)

    q = (qf_ref[0].astype(F32), qb_ref[0].astype(F32))
    k = (kf_ref[0].astype(F32), kb_ref[0].astype(F32))
    v = (vf_ref[0].astype(F32), vb_ref[0].astype(F32))
    logf = (lf_ref[0], lb_ref[0])
    o_refs = (of_ref, ob_ref)
    dirs = (0, 1)

    r = lax.broadcasted_iota(jnp.int32, (TM, TM), 0)
    c = lax.broadcasted_iota(jnp.int32, (TM, TM), 1)
    same = (r // BLK) == (c // BLK)
    tri = [jnp.where(jnp.logical_and(same, (c >= r) if d else (c <= r)), 1.0, 0.0).astype(BF16) for d in dirs]
    blk = jnp.where(same, 1.0, 0.0).astype(BF16)
    bl = [_dot_01(tri[d], logf[d]) for d in dirs]
    tot = [_dot_01(blk, logf[d]) for d in dirs]
    qdb = [(q[d] * jnp.exp(bl[d])).astype(BF16) for d in dirs]
    kd = [(k[d] * jnp.exp(tot[d] - bl[d])).astype(BF16) for d in dirs]
    e_blk = [jnp.exp(tot[d]) for d in dirs]
    f = [jnp.exp(logf[d]) for d in dirs]

    def regroup(x):
        return jnp.swapaxes(x.reshape(8, HG_NV, HW), 0, 1).reshape(2, BLK, 8, HW)

    q4 = [regroup(q[d]) for d in dirs]
    k4 = [regroup(k[d]) for d in dirs]
    v4 = [regroup(v[d]) for d in dirs]
    f4 = [regroup(f[d]) for d in dirs]
    ones_bd = _ones_bd()
    e = [None, None]
    o4 = [None, None]
    for n in range(BLK):
        ln = BLK - n
        for d in dirs:
            qs, ks = (slice(0, ln), slice(n, BLK)) if d else (slice(n, BLK), slice(0, ln))
            if n == 0:
                pn = q4[d] * k4[d]
            else:
                fs = slice(n - 1, n - 1 + ln) if d else slice(1, 1 + ln)
                e[d] = f4[d][:, fs] if n == 1 else (e[d][:, :ln] if d else e[d][:, 1:]) * f4[d][:, fs]
                pn = q4[d][:, qs] * k4[d][:, ks] * e[d]
            a = _dot(pn.reshape(2 * ln * 8, HW).astype(BF16), ones_bd).reshape(2, ln, 8, HW) * v4[d][:, ks]
            if n == 0:
                o4[d] = a
            else:
                pad = jnp.zeros((2, n, 8, HW), F32)
                o4[d] = o4[d] + jnp.concatenate([a, pad] if d else [pad, a], axis=1)
    o_band = [jnp.swapaxes(o4[d].reshape(HG_NV, 8, HW), 0, 1).reshape(TM, HW) for d in dirs]

    lane = lax.broadcasted_iota(jnp.int32, (HD, 128), 1)
    own = [(lane >= HD) if h % 2 else (lane < HD) for h in range(NH)]
    tile = [slice(128 * (h // 2), 128 * (h // 2) + 128) for h in range(NH)]
    zero_t = jnp.zeros((HD, 128), BF16)
    vb = [v[d].astype(BF16) for d in dirs]
    nb = TM // BLK
    s_scr = (sf_scr, sb_scr)
    s = [[s_scr[d][h] for h in range(NH)] for d in dirs]
    for j in range(nb):
        for d in dirs:
            ib = nb - 1 - j if d else j
            rows = slice(ib * BLK, (ib + 1) * BLK)
            s_bd = jnp.concatenate(
                [jnp.concatenate([s[d][h].astype(BF16) if c == h // 2 else zero_t for c in range(HW // 128)], axis=1)
                 for h in range(NH)], axis=0)
            o_refs[d][0, rows, :] = o_band[d][rows] + _dot_nt(qdb[d][rows], s_bd)
            u = _dot_tn(vb[d][rows], kd[d][rows])
            e_row = e_blk[d][ib * BLK:ib * BLK + 1, :]
            for h in range(NH):
                s[d][h] = s[d][h] * e_row[:, tile[h]] + jnp.where(own[h], u[h * HD:(h + 1) * HD, tile[h]], 0.0)
    for d in dirs:
        for h in range(NH):
            s_scr[d][h] = s[d][h]


def _hg_call(hq, hv, hk, hlf):
    bsz = hq.shape[0]
    f = lambda b, i: (b, i, 0)
    g0 = lambda b, i: (b, _bwd_tile(i), 0)
    g1 = lambda b, i: (b, _bwd_tile(i), 1)
    blk = lambda m: pl.BlockSpec((1, TM, HW), m)
    return pl.pallas_call(
        _hg_kernel,
        grid=(bsz, NT),
        in_specs=[blk(f), blk(f), blk(f), blk(f), blk(g0), blk(g0), blk(g1), blk(g1)],
        out_specs=[blk(f), blk(g0)],
        out_shape=[jax.ShapeDtypeStruct((bsz, T, HW), F32)] * 2,
        scratch_shapes=[pltpu.VMEM((NH, HD, 128), F32)] * 2,
        compiler_params=_cparams(("parallel", "arbitrary")),
    )(hq, hv, hk, hlf, hq, hv, hk, hlf)


def _bd4(x, bd):
    return jnp.where(bd, jnp.concatenate([x.astype(BF16)] * NH, axis=0), jnp.zeros((), BF16))


def _split3(x):
    h = x.astype(BF16)
    r1 = x - h.astype(F32)
    m = r1.astype(BF16)
    l = (r1 - m.astype(F32)).astype(BF16)
    return jnp.concatenate([h, m, l], axis=0)


def _dot_01(sel01, x):
    return _dot(jnp.concatenate([sel01] * 3, axis=1), _split3(x))


class _GdMasks:
    def __init__(self, reverse):
        r = lax.broadcasted_iota(jnp.int32, (CH, CH), 0)
        c = lax.broadcasted_iota(jnp.int32, (CH, CH), 1)
        self.tri = jnp.where((c >= r) if reverse else (c <= r), 1.0, 0.0).astype(BF16)
        t_i = lax.broadcasted_iota(jnp.int32, (CH, HW), 0)
        s_i = lax.broadcasted_iota(jnp.int32, (CH, HW), 1) % HD
        self.later = (t_i < s_i) if reverse else (t_i > s_i)
        self.valid = (s_i >= t_i) if reverse else (s_i <= t_i)
        self.strict = (s_i > t_i) if reverse else (s_i < t_i)
        self.eye = jnp.where(t_i == s_i, 1.0, 0.0)
        self.last = 0 if reverse else CH - 1
        self.off = {}
        m = 1
        while m < CH:
            t_blk, s_blk = t_i // m, s_i // m
            pair = (t_blk // 2) == (s_blk // 2)
            lo_hi = (t_blk % 2 == 0, s_blk % 2 == 1) if reverse else (t_blk % 2 == 1, s_blk % 2 == 0)
            self.off[m] = jnp.logical_and(pair, jnp.logical_and(*lo_hi))
            m *= 2


def _gd_prep(q, k, v, ab, dirs, masks, bd):
    grp = lax.broadcasted_iota(jnp.int32, (CH, HW), 1) // HD

    def widen(a, col0):
        out = jnp.zeros((CH, HW), F32)
        for h in range(NH):
            out = jnp.where(grp == h, jnp.broadcast_to(a[:, col0 + h:col0 + h + 1], (CH, HW)), out)
        return out

    ix = range(len(q))
    mk = [masks[d] for d in dirs]
    la = [widen(ab[j], dirs[j] * NH) for j in ix]
    beta = [widen(ab[j], 2 * NH + dirs[j] * NH) for j in ix]
    gc = [_dot_01(mk[j].tri, la[j]) for j in ix]
    diff = [_dot_01(mk[j].tri, jnp.where(mk[j].later, la[j], 0.0)) for j in ix]
    kb = [_bd4(k[j], bd) for j in ix]
    kq = [_dot_nt(jnp.concatenate([k[j].astype(BF16), q[j].astype(BF16)], axis=0), kb[j]) for j in ix]
    kk = [t[:CH] for t in kq]
    qk = [t[CH:] for t in kq]
    dm = [jnp.where(mk[j].valid, jnp.exp(jnp.minimum(diff[j], 0.0)), 0.0) for j in ix]
    n = [jnp.where(mk[j].strict, beta[j] * kk[j] * dm[j], 0.0) for j in ix]

    tinv = [mk[j].eye - jnp.where(mk[j].off[1], n[j], 0.0) for j in ix]
    m = 2
    while m < CH:
        y = [_dot(tinv[j].astype(BF16), _bd4(jnp.where(mk[j].off[m], n[j], 0.0), bd)) for j in ix]
        tinv = [tinv[j] - _dot(y[j].astype(BF16), _bd4(tinv[j], bd)) for j in ix]
        m *= 2
    tb = [t.astype(BF16) for t in tinv]
    eg = [jnp.exp(g) for g in gc]
    u = [_dot(tb[j], _bd4(v[j] * beta[j], bd)) for j in ix]
    w = [_dot(tb[j], _bd4(k[j] * beta[j] * eg[j], bd)) for j in ix]
    sc = [(qk[j] * dm[j]).astype(BF16) for j in ix]
    qeff = [(q[j] * eg[j] - _dot(sc[j], _bd4(w[j], bd))).astype(BF16) for j in ix]
    oc = [_dot(sc[j], _bd4(u[j], bd)) for j in ix]
    gl = [gc[j][mk[j].last:mk[j].last + 1, :] for j in ix]
    kd = [(k[j] * jnp.exp(gl[j] - gc[j])).astype(BF16) for j in ix]
    kwu = [_dot_tn(kd[j], jnp.concatenate([w[j].astype(BF16), u[j].astype(BF16)], axis=1)) for j in ix]
    kw = [jnp.where(bd, t[:, :HW], 0.0).astype(BF16) for t in kwu]
    ku = [jnp.where(bd, t[:, HW:], 0.0) for t in kwu]
    a = [jnp.exp(g) for g in gl]
    return qeff, oc, kw, ku, a


def _gd_kernel(qf_ref, kf_ref, vf_ref, abf_ref, qb_ref, kb_ref, vb_ref, abb_ref,
               of_ref, ob_ref, sf_scr, sb_scr):
    @pl.when(pl.program_id(1) == 0)
    def _():
        sf_scr[...] = jnp.zeros_like(sf_scr)
        sb_scr[...] = jnp.zeros_like(sb_scr)

    bd = _bd_mask()
    n_ch = TM // CH
    refs = ((qf_ref, kf_ref, vf_ref, abf_ref), (qb_ref, kb_ref, vb_ref, abb_ref))
    o_refs = (of_ref, ob_ref)
    items = [(d, (n_ch - 1 - j) if d else j) for j in range(n_ch) for d in range(2)]
    rows = [slice(ic * CH, (ic + 1) * CH) for _, ic in items]
    dirs = [d for d, _ in items]
    load = lambda which: [refs[d][which][0, r, :] for d, r in zip(dirs, rows)]
    qeff, oc, kw, ku, a = _gd_prep(load(0), load(1), load(2), load(3), dirs, (_GdMasks(False), _GdMasks(True)), bd)

    s = [sf_scr[...], sb_scr[...]]
    for j, (d, _) in enumerate(items):
        sb = s[d].astype(BF16)
        o_refs[d][0, rows[j], :] = _dot(qeff[j], sb) + oc[j]
        s[d] = s[d] * a[j] - _dot(kw[j], sb) + ku[j]
    sf_scr[...] = s[0]
    sb_scr[...] = s[1]


def _gd_call(gq, gk, gv, gab):
    bsz = gq.shape[0]
    f = lambda b, i: (b, i, 0)
    g0 = lambda b, i: (b, _bwd_tile(i), 0)
    blk = lambda m: pl.BlockSpec((1, TM, HW), m)
    abs_ = lambda m: pl.BlockSpec((1, TM, 128), m)
    return pl.pallas_call(
        _gd_kernel,
        grid=(bsz, NT),
        in_specs=[blk(f), blk(f), blk(f), abs_(f), blk(g0), blk(g0), blk(g0), abs_(g0)],
        out_specs=[blk(f), blk(g0)],
        out_shape=[jax.ShapeDtypeStruct((bsz, T, HW), F32)] * 2,
        scratch_shapes=[pltpu.VMEM((HW, HW), F32)] * 2,
        compiler_params=_cparams(("parallel", "arbitrary")),
    )(gq, gk, gv, gab, gq, gk, gv, gab)


def _outproj_kernel(*refs, with_ctx, split):
    if split:
        ctx_ref, refs = refs[0], refs[1:]
    (x_ref, oal_ref, oac_ref, hof_ref, hob_ref, hsg_ref, gof_ref, gob_ref, gsg_ref,
     wo_ref, hng_ref, gng_ref, g1_ref, gate_ref, o_ref) = refs
    ones_bd = _ones_bd()
    oa = oal_ref[0]
    x = x_ref[0]
    if with_ctx:
        oa = jnp.where(pl.program_id(1) == 0, oac_ref[0], oa)
    if split:
        x = jnp.where(pl.program_id(1) == 0, ctx_ref[0], x)

    def finish(of_ref, ob_ref, sg_ref, ng_ref):
        o = of_ref[0] + ob_ref[0]
        ms = _head_sum(o * o, ones_bd) * (1.0 / HD)
        return (o * lax.rsqrt(ms + EPS) * ng_ref[...] * sg_ref[0]).astype(BF16)

    ob = finish(hof_ref, hob_ref, hsg_ref, hng_ref)
    oc = finish(gof_ref, gob_ref, gsg_ref, gng_ref)
    mix = (_dot(oa, wo_ref[0:512, :]) + _dot(ob, wo_ref[512:768, :]) + _dot(oc, wo_ref[768:1024, :]))
    ms = jnp.mean(mix * mix, axis=-1, keepdims=True)
    o_ref[0] = x + gate_ref[0, 0] * (mix * lax.rsqrt(ms + EPS) * g1_ref[...])


def _outproj_call(xs, oa_lat, oa_ctx, hof, hob, hsg, gof, gob, gsg, wo, hng, gng, g1, gate, with_ctx):
    split = isinstance(xs, tuple)
    assert with_ctx or not split
    bsz = oa_lat.shape[0]
    t0 = 0 if with_ctx else 1
    row = lambda wd: pl.BlockSpec((1, TM, wd), lambda b, i: (b, i + t0, 0))
    if split:
        x_specs = [pl.BlockSpec((1, TM, D), lambda b, i: (b, 0, 0)),
                   pl.BlockSpec((1, TM, D), lambda b, i: (b, jnp.maximum(i - 1, 0), 0))]
        x_args = list(xs)
    else:
        x_specs, x_args = [row(D)], [xs]
    oal = pl.BlockSpec((1, TM, 512), lambda b, i: (b, jnp.maximum(i + t0 - 1, 0), 0))
    oac = pl.BlockSpec((1, TM, 512), lambda b, i: (b, 0, 0))
    mod = pl.BlockSpec((1, 1, 1, D), lambda b, i: (b, _sel(i + t0, 1), 0, 0))
    return pl.pallas_call(
        functools.partial(_outproj_kernel, with_ctx=with_ctx, split=split),
        grid=(bsz, NT - t0),
        in_specs=x_specs + [oal, oac, row(HW), row(HW), row(HW), row(HW), row(HW), row(HW),
                            _const_spec((D, D)), _const_spec((1, HW)), _const_spec((1, HW)), _const_spec((1, D)),
                            mod],
        out_specs=pl.BlockSpec((1, TM, D), lambda b, i: (b, i, 0)),
        out_shape=jax.ShapeDtypeStruct((bsz, (NT - t0) * TM, D), F32),
        compiler_params=_cparams(("parallel", "arbitrary")),
    )(*x_args, oa_lat, oa_ctx, hof, hob, hsg, gof, gob, gsg, wo, hng, gng, g1, gate)


def _regroup_rows(x):
    return jnp.swapaxes(x.reshape(8, TM // 8, x.shape[1]), 0, 1).reshape(TM, x.shape[1])


def _ungroup_rows(x):
    return jnp.swapaxes(x.reshape(TM // 8, 8, x.shape[1]), 0, 1).reshape(TM, x.shape[1])


def _ffn_kernel(xp_ref, xm_ref, xn_ref, a_ref, s_ref, wup_ref, cw_ref, cb_ref, wdn_ref, g3_ref, gate_ref,
                o_ref, h_scr, act_scr, *, nct, nt):
    i = pl.program_id(1)
    a, s = a_ref[0, 0], s_ref[0, 0]
    ng = TM // 8
    xg = _regroup_rows(xm_ref[0])
    h_scr[0:TM, :] = _normed(xg, a, s).astype(BF16)
    hp = jnp.where(_prev_ok(i, nct), _normed(xp_ref[0, HALO - 8:HALO, :], a, s), 0.0)
    hn = jnp.where(_next_ok(i, nct, nt), _normed(xn_ref[0, 0:8, :], a, s), 0.0)
    h_scr[TM:TM + 16, :] = jnp.concatenate([hp, hn], axis=0).astype(BF16)
    h = h_scr[...]
    sub = lax.broadcasted_iota(jnp.int32, (8, FF_CW), 0)

    def conv(col0, lo):
        u = _dot(h, wup_ref[:, col0 + lo:col0 + lo + FF_CW])
        ur = u[0:TM].reshape(ng, 8, FF_CW)
        first = jnp.where(sub == 0, u[TM + 7:TM + 8], pltpu.roll(ur[ng - 1], 1, 0))
        last = jnp.where(sub == 7, u[TM + 8:TM + 9], pltpu.roll(ur[0], 7, 0))
        w = cw_ref[:, col0 + lo:col0 + lo + FF_CW]
        return (jnp.concatenate([first[None], ur[:-1]], axis=0) * w[0:1] + ur * w[1:2]
                + jnp.concatenate([ur[1:], last[None]], axis=0) * w[2:3] + cb_ref[:, col0 + lo:col0 + lo + FF_CW])

    for cidx in range(D_FF // FF_CW):
        lo = cidx * FF_CW
        act = _silu(conv(0, lo)) * conv(D_FF, lo)
        act_scr[:, lo:lo + FF_CW] = act.reshape(TM, FF_CW).astype(BF16)
    ff = _dot(act_scr[...], wdn_ref[...])
    ms = jnp.mean(ff * ff, axis=-1, keepdims=True)
    o_ref[0] = _ungroup_rows(xg + gate_ref[0, 0] * (ff * lax.rsqrt(ms + EPS) * g3_ref[...]))


def _ffn_call(xs, a_ff, s_ff, wup, cw, cb, wdn, g3, gate, nct):
    bsz, rows, _ = xs.shape
    nt = rows // TM
    mod = _mod_spec(nct)
    return pl.pallas_call(
        functools.partial(_ffn_kernel, nct=nct, nt=nt),
        grid=(bsz, nt),
        in_specs=_halo_specs(D, nct, nt) + [
            mod, mod, _const_spec((D, 2 * D_FF)), _const_spec((3, 2 * D_FF)), _const_spec((1, 2 * D_FF)),
            _const_spec((D_FF, D)), _const_spec((1, D)), mod],
        out_specs=pl.BlockSpec((1, TM, D), lambda b, i: (b, i, 0)),
        out_shape=jax.ShapeDtypeStruct((bsz, rows, D), F32),
        scratch_shapes=[pltpu.VMEM((TM + 16, D), BF16), pltpu.VMEM((TM, D_FF), BF16)],
        compiler_params=_cparams(("parallel", "arbitrary")),
    )(xs, xs, xs, a_ff, s_ff, wup, cw, cb, wdn, g3, gate)


def _rope_tables():
    n_freq = HD // 4
    inv = ROPE_THETA ** (-jnp.arange(n_freq, dtype=F32) / n_freq)
    rows = jnp.repeat(jnp.arange(SEQ // GRID_W, dtype=F32), GRID_W)
    cols = jnp.tile(jnp.arange(GRID_W, dtype=F32), SEQ // GRID_W)
    ang = jnp.concatenate([rows[:, None] * inv, cols[:, None] * inv], axis=-1)
    cos, sin = jnp.cos(ang), jnp.sin(ang)
    cos_l = jnp.tile(jnp.concatenate([cos, cos], axis=-1), (1, 2))
    sin_l = jnp.tile(jnp.concatenate([-sin, sin], axis=-1), (1, 2))
    cos_t = jnp.concatenate([jnp.ones((CTX, 128), F32), cos_l], axis=0)
    sin_t = jnp.concatenate([jnp.zeros((CTX, 128), F32), sin_l], axis=0)
    return cos_t, sin_t


def kernel(x, c, ctx, c_ctx, ada_w, ada_b, norm_g, w_in, w_out, da_lambda, da_subln_g, hg_lb_logits, hg_norm_g,
           gd_conv_w, gd_a_log, gd_dt_bias, gd_norm_g, ffn_w_up, ffn_conv_w, ffn_conv_b, ffn_w_down):
    bsz = x.shape[0]
    depth = ada_w.shape[0]
    assert x.shape == (bsz, SEQ, D) and ctx.shape == (bsz, CTX, D) and bsz <= 8
    cos_t, sin_t = _rope_tables()
    lb_w = jax.nn.softmax(hg_lb_logits.astype(F32), axis=0)
    lb_all = jnp.cumsum(lb_w, axis=0) - lb_w[0]

    cond = jnp.zeros((16, D), F32).at[:bsz].set(jax.nn.silu(c)).at[8].set(jax.nn.silu(c_ctx))
    mods = _ada_call(cond, ada_w, ada_b)

    xs = (ctx.astype(F32), x.astype(F32))
    for layer in range(depth):
        need_ctx = layer < depth - 1
        lam_init = 0.8 - 0.6 * math.exp(-0.3 * layer)
        m = mods[layer].reshape(16, 6, D)
        mod = jnp.stack([jnp.broadcast_to(m[8], (bsz, 6, D)), m[:bsz]], axis=1)
        g = norm_g[layer].astype(F32)

        def vec(v):
            return v.reshape(bsz, 2, 1, D)

        a_in, s_in = vec(g[0] * (1.0 + mod[:, :, 1])), vec(mod[:, :, 0])
        a_ff, s_ff = vec(g[2] * (1.0 + mod[:, :, 4])), vec(mod[:, :, 3])
        gate1, gate2 = vec(mod[:, :, 2]), vec(mod[:, :, 5])

        wl = w_in[layer]
        wab = jnp.zeros((D, 128), F32).at[:, :4 * NH].set(wl[:, C_GDA:C_GDG])
        w = [wl[:, C_DAQ:C_DAV], wl[:, C_DAV:C_HGQ], wl[:, C_HGQ:C_GDQKV], wl[:, C_GDQKV:C_GDA], wab,
             wl[:, C_GDG:IN_COLS]]
        w = [t.astype(BF16) for t in w]
        lb = lb_all[layer].reshape(1, 2 * HW)
        nega = jnp.zeros((1, 128), F32).at[0, :2 * NH].set(-jnp.exp(gd_a_log[layer].astype(F32)).reshape(-1))
        dtb = jnp.zeros((1, 128), F32).at[0, :2 * NH].set(gd_dt_bias[layer].astype(F32).reshape(-1))
        consts = [1.0 - lb, jnp.maximum(lb, LB_FLOOR), gd_conv_w[layer].astype(F32), nega, dtb]
        (q, k, v, hq, hv, hk, hlf, hsg, gq, gk, gv, gab, gsg) = _inproj_call(
            xs, a_in, s_in, cos_t, sin_t, w, consts)

        lp = da_lambda[layer].astype(F32)
        lam = jnp.exp(jnp.sum(lp[0] * lp[1])) - jnp.exp(jnp.sum(lp[2] * lp[3])) + lam_init
        lam_arr = jnp.full((8, 128), lam, F32)
        g_arr = (da_subln_g[layer].astype(F32) * (1.0 - lam_init)).reshape(1, DV)
        oa_lat = _attn_call(q, k, v, lam_arr, g_arr, CTX, SEQ, T)
        oa_ctx = _attn_call(q, k, v, lam_arr, g_arr, 0, CTX, CTX) if need_ctx else oa_lat

        hof, hob = _hg_call(hq, hv, hk, hlf)
        gof, gob = _gd_call(gq, gk, gv, gab)

        hng = jnp.tile(hg_norm_g[layer].astype(F32), NH).reshape(1, HW)
        gng = jnp.tile(gd_norm_g[layer].astype(F32), NH).reshape(1, HW)
        xs = _outproj_call(xs, oa_lat, oa_ctx, hof, hob, hsg, gof, gob, gsg, w_out[layer].astype(BF16), hng, gng,
                           g[1].reshape(1, D), gate1, need_ctx)
        xs = _ffn_call(xs, a_ff, s_ff, ffn_w_up[layer].astype(BF16), ffn_conv_w[layer].astype(F32),
                       ffn_conv_b[layer].astype(F32).reshape(1, 2 * D_FF), ffn_w_down[layer].astype(BF16),
                       g[3].reshape(1, D), gate2, 1 if need_ctx else 0)
    return xs
```

```python
import functools
import math

import jax
import jax.numpy as jnp
import numpy as np
from jax import lax
from jax.experimental import pallas as pl
from jax.experimental.pallas import tpu as pltpu

F32 = jnp.float32
BF16 = jnp.bfloat16

D = 1024
CTX = 256
SEQ = 2048
T = CTX + SEQ
GRID_W = 64
ROPE_THETA = 10000.0
EPS = 1e-6
LB_FLOOR = 1e-30
NH = 4
HD = 64
DV = 128
HW = NH * HD
D_FF = 2816
TM = 256
NT = T // TM
HALO = 16
TQ = 128
ATT_QB = 256
ATT_QIN = 4
Q_SCALE = HD ** -0.5 * math.log2(math.e)
BLK = 16
HG_NV = TM // 8
assert HG_NV == 2 * BLK
CH = 64
FF_CW = 256
VMEM_LIMIT = 56 * 1024 * 1024

C_DAQ, C_DAK, C_DAV = 0, 512, 1024
C_HGQ, C_HGI, C_HGF, C_HGG = 1536, 1792, 2048, 2560
C_GDQKV, C_GDA, C_GDB, C_GDG = 2816, 3584, 3592, 3600
IN_COLS = 3856


def _cparams(sem):
    return pltpu.CompilerParams(dimension_semantics=sem, vmem_limit_bytes=VMEM_LIMIT)


def _const_spec(shape):
    n = len(shape)
    return pl.BlockSpec(shape, lambda *_: (0,) * n)


def _sigmoid(x):
    return 1.0 / (1.0 + jnp.exp(-x))


def _silu(x):
    return x * _sigmoid(x)


def _softplus(x):
    return jnp.maximum(x, 0.0) + jnp.log(1.0 + jnp.exp(-jnp.abs(x)))


def _dot(a, b):
    return jnp.dot(a, b, preferred_element_type=F32)


def _dot_nt(a, b):
    return lax.dot_general(a, b, (((1,), (1,)), ((), ())), preferred_element_type=F32)


def _dot_tn(a, b):
    return lax.dot_general(a, b, (((0,), (0,)), ((), ())), preferred_element_type=F32)


def _dot_hi(a, b):
    return jnp.dot(a, b, preferred_element_type=F32, precision=lax.Precision.HIGHEST)


def _head_sum(x, ones_bd):
    return _dot(x.astype(BF16), ones_bd)


def _ones_bd():
    r = lax.broadcasted_iota(jnp.int32, (HW, HW), 0) // HD
    c = lax.broadcasted_iota(jnp.int32, (HW, HW), 1) // HD
    return jnp.where(r == c, 1.0, 0.0).astype(BF16)


def _bd_mask():
    r = lax.broadcasted_iota(jnp.int32, (HW, HW), 0) // HD
    c = lax.broadcasted_iota(jnp.int32, (HW, HW), 1) // HD
    return r == c


def _normed(x, a, s):
    ms = jnp.mean(x * x, axis=-1, keepdims=True)
    return (x * lax.rsqrt(ms + EPS)) * a + s


def _ada_kernel(c_ref, w_ref, b_ref, o_ref):
    o_ref[0] = _dot(c_ref[...].astype(BF16), w_ref[0].astype(BF16)) + b_ref[0]


def _ada_call(cond, ada_w, ada_b):
    depth = ada_w.shape[0]
    nc = 6 * D
    cw = 1536
    return pl.pallas_call(
        _ada_kernel,
        grid=(depth, nc // cw),
        in_specs=[pl.BlockSpec((16, D), lambda l, j: (0, 0)),
                  pl.BlockSpec((1, D, cw), lambda l, j: (l, 0, j)),
                  pl.BlockSpec((1, 1, cw), lambda l, j: (l, 0, j))],
        out_specs=pl.BlockSpec((1, 16, cw), lambda l, j: (l, 0, j)),
        out_shape=jax.ShapeDtypeStruct((depth, 16, nc), F32),
        compiler_params=_cparams(("arbitrary", "arbitrary")),
    )(cond, ada_w, ada_b.reshape(depth, 1, nc))


def _sel(i, nct):
    return jnp.where(i >= nct, 1, 0)


def _prev_ok(i, nct):
    return i > nct


def _next_ok(i, nct, nt):
    return jnp.logical_and(i >= nct, i <= nt - 2)


def _halo_specs(width, nct, nt):
    per = TM // HALO
    return [pl.BlockSpec((1, HALO, width), lambda b, i: (b, i * per - jnp.where(_prev_ok(i, nct), 1, 0), 0)),
            pl.BlockSpec((1, TM, width), lambda b, i: (b, i, 0)),
            pl.BlockSpec((1, HALO, width), lambda b, i: (b, (i + 1) * per - jnp.where(_next_ok(i, nct, nt), 0, 1), 0))]


def _mod_spec(nct):
    return pl.BlockSpec((1, 1, 1, D), lambda b, i: (b, _sel(i, nct), 0, 0))


def _fill_h(h_scr, xp, xm, xn, a, s, i, nct, nt):
    hp = _normed(xp, a, s)
    hn = _normed(xn, a, s)
    h_scr[0:HALO, :] = jnp.where(_prev_ok(i, nct), hp, 0.0).astype(BF16)
    h_scr[HALO:HALO + TM, :] = _normed(xm, a, s).astype(BF16)
    h_scr[HALO + TM:, :] = jnp.where(_next_ok(i, nct, nt), hn, 0.0).astype(BF16)


def _conv3(u_scr, w_ref):
    return (u_scr[HALO - 1:HALO - 1 + TM, :] * w_ref[0:1, :]
            + u_scr[HALO:HALO + TM, :] * w_ref[1:2, :]
            + u_scr[HALO + 1:HALO + 1 + TM, :] * w_ref[2:3, :])


def _inproj_kernel(*refs, split):
    if split:
        ctx_ref, refs = refs[0], refs[1:]
    (xp_ref, xm_ref, xn_ref, a_ref, s_ref, cos_ref, sin_ref,
     wqk_ref, wv_ref, whg_ref, wgq_ref, wab_ref, wgg_ref,
     lb1m_ref, lbm_ref, gconv_ref, nega_ref, dtb_ref,
     q_ref, k_ref, v_ref, hq_ref, hv_ref, hk_ref, hlf_ref, hsg_ref,
     gq_ref, gk_ref, gv_ref, gab_ref, gsg_ref,
     h_scr, u_scr) = refs
    i = pl.program_id(1)
    xm = jnp.where(i == 0, ctx_ref[0], xm_ref[0]) if split else xm_ref[0]
    _fill_h(h_scr, xp_ref[0], xm, xn_ref[0], a_ref[0, 0], s_ref[0, 0], i, 1, NT)
    h = h_scr[HALO:HALO + TM, :]

    z = _dot(h, wqk_ref[...])
    z_v = _dot(h, wv_ref[...])
    z_hg = _dot(h, whg_ref[...])
    u_scr[...] = _dot(h_scr[...], wgq_ref[...])
    z_ab = _dot(h, wab_ref[...])
    z_gg = _dot(h, wgg_ref[...])

    lane = lax.broadcasted_iota(jnp.int32, (TM, 128), 1)
    first_half = (lane % HD) < (HD // 2)
    cs, sn = cos_ref[...], sin_ref[...]
    for j in range(8):
        xj = z[:, j * 128:(j + 1) * 128]
        sw = jnp.where(first_half, pltpu.roll(xj, 128 - HD // 2, 1), pltpu.roll(xj, HD // 2, 1))
        r = xj * cs + sw * sn
        if j < 4:
            q_ref[0, :, j * 128:(j + 1) * 128] = (r * Q_SCALE).astype(BF16)
        else:
            k_ref[0, :, (j - 4) * 128:(j - 3) * 128] = r.astype(BF16)
    v_ref[0] = z_v.astype(BF16)

    z = z_hg
    hq_ref[0] = _silu(z[:, 0:HW]).astype(BF16)
    hv_ref[0] = z[:, HW:2 * HW].astype(BF16)
    sg = _sigmoid(z[:, 2 * HW:4 * HW])
    hk_ref[0] = (lb1m_ref[...] * (1.0 - sg)).astype(BF16)
    hlf_ref[0] = jnp.log(lbm_ref[...] + lb1m_ref[...] * sg)
    hsg_ref[0] = _silu(z[:, 4 * HW:5 * HW]).astype(BF16)

    y = _silu(_conv3(u_scr, gconv_ref))
    ones_bd = _ones_bd()
    qg, kg = y[:, 0:HW], y[:, HW:2 * HW]
    gq_ref[0] = (qg * lax.rsqrt(_head_sum(qg * qg, ones_bd) + EPS) * (HD ** -0.5)).astype(BF16)
    gk_ref[0] = (kg * lax.rsqrt(_head_sum(kg * kg, ones_bd) + EPS)).astype(BF16)
    gv_ref[0] = y[:, 2 * HW:3 * HW].astype(BF16)
    gab_ref[0] = jnp.where(lane < 2 * NH, nega_ref[...] * _softplus(z_ab + dtb_ref[...]), _sigmoid(z_ab))
    gsg_ref[0] = _silu(z_gg).astype(BF16)


def _split_x_specs():
    per = TM // HALO
    lat = lambda i: jnp.maximum(i - 1, 0)
    return [pl.BlockSpec((1, TM, D), lambda b, i: (b, 0, 0)),
            pl.BlockSpec((1, HALO, D), lambda b, i: (b, lat(i) * per - jnp.where(_prev_ok(i, 1), 1, 0), 0)),
            pl.BlockSpec((1, TM, D), lambda b, i: (b, lat(i), 0)),
            pl.BlockSpec((1, HALO, D), lambda b, i: (b, (lat(i) + 1) * per - jnp.where(_next_ok(i, 1, NT), 0, 1), 0))]


def _inproj_call(xs, a_in, s_in, cos_t, sin_t, w, consts):
    split = isinstance(xs, tuple)
    x_args = (xs[0], xs[1], xs[1], xs[1]) if split else (xs, xs, xs)
    bsz = x_args[0].shape[0]
    row = lambda wd: pl.BlockSpec((1, TM, wd), lambda b, i: (b, i, 0))
    tab = pl.BlockSpec((TM, 128), lambda b, i: (i, 0))
    out_w = [(512, BF16), (512, BF16), (512, BF16), (HW, BF16), (HW, BF16), (2 * HW, BF16), (2 * HW, F32),
             (HW, BF16), (HW, BF16), (HW, BF16), (HW, BF16), (128, F32), (HW, BF16)]
    return pl.pallas_call(
        functools.partial(_inproj_kernel, split=split),
        grid=(bsz, NT),
        in_specs=(_split_x_specs() if split else _halo_specs(D, 1, NT)) + [_mod_spec(1), _mod_spec(1), tab, tab]
        + [_const_spec(x.shape) for x in w] + [_const_spec(x.shape) for x in consts],
        out_specs=[row(wd) for wd, _ in out_w],
        out_shape=[jax.ShapeDtypeStruct((bsz, T, wd), dt) for wd, dt in out_w],
        scratch_shapes=[pltpu.VMEM((TM + 2 * HALO, D), BF16), pltpu.VMEM((TM + 2 * HALO, 3 * HW), F32)],
        compiler_params=_cparams(("parallel", "arbitrary")),
    )(*x_args, a_in, s_in, cos_t, sin_t, *w, *consts)


def _attn_kernel(*refs, n_qin):
    q_refs = refs[:n_qin]
    k_ref, v_ref, lam_ref, g_ref, o_ref = refs[n_qin:]
    k, v = k_ref[0], v_ref[0]
    lane = lax.broadcasted_iota(jnp.int32, (TQ, 128), 1)
    zero = jnp.zeros((TQ, 128), BF16)
    per = ATT_QB // TQ
    ix = range(n_qin * per)
    q = [q_refs[j // per][0, (j % per) * TQ:(j % per + 1) * TQ, :] for j in ix]
    qs = [jnp.concatenate([jnp.where(lane < HD, q[j], zero), jnp.where(lane >= HD, q[j], zero)], axis=0) for j in ix]
    st = [_dot_nt(k, qs[j]) for j in ix]
    m = [jnp.max(st[j], axis=0, keepdims=True) for j in ix]
    p = [jnp.exp2(st[j] - m[j]) for j in ix]
    l = [jnp.sum(p[j], axis=0, keepdims=True) for j in ix]
    ot = [_dot_tn(v, p[j].astype(BF16)) * (1.0 / l[j]) for j in ix]
    for j in ix:
        d = ot[j][:, :TQ] - lam_ref[0:1, :] * ot[j][:, TQ:]
        ms = jnp.mean(d * d, axis=0, keepdims=True)
        dn = d * lax.rsqrt(ms + EPS)
        o_ref[0, j * TQ:(j + 1) * TQ, :] = (dn.T * g_ref[...]).astype(BF16)


def _attn_call(q, k, v, lam_arr, g_arr, q_row0, n_rows, tk):
    bsz = q.shape[0]
    n_qin = min(ATT_QIN, n_rows // ATT_QB)
    tq = n_qin * ATT_QB
    assert q_row0 % ATT_QB == 0 and n_rows % tq == 0
    qoff = q_row0 // ATT_QB

    def q_spec(j):
        return pl.BlockSpec((1, ATT_QB, 128), lambda b, h, i: (b, qoff + i * n_qin + j, h))

    return pl.pallas_call(
        functools.partial(_attn_kernel, n_qin=n_qin),
        grid=(bsz, NH, n_rows // tq),
        in_specs=[q_spec(j) for j in range(n_qin)]
        + [pl.BlockSpec((1, tk, 128), lambda b, h, i: (b, 0, h)),
           pl.BlockSpec((1, tk, 128), lambda b, h, i: (b, 0, h)),
           _const_spec((8, 128)), _const_spec((1, 128))],
        out_specs=pl.BlockSpec((1, tq, 128), lambda b, h, i: (b, i, h)),
        out_shape=jax.ShapeDtypeStruct((bsz, n_rows, 512), BF16),
        compiler_params=_cparams(("parallel", "parallel", "arbitrary")),
    )(*([q] * n_qin), k, v, lam_arr, g_arr)


def _bwd_tile(i):
    return jnp.where(i == 0, 0, NT - i)


def _zero_at_row_start(*scratch):
    @pl.when(pl.program_id(1) == 0)
    def _():
        for s in scratch:
            s[...] = jnp.zeros_like(s)


def _hg_body(qf_ref, vf_ref, kf_ref, lf_ref, qb_ref, vb_ref, kb_ref, lb_ref,
             of_ref, ob_ref, sf_scr, sb_scr):
    q = (qf_ref[0].astype(F32), qb_ref[0].astype(F32))
    k = (kf_ref[0].astype(F32), kb_ref[0].astype(F32))
    v = (vf_ref[0].astype(F32), vb_ref[0].astype(F32))
    logf = (lf_ref[0], lb_ref[0])
    o_refs = (of_ref, ob_ref)
    dirs = (0, 1)

    r = lax.broadcasted_iota(jnp.int32, (TM, TM), 0)
    c = lax.broadcasted_iota(jnp.int32, (TM, TM), 1)
    same = (r // BLK) == (c // BLK)
    tri = [jnp.where(jnp.logical_and(same, (c >= r) if d else (c <= r)), 1.0, 0.0).astype(BF16) for d in dirs]
    blk = jnp.where(same, 1.0, 0.0).astype(BF16)
    bl = [_dot_01(tri[d], logf[d]) for d in dirs]
    tot = [_dot_01(blk, logf[d]) for d in dirs]
    qdb = [(q[d] * jnp.exp(bl[d])).astype(BF16) for d in dirs]
    kd = [(k[d] * jnp.exp(tot[d] - bl[d])).astype(BF16) for d in dirs]
    e_blk = [jnp.exp(tot[d]) for d in dirs]
    f = [jnp.exp(logf[d]) for d in dirs]

    def regroup(x):
        return jnp.swapaxes(x.reshape(8, HG_NV, HW), 0, 1).reshape(2, BLK, 8, HW)

    q4 = [regroup(q[d]) for d in dirs]
    k4 = [regroup(k[d]) for d in dirs]
    v4 = [regroup(v[d]) for d in dirs]
    f4 = [regroup(f[d]) for d in dirs]
    ones_bd = _ones_bd()
    e = [None, None]
    o4 = [None, None]
    for n in range(BLK):
        ln = BLK - n
        for d in dirs:
            qs, ks = (slice(0, ln), slice(n, BLK)) if d else (slice(n, BLK), slice(0, ln))
            if n == 0:
                pn = q4[d] * k4[d]
            else:
                fs = slice(n - 1, n - 1 + ln) if d else slice(1, 1 + ln)
                e[d] = f4[d][:, fs] if n == 1 else (e[d][:, :ln] if d else e[d][:, 1:]) * f4[d][:, fs]
                pn = q4[d][:, qs] * k4[d][:, ks] * e[d]
            a = _dot(pn.reshape(2 * ln * 8, HW).astype(BF16), ones_bd).reshape(2, ln, 8, HW) * v4[d][:, ks]
            if n == 0:
                o4[d] = a
            else:
                pad = jnp.zeros((2, n, 8, HW), F32)
                o4[d] = o4[d] + jnp.concatenate([a, pad] if d else [pad, a], axis=1)
    o_band = [jnp.swapaxes(o4[d].reshape(HG_NV, 8, HW), 0, 1).reshape(TM, HW) for d in dirs]

    lane = lax.broadcasted_iota(jnp.int32, (HD, 128), 1)
    own = [(lane >= HD) if h % 2 else (lane < HD) for h in range(NH)]
    tile = [slice(128 * (h // 2), 128 * (h // 2) + 128) for h in range(NH)]
    zero_t = jnp.zeros((HD, 128), BF16)
    vb = [v[d].astype(BF16) for d in dirs]
    nb = TM // BLK
    s_scr = (sf_scr, sb_scr)
    s = [[s_scr[d][h] for h in range(NH)] for d in dirs]
    for j in range(nb):
        for d in dirs:
            ib = nb - 1 - j if d else j
            rows = slice(ib * BLK, (ib + 1) * BLK)
            s_bd = jnp.concatenate(
                [jnp.concatenate([s[d][h].astype(BF16) if c == h // 2 else zero_t for c in range(HW // 128)], axis=1)
                 for h in range(NH)], axis=0)
            o_refs[d][0, rows, :] = (o_band[d][rows] + _dot_nt(qdb[d][rows], s_bd)).astype(BF16)
            u = _dot_tn(vb[d][rows], kd[d][rows])
            e_row = e_blk[d][ib * BLK:ib * BLK + 1, :]
            for h in range(NH):
                s[d][h] = s[d][h] * e_row[:, tile[h]] + jnp.where(own[h], u[h * HD:(h + 1) * HD, tile[h]], 0.0)
    for d in dirs:
        for h in range(NH):
            s_scr[d][h] = s[d][h]


def _bd4(x, bd):
    return jnp.where(bd, jnp.concatenate([x.astype(BF16)] * NH, axis=0), jnp.zeros((), BF16))


def _split3(x):
    h = x.astype(BF16)
    r1 = x - h.astype(F32)
    m = r1.astype(BF16)
    l = (r1 - m.astype(F32)).astype(BF16)
    return jnp.concatenate([h, m, l], axis=0)


def _dot_01(sel01, x):
    return _dot(jnp.concatenate([sel01] * 3, axis=1), _split3(x))


class _GdMasks:
    def __init__(self, reverse):
        r = lax.broadcasted_iota(jnp.int32, (CH, CH), 0)
        c = lax.broadcasted_iota(jnp.int32, (CH, CH), 1)
        self.tri = jnp.where((c >= r) if reverse else (c <= r), 1.0, 0.0).astype(BF16)
        t_i = lax.broadcasted_iota(jnp.int32, (CH, HW), 0)
        s_i = lax.broadcasted_iota(jnp.int32, (CH, HW), 1) % HD
        self.later = (t_i < s_i) if reverse else (t_i > s_i)
        self.valid = (s_i >= t_i) if reverse else (s_i <= t_i)
        self.strict = (s_i > t_i) if reverse else (s_i < t_i)
        self.eye = jnp.where(t_i == s_i, 1.0, 0.0)
        self.last = 0 if reverse else CH - 1
        self.off = {}
        m = 1
        while m < CH:
            t_blk, s_blk = t_i // m, s_i // m
            pair = (t_blk // 2) == (s_blk // 2)
            lo_hi = (t_blk % 2 == 0, s_blk % 2 == 1) if reverse else (t_blk % 2 == 1, s_blk % 2 == 0)
            self.off[m] = jnp.logical_and(pair, jnp.logical_and(*lo_hi))
            m *= 2


def _gd_prep(q, k, v, ab, dirs, masks, bd):
    grp = lax.broadcasted_iota(jnp.int32, (CH, HW), 1) // HD

    def widen(a, col0):
        out = jnp.zeros((CH, HW), F32)
        for h in range(NH):
            out = jnp.where(grp == h, jnp.broadcast_to(a[:, col0 + h:col0 + h + 1], (CH, HW)), out)
        return out

    ix = range(len(q))
    mk = [masks[d] for d in dirs]
    la = [widen(ab[j], dirs[j] * NH) for j in ix]
    beta = [widen(ab[j], 2 * NH + dirs[j] * NH) for j in ix]
    gc = [_dot_01(mk[j].tri, la[j]) for j in ix]
    diff = [_dot_01(mk[j].tri, jnp.where(mk[j].later, la[j], 0.0)) for j in ix]
    kb = [_bd4(k[j], bd) for j in ix]
    kq = [_dot_nt(jnp.concatenate([k[j].astype(BF16), q[j].astype(BF16)], axis=0), kb[j]) for j in ix]
    kk = [t[:CH] for t in kq]
    qk = [t[CH:] for t in kq]
    dm = [jnp.where(mk[j].valid, jnp.exp(jnp.minimum(diff[j], 0.0)), 0.0) for j in ix]
    n = [jnp.where(mk[j].strict, beta[j] * kk[j] * dm[j], 0.0) for j in ix]

    tinv = [mk[j].eye - jnp.where(mk[j].off[1], n[j], 0.0) for j in ix]
    m = 2
    while m < CH:
        y = [_dot(tinv[j].astype(BF16), _bd4(jnp.where(mk[j].off[m], n[j], 0.0), bd)) for j in ix]
        tinv = [tinv[j] - _dot(y[j].astype(BF16), _bd4(tinv[j], bd)) for j in ix]
        m *= 2
    tb = [t.astype(BF16) for t in tinv]
    eg = [jnp.exp(g) for g in gc]
    u = [_dot(tb[j], _bd4(v[j] * beta[j], bd)) for j in ix]
    w = [_dot(tb[j], _bd4(k[j] * beta[j] * eg[j], bd)) for j in ix]
    sc = [(qk[j] * dm[j]).astype(BF16) for j in ix]
    qeff = [(q[j] * eg[j] - _dot(sc[j], _bd4(w[j], bd))).astype(BF16) for j in ix]
    oc = [_dot(sc[j], _bd4(u[j], bd)) for j in ix]
    gl = [gc[j][mk[j].last:mk[j].last + 1, :] for j in ix]
    kd = [(k[j] * jnp.exp(gl[j] - gc[j])).astype(BF16) for j in ix]
    kwu = [_dot_tn(kd[j], jnp.concatenate([w[j].astype(BF16), u[j].astype(BF16)], axis=1)) for j in ix]
    kw = [jnp.where(bd, t[:, :HW], 0.0).astype(BF16) for t in kwu]
    ku = [jnp.where(bd, t[:, HW:], 0.0) for t in kwu]
    a = [jnp.exp(g) for g in gl]
    return qeff, oc, kw, ku, a


def _gd_body(qf_ref, kf_ref, vf_ref, abf_ref, qb_ref, kb_ref, vb_ref, abb_ref,
             of_ref, ob_ref, sf_scr, sb_scr):
    bd = _bd_mask()
    n_ch = TM // CH
    refs = ((qf_ref, kf_ref, vf_ref, abf_ref), (qb_ref, kb_ref, vb_ref, abb_ref))
    o_refs = (of_ref, ob_ref)
    items = [(d, (n_ch - 1 - j) if d else j) for j in range(n_ch) for d in range(2)]
    rows = [slice(ic * CH, (ic + 1) * CH) for _, ic in items]
    dirs = [d for d, _ in items]
    load = lambda which: [refs[d][which][0, r, :] for d, r in zip(dirs, rows)]
    qeff, oc, kw, ku, a = _gd_prep(load(0), load(1), load(2), load(3), dirs, (_GdMasks(False), _GdMasks(True)), bd)

    s = [sf_scr[...], sb_scr[...]]
    for j, (d, _) in enumerate(items):
        sb = s[d].astype(BF16)
        o_refs[d][0, rows[j], :] = (_dot(qeff[j], sb) + oc[j]).astype(BF16)
        s[d] = s[d] * a[j] - _dot(kw[j], sb) + ku[j]
    sf_scr[...] = s[0]
    sb_scr[...] = s[1]


def _scans_kernel(*refs):
    hg_in, gd_in = refs[0:8], refs[8:16]
    hg_out, gd_out = refs[16:18], refs[18:20]
    hg_scr, gd_scr = refs[20:22], refs[22:24]
    _zero_at_row_start(*hg_scr, *gd_scr)
    _hg_body(*hg_in, *hg_out, *hg_scr)
    _gd_body(*gd_in, *gd_out, *gd_scr)


def _scans_call(hq, hv, hk, hlf, gq, gk, gv, gab):
    bsz = hq.shape[0]
    f = lambda b, i: (b, i, 0)
    g0 = lambda b, i: (b, _bwd_tile(i), 0)
    g1 = lambda b, i: (b, _bwd_tile(i), 1)
    blk = lambda m: pl.BlockSpec((1, TM, HW), m)
    abs_ = lambda m: pl.BlockSpec((1, TM, 128), m)
    return pl.pallas_call(
        _scans_kernel,
        grid=(bsz, NT),
        in_specs=[blk(f), blk(f), blk(f), blk(f), blk(g0), blk(g0), blk(g1), blk(g1),
                  blk(f), blk(f), blk(f), abs_(f), blk(g0), blk(g0), blk(g0), abs_(g0)],
        out_specs=[blk(f), blk(g0), blk(f), blk(g0)],
        out_shape=[jax.ShapeDtypeStruct((bsz, T, HW), BF16)] * 4,
        scratch_shapes=[pltpu.VMEM((NH, HD, 128), F32)] * 2 + [pltpu.VMEM((HW, HW), F32)] * 2,
        compiler_params=_cparams(("parallel", "arbitrary")),
    )(hq, hv, hk, hlf, hq, hv, hk, hlf, gq, gk, gv, gab, gq, gk, gv, gab)


def _outproj_kernel(*refs, with_ctx, split):
    if split:
        ctx_ref, refs = refs[0], refs[1:]
    (x_ref, oal_ref, oac_ref, hof_ref, hob_ref, hsg_ref, gof_ref, gob_ref, gsg_ref,
     wo_ref, hng_ref, gng_ref, g1_ref, gate_ref, o_ref) = refs
    ones_bd = _ones_bd()
    oa = oal_ref[0]
    x = x_ref[0]
    if with_ctx:
        oa = jnp.where(pl.program_id(1) == 0, oac_ref[0], oa)
    if split:
        x = jnp.where(pl.program_id(1) == 0, ctx_ref[0], x)

    def finish(of_ref, ob_ref, sg_ref, ng_ref):
        o = of_ref[0].astype(F32) + ob_ref[0].astype(F32)
        ms = _head_sum(o * o, ones_bd) * (1.0 / HD)
        return (o * lax.rsqrt(ms + EPS) * ng_ref[...] * sg_ref[0]).astype(BF16)

    ob = finish(hof_ref, hob_ref, hsg_ref, hng_ref)
    oc = finish(gof_ref, gob_ref, gsg_ref, gng_ref)
    mix = (_dot(oa, wo_ref[0:512, :]) + _dot(ob, wo_ref[512:768, :]) + _dot(oc, wo_ref[768:1024, :]))
    ms = jnp.mean(mix * mix, axis=-1, keepdims=True)
    o_ref[0] = x + gate_ref[0, 0] * (mix * lax.rsqrt(ms + EPS) * g1_ref[...])


def _outproj_call(xs, oa_lat, oa_ctx, hof, hob, hsg, gof, gob, gsg, wo, hng, gng, g1, gate, with_ctx):
    split = isinstance(xs, tuple)
    assert with_ctx or not split
    bsz = oa_lat.shape[0]
    t0 = 0 if with_ctx else 1
    row = lambda wd: pl.BlockSpec((1, TM, wd), lambda b, i: (b, i + t0, 0))
    if split:
        x_specs = [pl.BlockSpec((1, TM, D), lambda b, i: (b, 0, 0)),
                   pl.BlockSpec((1, TM, D), lambda b, i: (b, jnp.maximum(i - 1, 0), 0))]
        x_args = list(xs)
    else:
        x_specs, x_args = [row(D)], [xs]
    oal = pl.BlockSpec((1, TM, 512), lambda b, i: (b, jnp.maximum(i + t0 - 1, 0), 0))
    oac = pl.BlockSpec((1, TM, 512), lambda b, i: (b, 0, 0))
    mod = pl.BlockSpec((1, 1, 1, D), lambda b, i: (b, _sel(i + t0, 1), 0, 0))
    return pl.pallas_call(
        functools.partial(_outproj_kernel, with_ctx=with_ctx, split=split),
        grid=(bsz, NT - t0),
        in_specs=x_specs + [oal, oac, row(HW), row(HW), row(HW), row(HW), row(HW), row(HW),
                            _const_spec((D, D)), _const_spec((1, HW)), _const_spec((1, HW)), _const_spec((1, D)),
                            mod],
        out_specs=pl.BlockSpec((1, TM, D), lambda b, i: (b, i, 0)),
        out_shape=jax.ShapeDtypeStruct((bsz, (NT - t0) * TM, D), F32),
        compiler_params=_cparams(("parallel", "arbitrary")),
    )(*x_args, oa_lat, oa_ctx, hof, hob, hsg, gof, gob, gsg, wo, hng, gng, g1, gate)


def _regroup_rows(x):
    return jnp.swapaxes(x.reshape(8, TM // 8, x.shape[1]), 0, 1).reshape(TM, x.shape[1])


def _ungroup_rows(x):
    return jnp.swapaxes(x.reshape(TM // 8, 8, x.shape[1]), 0, 1).reshape(TM, x.shape[1])


def _ffn_kernel(xp_ref, xm_ref, xn_ref, a_ref, s_ref, wup_ref, cw_ref, cb_ref, wdn_ref, g3_ref, gate_ref,
                o_ref, h_scr, act_scr, *, nct, nt):
    i = pl.program_id(1)
    a, s = a_ref[0, 0], s_ref[0, 0]
    ng = TM // 8
    xg = _regroup_rows(xm_ref[0])
    h_scr[0:TM, :] = _normed(xg, a, s).astype(BF16)
    hp = jnp.where(_prev_ok(i, nct), _normed(xp_ref[0, HALO - 8:HALO, :], a, s), 0.0)
    hn = jnp.where(_next_ok(i, nct, nt), _normed(xn_ref[0, 0:8, :], a, s), 0.0)
    h_scr[TM:TM + 16, :] = jnp.concatenate([hp, hn], axis=0).astype(BF16)
    h = h_scr[...]
    sub = lax.broadcasted_iota(jnp.int32, (8, FF_CW), 0)

    def conv(col0, lo):
        u = _dot(h, wup_ref[:, col0 + lo:col0 + lo + FF_CW])
        ur = u[0:TM].reshape(ng, 8, FF_CW)
        first = jnp.where(sub == 0, u[TM + 7:TM + 8], pltpu.roll(ur[ng - 1], 1, 0))
        last = jnp.where(sub == 7, u[TM + 8:TM + 9], pltpu.roll(ur[0], 7, 0))
        w = cw_ref[:, col0 + lo:col0 + lo + FF_CW]
        return (jnp.concatenate([first[None], ur[:-1]], axis=0) * w[0:1] + ur * w[1:2]
                + jnp.concatenate([ur[1:], last[None]], axis=0) * w[2:3] + cb_ref[:, col0 + lo:col0 + lo + FF_CW])

    for cidx in range(D_FF // FF_CW):
        lo = cidx * FF_CW
        act = _silu(conv(0, lo)) * conv(D_FF, lo)
        act_scr[:, lo:lo + FF_CW] = act.reshape(TM, FF_CW).astype(BF16)
    ff = _dot(act_scr[...], wdn_ref[...])
    ms = jnp.mean(ff * ff, axis=-1, keepdims=True)
    o_ref[0] = _ungroup_rows(xg + gate_ref[0, 0] * (ff * lax.rsqrt(ms + EPS) * g3_ref[...]))


def _ffn_call(xs, a_ff, s_ff, wup, cw, cb, wdn, g3, gate, nct):
    bsz, rows, _ = xs.shape
    nt = rows // TM
    mod = _mod_spec(nct)
    return pl.pallas_call(
        functools.partial(_ffn_kernel, nct=nct, nt=nt),
        grid=(bsz, nt),
        in_specs=_halo_specs(D, nct, nt) + [
            mod, mod, _const_spec((D, 2 * D_FF)), _const_spec((3, 2 * D_FF)), _const_spec((1, 2 * D_FF)),
            _const_spec((D_FF, D)), _const_spec((1, D)), mod],
        out_specs=pl.BlockSpec((1, TM, D), lambda b, i: (b, i, 0)),
        out_shape=jax.ShapeDtypeStruct((bsz, rows, D), F32),
        scratch_shapes=[pltpu.VMEM((TM + 16, D), BF16), pltpu.VMEM((TM, D_FF), BF16)],
        compiler_params=_cparams(("parallel", "arbitrary")),
    )(xs, xs, xs, a_ff, s_ff, wup, cw, cb, wdn, g3, gate)


def _rope_tables():
    n_freq = HD // 4
    inv = ROPE_THETA ** (-jnp.arange(n_freq, dtype=F32) / n_freq)
    rows = jnp.repeat(jnp.arange(SEQ // GRID_W, dtype=F32), GRID_W)
    cols = jnp.tile(jnp.arange(GRID_W, dtype=F32), SEQ // GRID_W)
    ang = jnp.concatenate([rows[:, None] * inv, cols[:, None] * inv], axis=-1)
    cos, sin = jnp.cos(ang), jnp.sin(ang)
    cos_l = jnp.tile(jnp.concatenate([cos, cos], axis=-1), (1, 2))
    sin_l = jnp.tile(jnp.concatenate([-sin, sin], axis=-1), (1, 2))
    cos_t = jnp.concatenate([jnp.ones((CTX, 128), F32), cos_l], axis=0)
    sin_t = jnp.concatenate([jnp.zeros((CTX, 128), F32), sin_l], axis=0)
    return cos_t, sin_t


def kernel(x, c, ctx, c_ctx, ada_w, ada_b, norm_g, w_in, w_out, da_lambda, da_subln_g, hg_lb_logits, hg_norm_g,
           gd_conv_w, gd_a_log, gd_dt_bias, gd_norm_g, ffn_w_up, ffn_conv_w, ffn_conv_b, ffn_w_down):
    bsz = x.shape[0]
    depth = ada_w.shape[0]
    assert x.shape == (bsz, SEQ, D) and ctx.shape == (bsz, CTX, D) and bsz <= 8
    cos_t, sin_t = _rope_tables()

    cond = jnp.concatenate([jax.nn.silu(c.astype(F32)), jnp.zeros((8 - bsz, D), F32),
                            jax.nn.silu(c_ctx.astype(F32))[None], jnp.zeros((7, D), F32)], axis=0)
    mods = _ada_call(cond, ada_w, ada_b).reshape(depth, 16, 6, D)

    mod = jnp.stack([jnp.broadcast_to(mods[:, 8:9], (depth, bsz, 6, D)), mods[:, :bsz]], axis=2)
    g = norm_g.astype(F32)[:, None, None]
    a_in, s_in = g[..., 0:1, :] * (1.0 + mod[..., 1:2, :]), mod[..., 0:1, :]
    a_ff, s_ff = g[..., 2:3, :] * (1.0 + mod[..., 4:5, :]), mod[..., 3:4, :]
    gate1, gate2 = mod[..., 2:3, :], mod[..., 5:6, :]
    g1, g3 = norm_g[:, 1:2].astype(F32), norm_g[:, 3:4].astype(F32)
    lb_w = jax.nn.softmax(hg_lb_logits.astype(F32), axis=0)
    lb = (jnp.cumsum(lb_w, axis=0) - lb_w[0]).reshape(depth, 1, 2 * HW)
    lb1m, lbm = 1.0 - lb, jnp.maximum(lb, LB_FLOOR)
    lane_pad = ((0, 0), (0, 0), (0, 128 - 2 * NH))
    nega = jnp.pad(-jnp.exp(gd_a_log.astype(F32)).reshape(depth, 1, 2 * NH), lane_pad)
    dtb = jnp.pad(gd_dt_bias.astype(F32).reshape(depth, 1, 2 * NH), lane_pad)
    lam_init = jnp.asarray([0.8 - 0.6 * math.exp(-0.3 * layer) for layer in range(depth)], F32)
    lp = da_lambda.astype(F32)
    lam = jnp.exp(jnp.sum(lp[:, 0] * lp[:, 1], axis=-1)) - jnp.exp(jnp.sum(lp[:, 2] * lp[:, 3], axis=-1)) + lam_init
    lam_arr = jnp.broadcast_to(lam[:, None, None], (depth, 8, 128))
    g_arr = (da_subln_g.astype(F32) * (1.0 - lam_init)[:, None]).reshape(depth, 1, DV)
    hng = jnp.tile(hg_norm_g.astype(F32), (1, NH)).reshape(depth, 1, HW)
    gng = jnp.tile(gd_norm_g.astype(F32), (1, NH)).reshape(depth, 1, HW)
    conv_g, conv_f = gd_conv_w.astype(F32), ffn_conv_w.astype(F32)
    conv_fb = ffn_conv_b.astype(F32).reshape(depth, 1, 2 * D_FF)

    xs = (ctx.astype(F32), x.astype(F32))
    for layer in range(depth):
        need_ctx = layer < depth - 1
        wl = w_in[layer]
        wab = jnp.pad(wl[:, C_GDA:C_GDG], ((0, 0), (0, 128 - 4 * NH)))
        w = [wl[:, C_DAQ:C_DAV], wl[:, C_DAV:C_HGQ], wl[:, C_HGQ:C_GDQKV], wl[:, C_GDQKV:C_GDA], wab,
             wl[:, C_GDG:IN_COLS]]
        w = [t.astype(BF16) for t in w]
        consts = [lb1m[layer], lbm[layer], conv_g[layer], nega[layer], dtb[layer]]
        (q, k, v, hq, hv, hk, hlf, hsg, gq, gk, gv, gab, gsg) = _inproj_call(
            xs, a_in[layer], s_in[layer], cos_t, sin_t, w, consts)

        oa_lat = _attn_call(q, k, v, lam_arr[layer], g_arr[layer], CTX, SEQ, T)
        oa_ctx = _attn_call(q, k, v, lam_arr[layer], g_arr[layer], 0, CTX, CTX) if need_ctx else oa_lat

        hof, hob, gof, gob = _scans_call(hq, hv, hk, hlf, gq, gk, gv, gab)

        xs = _outproj_call(xs, oa_lat, oa_ctx, hof, hob, hsg, gof, gob, gsg, w_out[layer].astype(BF16),
                           hng[layer], gng[layer], g1[layer], gate1[layer], need_ctx)
        xs = _ffn_call(xs, a_ff[layer], s_ff[layer], ffn_w_up[layer].astype(BF16), conv_f[layer], conv_fb[layer],
                       ffn_w_down[layer].astype(BF16), g3[layer], gate2[layer], 1 if need_ctx else 0)
    return xs
```

```python
import functools
import math

import jax
import jax.numpy as jnp
import numpy as np
from jax import lax
from jax.experimental import pallas as pl
from jax.experimental.pallas import tpu as pltpu

F32 = jnp.float32
BF16 = jnp.bfloat16

D = 1024
CTX = 256
SEQ = 2048
T = CTX + SEQ
GRID_W = 64
ROPE_THETA = 10000.0
EPS = 1e-6
LB_FLOOR = 1e-30
NH = 4
HD = 64
DV = 128
HW = NH * HD
D_FF = 2816
TM = 256
NT = T // TM
HALO = 16
TQ = 128
ATT_QB = 256
ATT_QIN = 4
Q_SCALE = HD ** -0.5 * math.log2(math.e)
BLK = 16
HG_NV = TM // 8
assert HG_NV == 2 * BLK
HG_SB = 2 * BLK
CH = 64
FF_CW = 256
VMEM_LIMIT = 56 * 1024 * 1024

C_DAQ, C_DAK, C_DAV = 0, 512, 1024
C_HGQ, C_HGI, C_HGF, C_HGG = 1536, 1792, 2048, 2560
C_GDQKV, C_GDA, C_GDB, C_GDG = 2816, 3584, 3592, 3600
IN_COLS = 3856


def _cparams(sem):
    return pltpu.CompilerParams(dimension_semantics=sem, vmem_limit_bytes=VMEM_LIMIT)


def _const_spec(shape):
    n = len(shape)
    return pl.BlockSpec(shape, lambda *_: (0,) * n)


def _sigmoid(x):
    return 1.0 / (1.0 + jnp.exp(-x))


def _silu(x):
    return x * _sigmoid(x)


def _softplus(x):
    return jnp.maximum(x, 0.0) + jnp.log(1.0 + jnp.exp(-jnp.abs(x)))


def _dot(a, b):
    return jnp.dot(a, b, preferred_element_type=F32)


def _dot_nt(a, b):
    return lax.dot_general(a, b, (((1,), (1,)), ((), ())), preferred_element_type=F32)


def _dot_tn(a, b):
    return lax.dot_general(a, b, (((0,), (0,)), ((), ())), preferred_element_type=F32)


def _dot_hi(a, b):
    return jnp.dot(a, b, preferred_element_type=F32, precision=lax.Precision.HIGHEST)


def _head_sum(x, ones_bd):
    return _dot(x.astype(BF16), ones_bd)


def _ones_bd():
    r = lax.broadcasted_iota(jnp.int32, (HW, HW), 0) // HD
    c = lax.broadcasted_iota(jnp.int32, (HW, HW), 1) // HD
    return jnp.where(r == c, 1.0, 0.0).astype(BF16)


def _bd_mask():
    r = lax.broadcasted_iota(jnp.int32, (HW, HW), 0) // HD
    c = lax.broadcasted_iota(jnp.int32, (HW, HW), 1) // HD
    return r == c


def _normed(x, a, s):
    ms = jnp.mean(x * x, axis=-1, keepdims=True)
    return (x * lax.rsqrt(ms + EPS)) * a + s


def _ada_kernel(c_ref, w_ref, b_ref, o_ref):
    o_ref[0] = _dot(c_ref[...].astype(BF16), w_ref[0].astype(BF16)) + b_ref[0]


def _ada_call(cond, ada_w, ada_b):
    depth = ada_w.shape[0]
    nc = 6 * D
    cw = 1536
    return pl.pallas_call(
        _ada_kernel,
        grid=(depth, nc // cw),
        in_specs=[pl.BlockSpec((16, D), lambda l, j: (0, 0)),
                  pl.BlockSpec((1, D, cw), lambda l, j: (l, 0, j)),
                  pl.BlockSpec((1, 1, cw), lambda l, j: (l, 0, j))],
        out_specs=pl.BlockSpec((1, 16, cw), lambda l, j: (l, 0, j)),
        out_shape=jax.ShapeDtypeStruct((depth, 16, nc), F32),
        compiler_params=_cparams(("arbitrary", "arbitrary")),
    )(cond, ada_w, ada_b.reshape(depth, 1, nc))


def _sel(i, nct):
    return jnp.where(i >= nct, 1, 0)


def _prev_ok(i, nct):
    return i > nct


def _next_ok(i, nct, nt):
    return jnp.logical_and(i >= nct, i <= nt - 2)


def _halo_specs(width, nct, nt):
    per = TM // HALO
    return [pl.BlockSpec((1, HALO, width), lambda b, i: (b, i * per - jnp.where(_prev_ok(i, nct), 1, 0), 0)),
            pl.BlockSpec((1, TM, width), lambda b, i: (b, i, 0)),
            pl.BlockSpec((1, HALO, width), lambda b, i: (b, (i + 1) * per - jnp.where(_next_ok(i, nct, nt), 0, 1), 0))]


def _mod_spec(nct):
    return pl.BlockSpec((1, 1, 1, D), lambda b, i: (b, _sel(i, nct), 0, 0))


def _fill_h(h_scr, xp, xm, xn, a, s, i, nct, nt):
    hp = _normed(xp, a, s)
    hn = _normed(xn, a, s)
    h_scr[0:HALO, :] = jnp.where(_prev_ok(i, nct), hp, 0.0).astype(BF16)
    h_scr[HALO:HALO + TM, :] = _normed(xm, a, s).astype(BF16)
    h_scr[HALO + TM:, :] = jnp.where(_next_ok(i, nct, nt), hn, 0.0).astype(BF16)


def _conv3(u_scr, w_ref):
    return (u_scr[HALO - 1:HALO - 1 + TM, :] * w_ref[0:1, :]
            + u_scr[HALO:HALO + TM, :] * w_ref[1:2, :]
            + u_scr[HALO + 1:HALO + 1 + TM, :] * w_ref[2:3, :])


def _inproj_kernel(*refs, split):
    if split:
        ctx_ref, refs = refs[0], refs[1:]
    (xp_ref, xm_ref, xn_ref, a_ref, s_ref, cos_ref, sin_ref,
     wqk_ref, wv_ref, whg_ref, wgq_ref, wab_ref, wgg_ref,
     lb1m_ref, lbm_ref, gconv_ref, nega_ref, dtb_ref,
     q_ref, k_ref, v_ref, hq_ref, hv_ref, hk_ref, hlf_ref, hsg_ref,
     gq_ref, gk_ref, gv_ref, gab_ref, gsg_ref,
     h_scr, u_scr) = refs
    i = pl.program_id(1)
    xm = jnp.where(i == 0, ctx_ref[0], xm_ref[0]) if split else xm_ref[0]
    _fill_h(h_scr, xp_ref[0], xm, xn_ref[0], a_ref[0, 0], s_ref[0, 0], i, 1, NT)
    h = h_scr[HALO:HALO + TM, :]

    z = _dot(h, wqk_ref[...])
    z_v = _dot(h, wv_ref[...])
    z_hg = _dot(h, whg_ref[...])
    u_scr[...] = _dot(h_scr[...], wgq_ref[...])
    z_ab = _dot(h, wab_ref[...])
    z_gg = _dot(h, wgg_ref[...])

    lane = lax.broadcasted_iota(jnp.int32, (TM, 128), 1)
    first_half = (lane % HD) < (HD // 2)
    cs, sn = cos_ref[...], sin_ref[...]
    for j in range(8):
        xj = z[:, j * 128:(j + 1) * 128]
        sw = jnp.where(first_half, pltpu.roll(xj, 128 - HD // 2, 1), pltpu.roll(xj, HD // 2, 1))
        r = xj * cs + sw * sn
        if j < 4:
            q_ref[0, :, j * 128:(j + 1) * 128] = (r * Q_SCALE).astype(BF16)
        else:
            k_ref[0, :, (j - 4) * 128:(j - 3) * 128] = r.astype(BF16)
    v_ref[0] = z_v.astype(BF16)

    z = z_hg
    hq_ref[0] = _silu(z[:, 0:HW]).astype(BF16)
    hv_ref[0] = z[:, HW:2 * HW].astype(BF16)
    sg = _sigmoid(z[:, 2 * HW:4 * HW])
    hk_ref[0] = (lb1m_ref[...] * (1.0 - sg)).astype(BF16)
    hlf_ref[0] = jnp.log(lbm_ref[...] + lb1m_ref[...] * sg)
    hsg_ref[0] = _silu(z[:, 4 * HW:5 * HW]).astype(BF16)

    y = _silu(_conv3(u_scr, gconv_ref))
    ones_bd = _ones_bd()
    qg, kg = y[:, 0:HW], y[:, HW:2 * HW]
    gq_ref[0] = (qg * lax.rsqrt(_head_sum(qg * qg, ones_bd) + EPS) * (HD ** -0.5)).astype(BF16)
    gk_ref[0] = (kg * lax.rsqrt(_head_sum(kg * kg, ones_bd) + EPS)).astype(BF16)
    gv_ref[0] = y[:, 2 * HW:3 * HW].astype(BF16)
    gab_ref[0] = jnp.where(lane < 2 * NH, nega_ref[...] * _softplus(z_ab + dtb_ref[...]), _sigmoid(z_ab))
    gsg_ref[0] = _silu(z_gg).astype(BF16)


def _split_x_specs():
    per = TM // HALO
    lat = lambda i: jnp.maximum(i - 1, 0)
    return [pl.BlockSpec((1, TM, D), lambda b, i: (b, 0, 0)),
            pl.BlockSpec((1, HALO, D), lambda b, i: (b, lat(i) * per - jnp.where(_prev_ok(i, 1), 1, 0), 0)),
            pl.BlockSpec((1, TM, D), lambda b, i: (b, lat(i), 0)),
            pl.BlockSpec((1, HALO, D), lambda b, i: (b, (lat(i) + 1) * per - jnp.where(_next_ok(i, 1, NT), 0, 1), 0))]


def _inproj_call(xs, a_in, s_in, cos_t, sin_t, w, consts):
    split = isinstance(xs, tuple)
    x_args = (xs[0], xs[1], xs[1], xs[1]) if split else (xs, xs, xs)
    bsz = x_args[0].shape[0]
    row = lambda wd: pl.BlockSpec((1, TM, wd), lambda b, i: (b, i, 0))
    tab = pl.BlockSpec((TM, 128), lambda b, i: (i, 0))
    out_w = [(512, BF16), (512, BF16), (512, BF16), (HW, BF16), (HW, BF16), (2 * HW, BF16), (2 * HW, F32),
             (HW, BF16), (HW, BF16), (HW, BF16), (HW, BF16), (128, F32), (HW, BF16)]
    return pl.pallas_call(
        functools.partial(_inproj_kernel, split=split),
        grid=(bsz, NT),
        in_specs=(_split_x_specs() if split else _halo_specs(D, 1, NT)) + [_mod_spec(1), _mod_spec(1), tab, tab]
        + [_const_spec(x.shape) for x in w] + [_const_spec(x.shape) for x in consts],
        out_specs=[row(wd) for wd, _ in out_w],
        out_shape=[jax.ShapeDtypeStruct((bsz, T, wd), dt) for wd, dt in out_w],
        scratch_shapes=[pltpu.VMEM((TM + 2 * HALO, D), BF16), pltpu.VMEM((TM + 2 * HALO, 3 * HW), F32)],
        compiler_params=_cparams(("parallel", "arbitrary")),
    )(*x_args, a_in, s_in, cos_t, sin_t, *w, *consts)


def _attn_kernel(*refs, n_qin):
    q_refs = refs[:n_qin]
    k_ref, v_ref, lam_ref, g_ref, o_ref = refs[n_qin:]
    k, v = k_ref[0], v_ref[0]
    lane = lax.broadcasted_iota(jnp.int32, (TQ, 128), 1)
    zero = jnp.zeros((TQ, 128), BF16)
    per = ATT_QB // TQ
    ix = range(n_qin * per)
    q = [q_refs[j // per][0, (j % per) * TQ:(j % per + 1) * TQ, :] for j in ix]
    qs = [jnp.concatenate([jnp.where(lane < HD, q[j], zero), jnp.where(lane >= HD, q[j], zero)], axis=0) for j in ix]
    st = [_dot_nt(k, qs[j]) for j in ix]
    m = [jnp.max(st[j], axis=0, keepdims=True) for j in ix]
    p = [jnp.exp2(st[j] - m[j]) for j in ix]
    l = [jnp.sum(p[j], axis=0, keepdims=True) for j in ix]
    ot = [_dot_tn(v, p[j].astype(BF16)) * (1.0 / l[j]) for j in ix]
    for j in ix:
        d = ot[j][:, :TQ] - lam_ref[0:1, :] * ot[j][:, TQ:]
        ms = jnp.mean(d * d, axis=0, keepdims=True)
        dn = d * lax.rsqrt(ms + EPS)
        o_ref[0, j * TQ:(j + 1) * TQ, :] = (dn.T * g_ref[...]).astype(BF16)


def _attn_call(q, k, v, lam_arr, g_arr, q_row0, n_rows, tk):
    bsz = q.shape[0]
    n_qin = min(ATT_QIN, n_rows // ATT_QB)
    tq = n_qin * ATT_QB
    assert q_row0 % ATT_QB == 0 and n_rows % tq == 0
    qoff = q_row0 // ATT_QB

    def q_spec(j):
        return pl.BlockSpec((1, ATT_QB, 128), lambda b, h, i: (b, qoff + i * n_qin + j, h))

    return pl.pallas_call(
        functools.partial(_attn_kernel, n_qin=n_qin),
        grid=(bsz, NH, n_rows // tq),
        in_specs=[q_spec(j) for j in range(n_qin)]
        + [pl.BlockSpec((1, tk, 128), lambda b, h, i: (b, 0, h)),
           pl.BlockSpec((1, tk, 128), lambda b, h, i: (b, 0, h)),
           _const_spec((8, 128)), _const_spec((1, 128))],
        out_specs=pl.BlockSpec((1, tq, 128), lambda b, h, i: (b, i, h)),
        out_shape=jax.ShapeDtypeStruct((bsz, n_rows, 512), BF16),
        compiler_params=_cparams(("parallel", "parallel", "arbitrary")),
    )(*([q] * n_qin), k, v, lam_arr, g_arr)


def _bwd_tile(i):
    return jnp.where(i == 0, 0, NT - i)


def _zero_at_row_start(*scratch):
    @pl.when(pl.program_id(1) == 0)
    def _():
        for s in scratch:
            s[...] = jnp.zeros_like(s)


def _hg_body(qf_ref, vf_ref, kf_ref, lf_ref, qb_ref, vb_ref, kb_ref, lb_ref,
             of_ref, ob_ref, sf_scr, sb_scr):
    q = (qf_ref[0].astype(F32), qb_ref[0].astype(F32))
    k = (kf_ref[0].astype(F32), kb_ref[0].astype(F32))
    v = (vf_ref[0].astype(F32), vb_ref[0].astype(F32))
    logf = (lf_ref[0], lb_ref[0])
    o_refs = (of_ref, ob_ref)
    dirs = (0, 1)

    r = lax.broadcasted_iota(jnp.int32, (TM, TM), 0)
    c = lax.broadcasted_iota(jnp.int32, (TM, TM), 1)
    same = (r // BLK) == (c // BLK)
    tri = [jnp.where(jnp.logical_and(same, (c >= r) if d else (c <= r)), 1.0, 0.0).astype(BF16) for d in dirs]
    blk = jnp.where(same, 1.0, 0.0).astype(BF16)
    bl = [_dot_01(tri[d], logf[d]) for d in dirs]
    tot = [_dot_01(blk, logf[d]) for d in dirs]
    qd16 = [q[d] * jnp.exp(bl[d]) for d in dirs]
    kd16 = [k[d] * jnp.exp(tot[d] - bl[d]) for d in dirs]
    f = [jnp.exp(logf[d]) for d in dirs]
    pair_of = jnp.where((r // HG_SB) == (c // HG_SB), 1.0, 0.0).astype(BF16)
    tot2 = [_dot_01(pair_of, logf[d]) for d in dirs]
    half = (lax.broadcasted_iota(jnp.int32, (TM, HW), 0) // BLK) % 2
    second = [half == (0 if d else 1) for d in dirs]
    e_other = [jnp.exp(tot2[d] - tot[d]) for d in dirs]
    qdb = [(qd16[d] * jnp.where(second[d], e_other[d], 1.0)).astype(BF16) for d in dirs]
    kd = [(kd16[d] * jnp.where(second[d], 1.0, e_other[d])).astype(BF16) for d in dirs]
    e_blk = [jnp.exp(tot2[d]) for d in dirs]

    def regroup(x):
        return jnp.swapaxes(x.reshape(8, HG_NV, HW), 0, 1).reshape(2, BLK, 8, HW)

    q4 = [regroup(q[d]) for d in dirs]
    k4 = [regroup(k[d]) for d in dirs]
    v4 = [regroup(v[d]) for d in dirs]
    f4 = [regroup(f[d]) for d in dirs]
    ones_bd = _ones_bd()
    e = [None, None]
    o4 = [None, None]
    for n in range(BLK):
        ln = BLK - n
        for d in dirs:
            qs, ks = (slice(0, ln), slice(n, BLK)) if d else (slice(n, BLK), slice(0, ln))
            if n == 0:
                pn = q4[d] * k4[d]
            else:
                fs = slice(n - 1, n - 1 + ln) if d else slice(1, 1 + ln)
                e[d] = f4[d][:, fs] if n == 1 else (e[d][:, :ln] if d else e[d][:, 1:]) * f4[d][:, fs]
                pn = q4[d][:, qs] * k4[d][:, ks] * e[d]
            a = _dot(pn.reshape(2 * ln * 8, HW).astype(BF16), ones_bd).reshape(2, ln, 8, HW) * v4[d][:, ks]
            if n == 0:
                o4[d] = a
            else:
                pad = jnp.zeros((2, n, 8, HW), F32)
                o4[d] = o4[d] + jnp.concatenate([a, pad] if d else [pad, a], axis=1)
    o_band = [jnp.swapaxes(o4[d].reshape(HG_NV, 8, HW), 0, 1).reshape(TM, HW) for d in dirs]

    n_sb = TM // HG_SB
    nq = n_sb * BLK
    hr = lax.broadcasted_iota(jnp.int32, (NH * nq, HW), 0) // nq
    hl = lax.broadcasted_iota(jnp.int32, (NH * nq, HW), 1) // HD
    same_head = hr == hl
    pr = lax.broadcasted_iota(jnp.int32, (nq, NH * nq), 0) // BLK
    pc = (lax.broadcasted_iota(jnp.int32, (nq, NH * nq), 1) % nq) // BLK
    same_step = pr == pc
    zero_b = jnp.zeros((), BF16)
    for d in dirs:
        def blocks(x, which):
            return x.reshape(n_sb, 2, BLK, HW)[:, which if d == 0 else 1 - which].reshape(nq, HW)

        k1 = blocks(kd16[d], 0).astype(BF16)
        v1 = blocks(v[d], 0).astype(BF16)
        q2 = blocks(qd16[d], 1).astype(BF16)
        sc = _dot_nt(q2, jnp.where(same_head, jnp.concatenate([k1] * NH, axis=0), zero_b))
        sc = jnp.where(same_step, sc, 0.0).astype(BF16)
        o2 = _dot(sc, jnp.where(same_head, jnp.concatenate([v1] * NH, axis=0), zero_b)).reshape(n_sb, BLK, HW)
        ob4 = o_band[d].reshape(n_sb, 2, BLK, HW)
        parts = [ob4[:, 0], ob4[:, 1]]
        parts[1 if d == 0 else 0] = parts[1 if d == 0 else 0] + o2
        o_band[d] = jnp.stack(parts, axis=1).reshape(TM, HW)

    lane = lax.broadcasted_iota(jnp.int32, (HD, 128), 1)
    own = [(lane >= HD) if h % 2 else (lane < HD) for h in range(NH)]
    tile = [slice(128 * (h // 2), 128 * (h // 2) + 128) for h in range(NH)]
    zero_t = jnp.zeros((HD, 128), BF16)
    vb = [v[d].astype(BF16) for d in dirs]
    nb = n_sb
    s_scr = (sf_scr, sb_scr)
    s = [[s_scr[d][h] for h in range(NH)] for d in dirs]
    for j in range(nb):
        for d in dirs:
            ib = nb - 1 - j if d else j
            rows = slice(ib * HG_SB, (ib + 1) * HG_SB)
            s_bd = jnp.concatenate(
                [jnp.concatenate([s[d][h].astype(BF16) if c == h // 2 else zero_t for c in range(HW // 128)], axis=1)
                 for h in range(NH)], axis=0)
            o_refs[d][0, rows, :] = (o_band[d][rows] + _dot_nt(qdb[d][rows], s_bd)).astype(BF16)
            u = _dot_tn(vb[d][rows], kd[d][rows])
            e_row = e_blk[d][ib * HG_SB:ib * HG_SB + 1, :]
            for h in range(NH):
                s[d][h] = s[d][h] * e_row[:, tile[h]] + jnp.where(own[h], u[h * HD:(h + 1) * HD, tile[h]], 0.0)
    for d in dirs:
        for h in range(NH):
            s_scr[d][h] = s[d][h]


def _bd4(x, bd):
    return jnp.where(bd, jnp.concatenate([x.astype(BF16)] * NH, axis=0), jnp.zeros((), BF16))


def _split3(x):
    h = x.astype(BF16)
    r1 = x - h.astype(F32)
    m = r1.astype(BF16)
    l = (r1 - m.astype(F32)).astype(BF16)
    return jnp.concatenate([h, m, l], axis=0)


def _dot_01(sel01, x):
    return _dot(jnp.concatenate([sel01] * 3, axis=1), _split3(x))


class _GdMasks:
    def __init__(self, reverse):
        r = lax.broadcasted_iota(jnp.int32, (CH, CH), 0)
        c = lax.broadcasted_iota(jnp.int32, (CH, CH), 1)
        self.tri = jnp.where((c >= r) if reverse else (c <= r), 1.0, 0.0).astype(BF16)
        t_i = lax.broadcasted_iota(jnp.int32, (CH, HW), 0)
        s_i = lax.broadcasted_iota(jnp.int32, (CH, HW), 1) % HD
        self.later = (t_i < s_i) if reverse else (t_i > s_i)
        self.valid = (s_i >= t_i) if reverse else (s_i <= t_i)
        self.strict = (s_i > t_i) if reverse else (s_i < t_i)
        self.eye = jnp.where(t_i == s_i, 1.0, 0.0)
        self.last = 0 if reverse else CH - 1
        self.off = {}
        m = 1
        while m < CH:
            t_blk, s_blk = t_i // m, s_i // m
            pair = (t_blk // 2) == (s_blk // 2)
            lo_hi = (t_blk % 2 == 0, s_blk % 2 == 1) if reverse else (t_blk % 2 == 1, s_blk % 2 == 0)
            self.off[m] = jnp.logical_and(pair, jnp.logical_and(*lo_hi))
            m *= 2


def _gd_prep(q, k, v, ab, dirs, masks, bd):
    grp = lax.broadcasted_iota(jnp.int32, (CH, HW), 1) // HD

    def widen(a, col0):
        out = jnp.zeros((CH, HW), F32)
        for h in range(NH):
            out = jnp.where(grp == h, jnp.broadcast_to(a[:, col0 + h:col0 + h + 1], (CH, HW)), out)
        return out

    ix = range(len(q))
    mk = [masks[d] for d in dirs]
    la = [widen(ab[j], dirs[j] * NH) for j in ix]
    beta = [widen(ab[j], 2 * NH + dirs[j] * NH) for j in ix]
    gc = [_dot_01(mk[j].tri, la[j]) for j in ix]
    diff = [_dot_01(mk[j].tri, jnp.where(mk[j].later, la[j], 0.0)) for j in ix]
    kb = [_bd4(k[j], bd) for j in ix]
    kq = [_dot_nt(jnp.concatenate([k[j].astype(BF16), q[j].astype(BF16)], axis=0), kb[j]) for j in ix]
    kk = [t[:CH] for t in kq]
    qk = [t[CH:] for t in kq]
    dm = [jnp.where(mk[j].valid, jnp.exp(jnp.minimum(diff[j], 0.0)), 0.0) for j in ix]
    n = [jnp.where(mk[j].strict, beta[j] * kk[j] * dm[j], 0.0) for j in ix]

    tinv = [mk[j].eye - jnp.where(mk[j].off[1], n[j], 0.0) for j in ix]
    m = 2
    while m < CH:
        y = [_dot(tinv[j].astype(BF16), _bd4(jnp.where(mk[j].off[m], n[j], 0.0), bd)) for j in ix]
        tinv = [tinv[j] - _dot(y[j].astype(BF16), _bd4(tinv[j], bd)) for j in ix]
        m *= 2
    tb = [t.astype(BF16) for t in tinv]
    eg = [jnp.exp(g) for g in gc]
    u = [_dot(tb[j], _bd4(v[j] * beta[j], bd)) for j in ix]
    w = [_dot(tb[j], _bd4(k[j] * beta[j] * eg[j], bd)) for j in ix]
    sc = [(qk[j] * dm[j]).astype(BF16) for j in ix]
    qeff = [(q[j] * eg[j] - _dot(sc[j], _bd4(w[j], bd))).astype(BF16) for j in ix]
    oc = [_dot(sc[j], _bd4(u[j], bd)) for j in ix]
    gl = [gc[j][mk[j].last:mk[j].last + 1, :] for j in ix]
    kd = [(k[j] * jnp.exp(gl[j] - gc[j])).astype(BF16) for j in ix]
    kwu = [_dot_tn(kd[j], jnp.concatenate([w[j].astype(BF16), u[j].astype(BF16)], axis=1)) for j in ix]
    kw = [jnp.where(bd, t[:, :HW], 0.0).astype(BF16) for t in kwu]
    ku = [jnp.where(bd, t[:, HW:], 0.0) for t in kwu]
    a = [jnp.exp(g) for g in gl]
    return qeff, oc, kw, ku, a


def _gd_body(qf_ref, kf_ref, vf_ref, abf_ref, qb_ref, kb_ref, vb_ref, abb_ref,
             of_ref, ob_ref, sf_scr, sb_scr):
    bd = _bd_mask()
    n_ch = TM // CH
    refs = ((qf_ref, kf_ref, vf_ref, abf_ref), (qb_ref, kb_ref, vb_ref, abb_ref))
    o_refs = (of_ref, ob_ref)
    items = [(d, (n_ch - 1 - j) if d else j) for j in range(n_ch) for d in range(2)]
    rows = [slice(ic * CH, (ic + 1) * CH) for _, ic in items]
    dirs = [d for d, _ in items]
    load = lambda which: [refs[d][which][0, r, :] for d, r in zip(dirs, rows)]
    qeff, oc, kw, ku, a = _gd_prep(load(0), load(1), load(2), load(3), dirs, (_GdMasks(False), _GdMasks(True)), bd)

    s = [sf_scr[...], sb_scr[...]]
    for j, (d, _) in enumerate(items):
        sb = s[d].astype(BF16)
        o_refs[d][0, rows[j], :] = (_dot(qeff[j], sb) + oc[j]).astype(BF16)
        s[d] = s[d] * a[j] - _dot(kw[j], sb) + ku[j]
    sf_scr[...] = s[0]
    sb_scr[...] = s[1]


def _scans_kernel(*refs):
    hg_in, gd_in = refs[0:8], refs[8:16]
    hg_out, gd_out = refs[16:18], refs[18:20]
    hg_scr, gd_scr = refs[20:22], refs[22:24]
    _zero_at_row_start(*hg_scr, *gd_scr)
    _hg_body(*hg_in, *hg_out, *hg_scr)
    _gd_body(*gd_in, *gd_out, *gd_scr)


def _scans_call(hq, hv, hk, hlf, gq, gk, gv, gab):
    bsz = hq.shape[0]
    f = lambda b, i: (b, i, 0)
    g0 = lambda b, i: (b, _bwd_tile(i), 0)
    g1 = lambda b, i: (b, _bwd_tile(i), 1)
    blk = lambda m: pl.BlockSpec((1, TM, HW), m)
    abs_ = lambda m: pl.BlockSpec((1, TM, 128), m)
    return pl.pallas_call(
        _scans_kernel,
        grid=(bsz, NT),
        in_specs=[blk(f), blk(f), blk(f), blk(f), blk(g0), blk(g0), blk(g1), blk(g1),
                  blk(f), blk(f), blk(f), abs_(f), blk(g0), blk(g0), blk(g0), abs_(g0)],
        out_specs=[blk(f), blk(g0), blk(f), blk(g0)],
        out_shape=[jax.ShapeDtypeStruct((bsz, T, HW), BF16)] * 4,
        scratch_shapes=[pltpu.VMEM((NH, HD, 128), F32)] * 2 + [pltpu.VMEM((HW, HW), F32)] * 2,
        compiler_params=_cparams(("parallel", "arbitrary")),
    )(hq, hv, hk, hlf, hq, hv, hk, hlf, gq, gk, gv, gab, gq, gk, gv, gab)


def _outproj_kernel(*refs, with_ctx, split):
    if split:
        ctx_ref, refs = refs[0], refs[1:]
    (x_ref, oal_ref, oac_ref, hof_ref, hob_ref, hsg_ref, gof_ref, gob_ref, gsg_ref,
     wo_ref, hng_ref, gng_ref, g1_ref, gate_ref, o_ref) = refs
    ones_bd = _ones_bd()
    oa = oal_ref[0]
    x = x_ref[0]
    if with_ctx:
        oa = jnp.where(pl.program_id(1) == 0, oac_ref[0], oa)
    if split:
        x = jnp.where(pl.program_id(1) == 0, ctx_ref[0], x)

    def finish(of_ref, ob_ref, sg_ref, ng_ref):
        o = of_ref[0].astype(F32) + ob_ref[0].astype(F32)
        ms = _head_sum(o * o, ones_bd) * (1.0 / HD)
        return (o * lax.rsqrt(ms + EPS) * ng_ref[...] * sg_ref[0]).astype(BF16)

    ob = finish(hof_ref, hob_ref, hsg_ref, hng_ref)
    oc = finish(gof_ref, gob_ref, gsg_ref, gng_ref)
    mix = (_dot(oa, wo_ref[0:512, :]) + _dot(ob, wo_ref[512:768, :]) + _dot(oc, wo_ref[768:1024, :]))
    ms = jnp.mean(mix * mix, axis=-1, keepdims=True)
    o_ref[0] = x + gate_ref[0, 0] * (mix * lax.rsqrt(ms + EPS) * g1_ref[...])


def _outproj_call(xs, oa_lat, oa_ctx, hof, hob, hsg, gof, gob, gsg, wo, hng, gng, g1, gate, with_ctx):
    split = isinstance(xs, tuple)
    assert with_ctx or not split
    bsz = oa_lat.shape[0]
    t0 = 0 if with_ctx else 1
    row = lambda wd: pl.BlockSpec((1, TM, wd), lambda b, i: (b, i + t0, 0))
    if split:
        x_specs = [pl.BlockSpec((1, TM, D), lambda b, i: (b, 0, 0)),
                   pl.BlockSpec((1, TM, D), lambda b, i: (b, jnp.maximum(i - 1, 0), 0))]
        x_args = list(xs)
    else:
        x_specs, x_args = [row(D)], [xs]
    oal = pl.BlockSpec((1, TM, 512), lambda b, i: (b, jnp.maximum(i + t0 - 1, 0), 0))
    oac = pl.BlockSpec((1, TM, 512), lambda b, i: (b, 0, 0))
    mod = pl.BlockSpec((1, 1, 1, D), lambda b, i: (b, _sel(i + t0, 1), 0, 0))
    return pl.pallas_call(
        functools.partial(_outproj_kernel, with_ctx=with_ctx, split=split),
        grid=(bsz, NT - t0),
        in_specs=x_specs + [oal, oac, row(HW), row(HW), row(HW), row(HW), row(HW), row(HW),
                            _const_spec((D, D)), _const_spec((1, HW)), _const_spec((1, HW)), _const_spec((1, D)),
                            mod],
        out_specs=pl.BlockSpec((1, TM, D), lambda b, i: (b, i, 0)),
        out_shape=jax.ShapeDtypeStruct((bsz, (NT - t0) * TM, D), F32),
        compiler_params=_cparams(("parallel", "arbitrary")),
    )(*x_args, oa_lat, oa_ctx, hof, hob, hsg, gof, gob, gsg, wo, hng, gng, g1, gate)


def _regroup_rows(x):
    return jnp.swapaxes(x.reshape(8, TM // 8, x.shape[1]), 0, 1).reshape(TM, x.shape[1])


def _ungroup_rows(x):
    return jnp.swapaxes(x.reshape(TM // 8, 8, x.shape[1]), 0, 1).reshape(TM, x.shape[1])


def _ffn_kernel(xp_ref, xm_ref, xn_ref, a_ref, s_ref, wup_ref, cw_ref, cb_ref, wdn_ref, g3_ref, gate_ref,
                o_ref, h_scr, act_scr, *, nct, nt):
    i = pl.program_id(1)
    a, s = a_ref[0, 0], s_ref[0, 0]
    ng = TM // 8
    xg = _regroup_rows(xm_ref[0])
    h_scr[0:TM, :] = _normed(xg, a, s).astype(BF16)
    hp = jnp.where(_prev_ok(i, nct), _normed(xp_ref[0, HALO - 8:HALO, :], a, s), 0.0)
    hn = jnp.where(_next_ok(i, nct, nt), _normed(xn_ref[0, 0:8, :], a, s), 0.0)
    h_scr[TM:TM + 16, :] = jnp.concatenate([hp, hn], axis=0).astype(BF16)
    h = h_scr[...]
    sub = lax.broadcasted_iota(jnp.int32, (8, FF_CW), 0)

    def conv(col0, lo):
        u = _dot(h, wup_ref[:, col0 + lo:col0 + lo + FF_CW])
        ur = u[0:TM].reshape(ng, 8, FF_CW)
        first = jnp.where(sub == 0, u[TM + 7:TM + 8], pltpu.roll(ur[ng - 1], 1, 0))
        last = jnp.where(sub == 7, u[TM + 8:TM + 9], pltpu.roll(ur[0], 7, 0))
        w = cw_ref[:, col0 + lo:col0 + lo + FF_CW]
        return (jnp.concatenate([first[None], ur[:-1]], axis=0) * w[0:1] + ur * w[1:2]
                + jnp.concatenate([ur[1:], last[None]], axis=0) * w[2:3] + cb_ref[:, col0 + lo:col0 + lo + FF_CW])

    for cidx in range(D_FF // FF_CW):
        lo = cidx * FF_CW
        act = _silu(conv(0, lo)) * conv(D_FF, lo)
        act_scr[:, lo:lo + FF_CW] = act.reshape(TM, FF_CW).astype(BF16)
    ff = _dot(act_scr[...], wdn_ref[...])
    ms = jnp.mean(ff * ff, axis=-1, keepdims=True)
    o_ref[0] = _ungroup_rows(xg + gate_ref[0, 0] * (ff * lax.rsqrt(ms + EPS) * g3_ref[...]))


def _ffn_call(xs, a_ff, s_ff, wup, cw, cb, wdn, g3, gate, nct):
    bsz, rows, _ = xs.shape
    nt = rows // TM
    mod = _mod_spec(nct)
    return pl.pallas_call(
        functools.partial(_ffn_kernel, nct=nct, nt=nt),
        grid=(bsz, nt),
        in_specs=_halo_specs(D, nct, nt) + [
            mod, mod, _const_spec((D, 2 * D_FF)), _const_spec((3, 2 * D_FF)), _const_spec((1, 2 * D_FF)),
            _const_spec((D_FF, D)), _const_spec((1, D)), mod],
        out_specs=pl.BlockSpec((1, TM, D), lambda b, i: (b, i, 0)),
        out_shape=jax.ShapeDtypeStruct((bsz, rows, D), F32),
        scratch_shapes=[pltpu.VMEM((TM + 16, D), BF16), pltpu.VMEM((TM, D_FF), BF16)],
        compiler_params=_cparams(("parallel", "arbitrary")),
    )(xs, xs, xs, a_ff, s_ff, wup, cw, cb, wdn, g3, gate)


def _rope_tables():
    n_freq = HD // 4
    inv = ROPE_THETA ** (-jnp.arange(n_freq, dtype=F32) / n_freq)
    rows = jnp.repeat(jnp.arange(SEQ // GRID_W, dtype=F32), GRID_W)
    cols = jnp.tile(jnp.arange(GRID_W, dtype=F32), SEQ // GRID_W)
    ang = jnp.concatenate([rows[:, None] * inv, cols[:, None] * inv], axis=-1)
    cos, sin = jnp.cos(ang), jnp.sin(ang)
    cos_l = jnp.tile(jnp.concatenate([cos, cos], axis=-1), (1, 2))
    sin_l = jnp.tile(jnp.concatenate([-sin, sin], axis=-1), (1, 2))
    cos_t = jnp.concatenate([jnp.ones((CTX, 128), F32), cos_l], axis=0)
    sin_t = jnp.concatenate([jnp.zeros((CTX, 128), F32), sin_l], axis=0)
    return cos_t, sin_t


def kernel(x, c, ctx, c_ctx, ada_w, ada_b, norm_g, w_in, w_out, da_lambda, da_subln_g, hg_lb_logits, hg_norm_g,
           gd_conv_w, gd_a_log, gd_dt_bias, gd_norm_g, ffn_w_up, ffn_conv_w, ffn_conv_b, ffn_w_down):
    bsz = x.shape[0]
    depth = ada_w.shape[0]
    assert x.shape == (bsz, SEQ, D) and ctx.shape == (bsz, CTX, D) and bsz <= 8
    cos_t, sin_t = _rope_tables()

    cond = jnp.concatenate([jax.nn.silu(c.astype(F32)), jnp.zeros((8 - bsz, D), F32),
                            jax.nn.silu(c_ctx.astype(F32))[None], jnp.zeros((7, D), F32)], axis=0)
    mods = _ada_call(cond, ada_w, ada_b).reshape(depth, 16, 6, D)

    mod = jnp.stack([jnp.broadcast_to(mods[:, 8:9], (depth, bsz, 6, D)), mods[:, :bsz]], axis=2)
    g = norm_g.astype(F32)[:, None, None]
    a_in, s_in = g[..., 0:1, :] * (1.0 + mod[..., 1:2, :]), mod[..., 0:1, :]
    a_ff, s_ff = g[..., 2:3, :] * (1.0 + mod[..., 4:5, :]), mod[..., 3:4, :]
    gate1, gate2 = mod[..., 2:3, :], mod[..., 5:6, :]
    g1, g3 = norm_g[:, 1:2].astype(F32), norm_g[:, 3:4].astype(F32)
    lb_w = jax.nn.softmax(hg_lb_logits.astype(F32), axis=0)
    lb = (jnp.cumsum(lb_w, axis=0) - lb_w[0]).reshape(depth, 1, 2 * HW)
    lb1m, lbm = 1.0 - lb, jnp.maximum(lb, LB_FLOOR)
    lane_pad = ((0, 0), (0, 0), (0, 128 - 2 * NH))
    nega = jnp.pad(-jnp.exp(gd_a_log.astype(F32)).reshape(depth, 1, 2 * NH), lane_pad)
    dtb = jnp.pad(gd_dt_bias.astype(F32).reshape(depth, 1, 2 * NH), lane_pad)
    lam_init = jnp.asarray([0.8 - 0.6 * math.exp(-0.3 * layer) for layer in range(depth)], F32)
    lp = da_lambda.astype(F32)
    lam = jnp.exp(jnp.sum(lp[:, 0] * lp[:, 1], axis=-1)) - jnp.exp(jnp.sum(lp[:, 2] * lp[:, 3], axis=-1)) + lam_init
    lam_arr = jnp.broadcast_to(lam[:, None, None], (depth, 8, 128))
    g_arr = (da_subln_g.astype(F32) * (1.0 - lam_init)[:, None]).reshape(depth, 1, DV)
    hng = jnp.tile(hg_norm_g.astype(F32), (1, NH)).reshape(depth, 1, HW)
    gng = jnp.tile(gd_norm_g.astype(F32), (1, NH)).reshape(depth, 1, HW)
    conv_g, conv_f = gd_conv_w.astype(F32), ffn_conv_w.astype(F32)
    conv_fb = ffn_conv_b.astype(F32).reshape(depth, 1, 2 * D_FF)

    xs = (ctx.astype(F32), x.astype(F32))
    for layer in range(depth):
        need_ctx = layer < depth - 1
        wl = w_in[layer]
        wab = jnp.pad(wl[:, C_GDA:C_GDG], ((0, 0), (0, 128 - 4 * NH)))
        w = [wl[:, C_DAQ:C_DAV], wl[:, C_DAV:C_HGQ], wl[:, C_HGQ:C_GDQKV], wl[:, C_GDQKV:C_GDA], wab,
             wl[:, C_GDG:IN_COLS]]
        w = [t.astype(BF16) for t in w]
        consts = [lb1m[layer], lbm[layer], conv_g[layer], nega[layer], dtb[layer]]
        (q, k, v, hq, hv, hk, hlf, hsg, gq, gk, gv, gab, gsg) = _inproj_call(
            xs, a_in[layer], s_in[layer], cos_t, sin_t, w, consts)

        oa_lat = _attn_call(q, k, v, lam_arr[layer], g_arr[layer], CTX, SEQ, T)
        oa_ctx = _attn_call(q, k, v, lam_arr[layer], g_arr[layer], 0, CTX, CTX) if need_ctx else oa_lat

        hof, hob, gof, gob = _scans_call(hq, hv, hk, hlf, gq, gk, gv, gab)

        xs = _outproj_call(xs, oa_lat, oa_ctx, hof, hob, hsg, gof, gob, gsg, w_out[layer].astype(BF16),
                           hng[layer], gng[layer], g1[layer], gate1[layer], need_ctx)
        xs = _ffn_call(xs, a_ff[layer], s_ff[layer], ffn_w_up[layer].astype(BF16), conv_f[layer], conv_fb[layer],
                       ffn_w_down[layer].astype(BF16), g3[layer], gate2[layer], 1 if need_ctx else 0)
    return xs
```

```python
import functools
import math

import jax
import jax.numpy as jnp
import numpy as np
from jax import lax
from jax.experimental import pallas as pl
from jax.experimental.pallas import tpu as pltpu

F32 = jnp.float32
BF16 = jnp.bfloat16

D = 1024
CTX = 256
SEQ = 2048
T = CTX + SEQ
GRID_W = 64
ROPE_THETA = 10000.0
EPS = 1e-6
LB_FLOOR = 1e-30
NH = 4
HD = 64
DV = 128
HW = NH * HD
D_FF = 2816
TM = 256
NT = T // TM
HALO = 16
TQ = 128
ATT_QB = 256
ATT_QIN = 4
Q_SCALE = HD ** -0.5 * math.log2(math.e)
BLK = 16
HG_NV = TM // 8
assert HG_NV == 2 * BLK
HG_SB = 2 * BLK
CH = 64
FF_CW = 256
VMEM_LIMIT = 56 * 1024 * 1024

C_DAQ, C_DAK, C_DAV = 0, 512, 1024
C_HGQ, C_HGI, C_HGF, C_HGG = 1536, 1792, 2048, 2560
C_GDQKV, C_GDA, C_GDB, C_GDG = 2816, 3584, 3592, 3600
IN_COLS = 3856


def _cparams(sem):
    return pltpu.CompilerParams(dimension_semantics=sem, vmem_limit_bytes=VMEM_LIMIT)


def _const_spec(shape):
    n = len(shape)
    return pl.BlockSpec(shape, lambda *_: (0,) * n)


def _sigmoid(x):
    return 1.0 / (1.0 + jnp.exp(-x))


def _silu(x):
    return x * _sigmoid(x)


def _softplus(x):
    return jnp.maximum(x, 0.0) + jnp.log(1.0 + jnp.exp(-jnp.abs(x)))


def _dot(a, b):
    return jnp.dot(a, b, preferred_element_type=F32)


def _dot_nt(a, b):
    return lax.dot_general(a, b, (((1,), (1,)), ((), ())), preferred_element_type=F32)


def _dot_tn(a, b):
    return lax.dot_general(a, b, (((0,), (0,)), ((), ())), preferred_element_type=F32)


def _dot_hi(a, b):
    return jnp.dot(a, b, preferred_element_type=F32, precision=lax.Precision.HIGHEST)


def _head_sum(x, ones_bd):
    return _dot(x.astype(BF16), ones_bd)


def _ones_bd():
    r = lax.broadcasted_iota(jnp.int32, (HW, HW), 0) // HD
    c = lax.broadcasted_iota(jnp.int32, (HW, HW), 1) // HD
    return jnp.where(r == c, 1.0, 0.0).astype(BF16)


def _bd_mask():
    r = lax.broadcasted_iota(jnp.int32, (HW, HW), 0) // HD
    c = lax.broadcasted_iota(jnp.int32, (HW, HW), 1) // HD
    return r == c


def _normed(x, a, s):
    ms = jnp.mean(x * x, axis=-1, keepdims=True)
    return (x * lax.rsqrt(ms + EPS)) * a + s


def _ada_kernel(c_ref, w_ref, b_ref, o_ref):
    o_ref[0] = _dot(c_ref[...].astype(BF16), w_ref[0].astype(BF16)) + b_ref[0]


def _ada_call(cond, ada_w, ada_b):
    depth = ada_w.shape[0]
    nc = 6 * D
    cw = 1536
    return pl.pallas_call(
        _ada_kernel,
        grid=(depth, nc // cw),
        in_specs=[pl.BlockSpec((16, D), lambda l, j: (0, 0)),
                  pl.BlockSpec((1, D, cw), lambda l, j: (l, 0, j)),
                  pl.BlockSpec((1, 1, cw), lambda l, j: (l, 0, j))],
        out_specs=pl.BlockSpec((1, 16, cw), lambda l, j: (l, 0, j)),
        out_shape=jax.ShapeDtypeStruct((depth, 16, nc), F32),
        compiler_params=_cparams(("arbitrary", "arbitrary")),
    )(cond, ada_w, ada_b.reshape(depth, 1, nc))


def _sel(i, nct):
    return jnp.where(i >= nct, 1, 0)


def _prev_ok(i, nct):
    return i > nct


def _next_ok(i, nct, nt):
    return jnp.logical_and(i >= nct, i <= nt - 2)


def _halo_specs(width, nct, nt):
    per = TM // HALO
    return [pl.BlockSpec((1, HALO, width), lambda b, i: (b, i * per - jnp.where(_prev_ok(i, nct), 1, 0), 0)),
            pl.BlockSpec((1, TM, width), lambda b, i: (b, i, 0)),
            pl.BlockSpec((1, HALO, width), lambda b, i: (b, (i + 1) * per - jnp.where(_next_ok(i, nct, nt), 0, 1), 0))]


def _mod_spec(nct):
    return pl.BlockSpec((1, 1, 1, D), lambda b, i: (b, _sel(i, nct), 0, 0))


def _fill_h(h_scr, xp, xm, xn, a, s, i, nct, nt):
    hp = _normed(xp, a, s)
    hn = _normed(xn, a, s)
    h_scr[0:HALO, :] = jnp.where(_prev_ok(i, nct), hp, 0.0).astype(BF16)
    h_scr[HALO:HALO + TM, :] = _normed(xm, a, s).astype(BF16)
    h_scr[HALO + TM:, :] = jnp.where(_next_ok(i, nct, nt), hn, 0.0).astype(BF16)


def _conv3(u_scr, w_ref):
    return (u_scr[HALO - 1:HALO - 1 + TM, :] * w_ref[0:1, :]
            + u_scr[HALO:HALO + TM, :] * w_ref[1:2, :]
            + u_scr[HALO + 1:HALO + 1 + TM, :] * w_ref[2:3, :])


def _inproj_kernel(*refs, split):
    if split:
        ctx_ref, refs = refs[0], refs[1:]
    (xp_ref, xm_ref, xn_ref, a_ref, s_ref, cos_ref, sin_ref, win_ref,
     lb1m_ref, lbm_ref, gconv_ref, nega_ref, dtb_ref,
     q_ref, k_ref, v_ref, hq_ref, hv_ref, hk_ref, hlf_ref, hsg_ref,
     gq_ref, gk_ref, gv_ref, gab_ref, gsg_ref,
     h_scr, u_scr, wqk_ref, wv_ref, whg_ref, wgq_ref, wab_ref, wgg_ref) = refs
    i = pl.program_id(1)

    @pl.when(jnp.logical_and(pl.program_id(0) == 0, i == 0))
    def _():
        wqk_ref[...] = win_ref[0, :, C_DAQ:C_DAV].astype(BF16)
        wv_ref[...] = win_ref[0, :, C_DAV:C_HGQ].astype(BF16)
        whg_ref[...] = win_ref[0, :, C_HGQ:C_GDQKV].astype(BF16)
        wgq_ref[...] = win_ref[0, :, C_GDQKV:C_GDA].astype(BF16)
        ab_lane = lax.broadcasted_iota(jnp.int32, (D, 128), 1)
        wab_ref[...] = jnp.where(ab_lane < 4 * NH, win_ref[0, :, C_GDA:C_GDA + 128], 0.0).astype(BF16)
        wgg_ref[...] = win_ref[0, :, C_GDG:IN_COLS].astype(BF16)

    xm = jnp.where(i == 0, ctx_ref[0], xm_ref[0]) if split else xm_ref[0]
    _fill_h(h_scr, xp_ref[0], xm, xn_ref[0], a_ref[0, 0], s_ref[0, 0], i, 1, NT)
    h = h_scr[HALO:HALO + TM, :]

    z = _dot(h, wqk_ref[...])
    z_v = _dot(h, wv_ref[...])
    z_hg = _dot(h, whg_ref[...])
    u_scr[...] = _dot(h_scr[...], wgq_ref[...])
    z_ab = _dot(h, wab_ref[...])
    z_gg = _dot(h, wgg_ref[...])

    lane = lax.broadcasted_iota(jnp.int32, (TM, 128), 1)
    first_half = (lane % HD) < (HD // 2)
    cs, sn = cos_ref[...], sin_ref[...]
    for j in range(8):
        xj = z[:, j * 128:(j + 1) * 128]
        sw = jnp.where(first_half, pltpu.roll(xj, 128 - HD // 2, 1), pltpu.roll(xj, HD // 2, 1))
        r = xj * cs + sw * sn
        if j < 4:
            q_ref[0, :, j * 128:(j + 1) * 128] = (r * Q_SCALE).astype(BF16)
        else:
            k_ref[0, :, (j - 4) * 128:(j - 3) * 128] = r.astype(BF16)
    v_ref[0] = z_v.astype(BF16)

    z = z_hg
    hq_ref[0] = _silu(z[:, 0:HW]).astype(BF16)
    hv_ref[0] = z[:, HW:2 * HW].astype(BF16)
    sg = _sigmoid(z[:, 2 * HW:4 * HW])
    hk_ref[0] = (lb1m_ref[...] * (1.0 - sg)).astype(BF16)
    hlf_ref[0] = jnp.log(lbm_ref[...] + lb1m_ref[...] * sg)
    hsg_ref[0] = _silu(z[:, 4 * HW:5 * HW]).astype(BF16)

    y = _silu(_conv3(u_scr, gconv_ref))
    ones_bd = _ones_bd()
    qg, kg = y[:, 0:HW], y[:, HW:2 * HW]
    gq_ref[0] = (qg * lax.rsqrt(_head_sum(qg * qg, ones_bd) + EPS) * (HD ** -0.5)).astype(BF16)
    gk_ref[0] = (kg * lax.rsqrt(_head_sum(kg * kg, ones_bd) + EPS)).astype(BF16)
    gv_ref[0] = y[:, 2 * HW:3 * HW].astype(BF16)
    gab_ref[0] = jnp.where(lane < 2 * NH, nega_ref[...] * _softplus(z_ab + dtb_ref[...]), _sigmoid(z_ab))
    gsg_ref[0] = _silu(z_gg).astype(BF16)


def _split_x_specs():
    per = TM // HALO
    lat = lambda i: jnp.maximum(i - 1, 0)
    return [pl.BlockSpec((1, TM, D), lambda b, i: (b, 0, 0)),
            pl.BlockSpec((1, HALO, D), lambda b, i: (b, lat(i) * per - jnp.where(_prev_ok(i, 1), 1, 0), 0)),
            pl.BlockSpec((1, TM, D), lambda b, i: (b, lat(i), 0)),
            pl.BlockSpec((1, HALO, D), lambda b, i: (b, (lat(i) + 1) * per - jnp.where(_next_ok(i, 1, NT), 0, 1), 0))]


def _inproj_call(xs, a_in, s_in, cos_t, sin_t, w_in, layer, consts):
    split = isinstance(xs, tuple)
    w_spec = pl.BlockSpec((1, D, IN_COLS), lambda b, i: (layer, 0, 0))
    w_groups = [C_DAV - C_DAQ, C_HGQ - C_DAV, C_GDQKV - C_HGQ, C_GDA - C_GDQKV, 128, IN_COLS - C_GDG]
    x_args = (xs[0], xs[1], xs[1], xs[1]) if split else (xs, xs, xs)
    bsz = x_args[0].shape[0]
    row = lambda wd: pl.BlockSpec((1, TM, wd), lambda b, i: (b, i, 0))
    tab = pl.BlockSpec((TM, 128), lambda b, i: (i, 0))
    out_w = [(512, BF16), (512, BF16), (512, BF16), (HW, BF16), (HW, BF16), (2 * HW, BF16), (2 * HW, F32),
             (HW, BF16), (HW, BF16), (HW, BF16), (HW, BF16), (128, F32), (HW, BF16)]
    return pl.pallas_call(
        functools.partial(_inproj_kernel, split=split),
        grid=(bsz, NT),
        in_specs=(_split_x_specs() if split else _halo_specs(D, 1, NT)) + [_mod_spec(1), _mod_spec(1), tab, tab]
        + [w_spec] + [_const_spec(x.shape) for x in consts],
        out_specs=[row(wd) for wd, _ in out_w],
        out_shape=[jax.ShapeDtypeStruct((bsz, T, wd), dt) for wd, dt in out_w],
        scratch_shapes=[pltpu.VMEM((TM + 2 * HALO, D), BF16), pltpu.VMEM((TM + 2 * HALO, 3 * HW), F32)]
        + [pltpu.VMEM((D, wd), BF16) for wd in w_groups],
        compiler_params=_cparams(("arbitrary", "arbitrary")),
    )(*x_args, a_in, s_in, cos_t, sin_t, w_in, *consts)


def _attn_kernel(*refs, n_qin):
    q_refs = refs[:n_qin]
    k_ref, v_ref, lam_ref, g_ref, o_ref = refs[n_qin:]
    k, v = k_ref[0], v_ref[0]
    lane = lax.broadcasted_iota(jnp.int32, (TQ, 128), 1)
    zero = jnp.zeros((TQ, 128), BF16)
    per = ATT_QB // TQ
    ix = range(n_qin * per)
    q = [q_refs[j // per][0, (j % per) * TQ:(j % per + 1) * TQ, :] for j in ix]
    qs = [jnp.concatenate([jnp.where(lane < HD, q[j], zero), jnp.where(lane >= HD, q[j], zero)], axis=0) for j in ix]
    st = [_dot_nt(k, qs[j]) for j in ix]
    m = [jnp.max(st[j], axis=0, keepdims=True) for j in ix]
    p = [jnp.exp2(st[j] - m[j]) for j in ix]
    l = [jnp.sum(p[j], axis=0, keepdims=True) for j in ix]
    ot = [_dot_tn(v, p[j].astype(BF16)) * (1.0 / l[j]) for j in ix]
    for j in ix:
        d = ot[j][:, :TQ] - lam_ref[0:1, :] * ot[j][:, TQ:]
        ms = jnp.mean(d * d, axis=0, keepdims=True)
        dn = d * lax.rsqrt(ms + EPS)
        o_ref[0, j * TQ:(j + 1) * TQ, :] = (dn.T * g_ref[...]).astype(BF16)


def _attn_call(q, k, v, lam_arr, g_arr, q_row0, n_rows, tk):
    bsz = q.shape[0]
    n_qin = min(ATT_QIN, n_rows // ATT_QB)
    tq = n_qin * ATT_QB
    assert q_row0 % ATT_QB == 0 and n_rows % tq == 0
    qoff = q_row0 // ATT_QB

    def q_spec(j):
        return pl.BlockSpec((1, ATT_QB, 128), lambda b, h, i: (b, qoff + i * n_qin + j, h))

    return pl.pallas_call(
        functools.partial(_attn_kernel, n_qin=n_qin),
        grid=(bsz, NH, n_rows // tq),
        in_specs=[q_spec(j) for j in range(n_qin)]
        + [pl.BlockSpec((1, tk, 128), lambda b, h, i: (b, 0, h)),
           pl.BlockSpec((1, tk, 128), lambda b, h, i: (b, 0, h)),
           _const_spec((8, 128)), _const_spec((1, 128))],
        out_specs=pl.BlockSpec((1, tq, 128), lambda b, h, i: (b, i, h)),
        out_shape=jax.ShapeDtypeStruct((bsz, n_rows, 512), BF16),
        compiler_params=_cparams(("parallel", "parallel", "arbitrary")),
    )(*([q] * n_qin), k, v, lam_arr, g_arr)


def _bwd_tile(i):
    return jnp.where(i == 0, 0, NT - i)


def _zero_at_row_start(*scratch):
    @pl.when(pl.program_id(1) == 0)
    def _():
        for s in scratch:
            s[...] = jnp.zeros_like(s)


def _hg_body(qf_ref, vf_ref, kf_ref, lf_ref, qb_ref, vb_ref, kb_ref, lb_ref,
             of_ref, ob_ref, sf_scr, sb_scr):
    q = (qf_ref[0].astype(F32), qb_ref[0].astype(F32))
    k = (kf_ref[0].astype(F32), kb_ref[0].astype(F32))
    v = (vf_ref[0].astype(F32), vb_ref[0].astype(F32))
    logf = (lf_ref[0], lb_ref[0])
    o_refs = (of_ref, ob_ref)
    dirs = (0, 1)

    r = lax.broadcasted_iota(jnp.int32, (TM, TM), 0)
    c = lax.broadcasted_iota(jnp.int32, (TM, TM), 1)
    same = (r // BLK) == (c // BLK)
    tri = [jnp.where(jnp.logical_and(same, (c >= r) if d else (c <= r)), 1.0, 0.0).astype(BF16) for d in dirs]
    blk = jnp.where(same, 1.0, 0.0).astype(BF16)
    pair_of = jnp.where((r // HG_SB) == (c // HG_SB), 1.0, 0.0).astype(BF16)
    bl = [_dot_01(tri[d], logf[d]) for d in dirs]
    tot = [_dot_01(blk, logf[d]) for d in dirs]
    tot2 = [_dot_01(pair_of, logf[d]) for d in dirs]
    qd16 = [q[d] * jnp.exp(bl[d]) for d in dirs]
    kd16 = [k[d] * jnp.exp(tot[d] - bl[d]) for d in dirs]
    f = [jnp.exp(logf[d]) for d in dirs]
    half = (lax.broadcasted_iota(jnp.int32, (TM, HW), 0) // BLK) % 2
    second = [half == (0 if d else 1) for d in dirs]
    e_other = [jnp.exp(tot2[d] - tot[d]) for d in dirs]
    qdb = [(qd16[d] * jnp.where(second[d], e_other[d], 1.0)).astype(BF16) for d in dirs]
    kd = [(kd16[d] * jnp.where(second[d], 1.0, e_other[d])).astype(BF16) for d in dirs]
    e_blk = [jnp.exp(tot2[d]) for d in dirs]

    def regroup(x):
        return jnp.swapaxes(x.reshape(8, HG_NV, HW), 0, 1).reshape(2, BLK, 8, HW)

    q4 = [regroup(q[d]) for d in dirs]
    k4 = [regroup(k[d]) for d in dirs]
    v4 = [regroup(v[d]) for d in dirs]
    f4 = [regroup(f[d]) for d in dirs]
    ones_bd = _ones_bd()
    e = [None, None]
    o4 = [None, None]
    for n in range(BLK):
        ln = BLK - n
        for d in dirs:
            qs, ks = (slice(0, ln), slice(n, BLK)) if d else (slice(n, BLK), slice(0, ln))
            if n == 0:
                pn = q4[d] * k4[d]
            else:
                fs = slice(n - 1, n - 1 + ln) if d else slice(1, 1 + ln)
                e[d] = f4[d][:, fs] if n == 1 else (e[d][:, :ln] if d else e[d][:, 1:]) * f4[d][:, fs]
                pn = q4[d][:, qs] * k4[d][:, ks] * e[d]
            a = _dot(pn.reshape(2 * ln * 8, HW).astype(BF16), ones_bd).reshape(2, ln, 8, HW) * v4[d][:, ks]
            if n == 0:
                o4[d] = a
            else:
                pad = jnp.zeros((2, n, 8, HW), F32)
                o4[d] = o4[d] + jnp.concatenate([a, pad] if d else [pad, a], axis=1)
    o_band = [jnp.swapaxes(o4[d].reshape(HG_NV, 8, HW), 0, 1).reshape(TM, HW) for d in dirs]

    n_sb = TM // HG_SB
    nq = n_sb * BLK
    hr = lax.broadcasted_iota(jnp.int32, (NH * nq, HW), 0) // nq
    hl = lax.broadcasted_iota(jnp.int32, (NH * nq, HW), 1) // HD
    same_head = hr == hl
    pr = lax.broadcasted_iota(jnp.int32, (nq, NH * nq), 0) // BLK
    pc = (lax.broadcasted_iota(jnp.int32, (nq, NH * nq), 1) % nq) // BLK
    same_step = pr == pc
    zero_b = jnp.zeros((), BF16)
    for d in dirs:
        def blocks(x, which):
            return x.reshape(n_sb, 2, BLK, HW)[:, which if d == 0 else 1 - which].reshape(nq, HW)

        k1 = blocks(kd16[d], 0).astype(BF16)
        v1 = blocks(v[d], 0).astype(BF16)
        q2 = blocks(qd16[d], 1).astype(BF16)
        sc = _dot_nt(q2, jnp.where(same_head, jnp.concatenate([k1] * NH, axis=0), zero_b))
        sc = jnp.where(same_step, sc, 0.0).astype(BF16)
        o2 = _dot(sc, jnp.where(same_head, jnp.concatenate([v1] * NH, axis=0), zero_b)).reshape(n_sb, BLK, HW)
        ob4 = o_band[d].reshape(n_sb, 2, BLK, HW)
        parts = [ob4[:, 0], ob4[:, 1]]
        parts[1 if d == 0 else 0] = parts[1 if d == 0 else 0] + o2
        o_band[d] = jnp.stack(parts, axis=1).reshape(TM, HW)

    lane = lax.broadcasted_iota(jnp.int32, (HD, 128), 1)
    own = [(lane >= HD) if h % 2 else (lane < HD) for h in range(NH)]
    tile = [slice(128 * (h // 2), 128 * (h // 2) + 128) for h in range(NH)]
    zero_t = jnp.zeros((HD, 128), BF16)
    vb = [v[d].astype(BF16) for d in dirs]
    nb = n_sb
    s_scr = (sf_scr, sb_scr)
    s = [[s_scr[d][h] for h in range(NH)] for d in dirs]
    for j in range(nb):
        for d in dirs:
            ib = nb - 1 - j if d else j
            rows = slice(ib * HG_SB, (ib + 1) * HG_SB)
            s_bd = jnp.concatenate(
                [jnp.concatenate([s[d][h].astype(BF16) if c == h // 2 else zero_t for c in range(HW // 128)], axis=1)
                 for h in range(NH)], axis=0)
            o_refs[d][0, rows, :] = (o_band[d][rows] + _dot_nt(qdb[d][rows], s_bd)).astype(BF16)
            u = _dot_tn(vb[d][rows], kd[d][rows])
            e_row = e_blk[d][ib * HG_SB:ib * HG_SB + 1, :]
            for h in range(NH):
                s[d][h] = s[d][h] * e_row[:, tile[h]] + jnp.where(own[h], u[h * HD:(h + 1) * HD, tile[h]], 0.0)
    for d in dirs:
        for h in range(NH):
            s_scr[d][h] = s[d][h]


def _bd4(x, bd):
    return jnp.where(bd, jnp.concatenate([x] * NH, axis=0), 0.0).astype(BF16)


def _split3(x):
    h = x.astype(BF16)
    r1 = x - h.astype(F32)
    m = r1.astype(BF16)
    l = (r1 - m.astype(F32)).astype(BF16)
    return jnp.concatenate([h, m, l], axis=0)


def _dot_01(sel01, x):
    return _dot(jnp.concatenate([sel01] * 3, axis=1), _split3(x))


class _GdMasks:
    def __init__(self, reverse):
        r = lax.broadcasted_iota(jnp.int32, (CH, CH), 0)
        c = lax.broadcasted_iota(jnp.int32, (CH, CH), 1)
        self.tri = jnp.where((c >= r) if reverse else (c <= r), 1.0, 0.0).astype(BF16)
        t_i = lax.broadcasted_iota(jnp.int32, (CH, HW), 0)
        s_i = lax.broadcasted_iota(jnp.int32, (CH, HW), 1) % HD
        self.later = (t_i < s_i) if reverse else (t_i > s_i)
        self.valid = (s_i >= t_i) if reverse else (s_i <= t_i)
        self.strict = (s_i > t_i) if reverse else (s_i < t_i)
        self.eye = jnp.where(t_i == s_i, 1.0, 0.0)
        self.last = 0 if reverse else CH - 1
        self.off = {}
        m = 1
        while m < CH:
            t_blk, s_blk = t_i // m, s_i // m
            pair = (t_blk // 2) == (s_blk // 2)
            lo_hi = (t_blk % 2 == 0, s_blk % 2 == 1) if reverse else (t_blk % 2 == 1, s_blk % 2 == 0)
            self.off[m] = jnp.logical_and(pair, jnp.logical_and(*lo_hi))
            m *= 2


def _gd_prep(q, k, v, ab, dirs, masks, bd):
    grp = lax.broadcasted_iota(jnp.int32, (CH, HW), 1) // HD

    def widen(a, col0):
        out = jnp.zeros((CH, HW), F32)
        for h in range(NH):
            out = jnp.where(grp == h, jnp.broadcast_to(a[:, col0 + h:col0 + h + 1], (CH, HW)), out)
        return out

    ix = range(len(q))
    mk = [masks[d] for d in dirs]
    la = [widen(ab[j], dirs[j] * NH) for j in ix]
    beta = [widen(ab[j], 2 * NH + dirs[j] * NH) for j in ix]
    gc = [_dot_01(mk[j].tri, la[j]) for j in ix]
    diff = [_dot_01(mk[j].tri, jnp.where(mk[j].later, la[j], 0.0)) for j in ix]
    kb = [_bd4(k[j], bd) for j in ix]
    kq = [_dot_nt(jnp.concatenate([k[j].astype(BF16), q[j].astype(BF16)], axis=0), kb[j]) for j in ix]
    kk = [t[:CH] for t in kq]
    qk = [t[CH:] for t in kq]
    dm = [jnp.where(mk[j].valid, jnp.exp(jnp.minimum(diff[j], 0.0)), 0.0) for j in ix]
    n = [jnp.where(mk[j].strict, beta[j] * kk[j] * dm[j], 0.0) for j in ix]

    tinv = [mk[j].eye - jnp.where(mk[j].off[1], n[j], 0.0) for j in ix]
    m = 2
    while m < CH:
        y = [_dot(tinv[j].astype(BF16), _bd4(jnp.where(mk[j].off[m], n[j], 0.0), bd)) for j in ix]
        tinv = [tinv[j] - _dot(y[j].astype(BF16), _bd4(tinv[j], bd)) for j in ix]
        m *= 2
    tb = [t.astype(BF16) for t in tinv]
    eg = [jnp.exp(g) for g in gc]
    u = [_dot(tb[j], _bd4(v[j] * beta[j], bd)) for j in ix]
    w = [_dot(tb[j], _bd4(k[j] * beta[j] * eg[j], bd)) for j in ix]
    sc = [(qk[j] * dm[j]).astype(BF16) for j in ix]
    qeff = [(q[j] * eg[j] - _dot(sc[j], _bd4(w[j], bd))).astype(BF16) for j in ix]
    oc = [_dot(sc[j], _bd4(u[j], bd)) for j in ix]
    gl = [gc[j][mk[j].last:mk[j].last + 1, :] for j in ix]
    kd = [(k[j] * jnp.exp(gl[j] - gc[j])).astype(BF16) for j in ix]
    kwu = [_dot_tn(kd[j], jnp.concatenate([w[j].astype(BF16), u[j].astype(BF16)], axis=1)) for j in ix]
    kw = [t[:, :HW].astype(BF16) for t in kwu]
    ku = [t[:, HW:] for t in kwu]
    a = [jnp.exp(g) for g in gl]
    return qeff, oc, kw, ku, a


def _gd_body(qf_ref, kf_ref, vf_ref, abf_ref, qb_ref, kb_ref, vb_ref, abb_ref,
             of_ref, ob_ref, sf_scr, sb_scr):
    bd = _bd_mask()
    n_ch = TM // CH
    refs = ((qf_ref, kf_ref, vf_ref, abf_ref), (qb_ref, kb_ref, vb_ref, abb_ref))
    o_refs = (of_ref, ob_ref)
    items = [(d, (n_ch - 1 - j) if d else j) for j in range(n_ch) for d in range(2)]
    rows = [slice(ic * CH, (ic + 1) * CH) for _, ic in items]
    dirs = [d for d, _ in items]
    load = lambda which: [refs[d][which][0, r, :] for d, r in zip(dirs, rows)]
    qeff, oc, kw, ku, a = _gd_prep(load(0), load(1), load(2), load(3), dirs, (_GdMasks(False), _GdMasks(True)), bd)

    s = [sf_scr[...], sb_scr[...]]
    for j, (d, _) in enumerate(items):
        sb = s[d].astype(BF16)
        o_refs[d][0, rows[j], :] = (_dot(qeff[j], sb) + oc[j]).astype(BF16)
        s[d] = jnp.where(bd, s[d] * a[j] - _dot(kw[j], sb) + ku[j], 0.0)
    sf_scr[...] = s[0]
    sb_scr[...] = s[1]


def _scans_kernel(*refs):
    hg_in, gd_in = refs[0:8], refs[8:16]
    hg_out, gd_out = refs[16:18], refs[18:20]
    hg_scr, gd_scr = refs[20:22], refs[22:24]
    _zero_at_row_start(*hg_scr, *gd_scr)
    _hg_body(*hg_in, *hg_out, *hg_scr)
    _gd_body(*gd_in, *gd_out, *gd_scr)


def _scans_call(hq, hv, hk, hlf, gq, gk, gv, gab):
    bsz = hq.shape[0]
    f = lambda b, i: (b, i, 0)
    g0 = lambda b, i: (b, _bwd_tile(i), 0)
    g1 = lambda b, i: (b, _bwd_tile(i), 1)
    blk = lambda m: pl.BlockSpec((1, TM, HW), m)
    abs_ = lambda m: pl.BlockSpec((1, TM, 128), m)
    return pl.pallas_call(
        _scans_kernel,
        grid=(bsz, NT),
        in_specs=[blk(f), blk(f), blk(f), blk(f), blk(g0), blk(g0), blk(g1), blk(g1),
                  blk(f), blk(f), blk(f), abs_(f), blk(g0), blk(g0), blk(g0), abs_(g0)],
        out_specs=[blk(f), blk(g0), blk(f), blk(g0)],
        out_shape=[jax.ShapeDtypeStruct((bsz, T, HW), BF16)] * 4,
        scratch_shapes=[pltpu.VMEM((NH, HD, 128), F32)] * 2 + [pltpu.VMEM((HW, HW), F32)] * 2,
        compiler_params=_cparams(("parallel", "arbitrary")),
    )(hq, hv, hk, hlf, hq, hv, hk, hlf, gq, gk, gv, gab, gq, gk, gv, gab)


def _outproj_kernel(*refs, with_ctx, split):
    if split:
        ctx_ref, refs = refs[0], refs[1:]
    (x_ref, oal_ref, oac_ref, hof_ref, hob_ref, hsg_ref, gof_ref, gob_ref, gsg_ref,
     wo_ref, hng_ref, gng_ref, g1_ref, gate_ref, o_ref) = refs
    ones_bd = _ones_bd()
    oa = oal_ref[0]
    x = x_ref[0]
    if with_ctx:
        oa = jnp.where(pl.program_id(1) == 0, oac_ref[0], oa)
    if split:
        x = jnp.where(pl.program_id(1) == 0, ctx_ref[0], x)

    def finish(of_ref, ob_ref, sg_ref, ng_ref):
        o = of_ref[0].astype(F32) + ob_ref[0].astype(F32)
        ms = _head_sum(o * o, ones_bd) * (1.0 / HD)
        return (o * lax.rsqrt(ms + EPS) * ng_ref[...] * sg_ref[0]).astype(BF16)

    ob = finish(hof_ref, hob_ref, hsg_ref, hng_ref)
    oc = finish(gof_ref, gob_ref, gsg_ref, gng_ref)
    mix = _dot(jnp.concatenate([oa, ob, oc], axis=1), wo_ref[...])
    ms = jnp.mean(mix * mix, axis=-1, keepdims=True)
    o_ref[0] = x + gate_ref[0, 0] * (mix * lax.rsqrt(ms + EPS) * g1_ref[...])


def _outproj_call(xs, oa_lat, oa_ctx, hof, hob, hsg, gof, gob, gsg, wo, hng, gng, g1, gate, with_ctx):
    split = isinstance(xs, tuple)
    assert with_ctx or not split
    bsz = oa_lat.shape[0]
    t0 = 0 if with_ctx else 1
    row = lambda wd: pl.BlockSpec((1, TM, wd), lambda b, i: (b, i + t0, 0))
    if split:
        x_specs = [pl.BlockSpec((1, TM, D), lambda b, i: (b, 0, 0)),
                   pl.BlockSpec((1, TM, D), lambda b, i: (b, jnp.maximum(i - 1, 0), 0))]
        x_args = list(xs)
    else:
        x_specs, x_args = [row(D)], [xs]
    oal = pl.BlockSpec((1, TM, 512), lambda b, i: (b, jnp.maximum(i + t0 - 1, 0), 0))
    oac = pl.BlockSpec((1, TM, 512), lambda b, i: (b, 0, 0))
    mod = pl.BlockSpec((1, 1, 1, D), lambda b, i: (b, _sel(i + t0, 1), 0, 0))
    return pl.pallas_call(
        functools.partial(_outproj_kernel, with_ctx=with_ctx, split=split),
        grid=(bsz, NT - t0),
        in_specs=x_specs + [oal, oac, row(HW), row(HW), row(HW), row(HW), row(HW), row(HW),
                            _const_spec((D, D)), _const_spec((1, HW)), _const_spec((1, HW)), _const_spec((1, D)),
                            mod],
        out_specs=pl.BlockSpec((1, TM, D), lambda b, i: (b, i, 0)),
        out_shape=jax.ShapeDtypeStruct((bsz, (NT - t0) * TM, D), F32),
        compiler_params=_cparams(("parallel", "arbitrary")),
    )(*x_args, oa_lat, oa_ctx, hof, hob, hsg, gof, gob, gsg, wo, hng, gng, g1, gate)


def _regroup_rows(x):
    return jnp.swapaxes(x.reshape(8, TM // 8, x.shape[1]), 0, 1).reshape(TM, x.shape[1])


def _ungroup_rows(x):
    return jnp.swapaxes(x.reshape(TM // 8, 8, x.shape[1]), 0, 1).reshape(TM, x.shape[1])


def _ffn_kernel(xp_ref, xm_ref, xn_ref, a_ref, s_ref, wup_ref, cw_ref, cb_ref, wdn_ref, g3_ref, gate_ref,
                o_ref, h_scr, act_scr, *, nct, nt):
    i = pl.program_id(1)
    a, s = a_ref[0, 0], s_ref[0, 0]
    ng = TM // 8
    xg = _regroup_rows(xm_ref[0])
    h_scr[0:TM, :] = _normed(xg, a, s).astype(BF16)
    hp = jnp.where(_prev_ok(i, nct), _normed(xp_ref[0, HALO - 8:HALO, :], a, s), 0.0)
    hn = jnp.where(_next_ok(i, nct, nt), _normed(xn_ref[0, 0:8, :], a, s), 0.0)
    h_scr[TM:TM + 16, :] = jnp.concatenate([hp, hn], axis=0).astype(BF16)
    h = h_scr[...]
    sub = lax.broadcasted_iota(jnp.int32, (8, FF_CW), 0)

    def conv(col0, lo):
        u = _dot(h, wup_ref[:, col0 + lo:col0 + lo + FF_CW])
        ur = u[0:TM].reshape(ng, 8, FF_CW)
        first = jnp.where(sub == 0, u[TM + 7:TM + 8], pltpu.roll(ur[ng - 1], 1, 0))
        last = jnp.where(sub == 7, u[TM + 8:TM + 9], pltpu.roll(ur[0], 7, 0))
        w = cw_ref[:, col0 + lo:col0 + lo + FF_CW]
        return (jnp.concatenate([first[None], ur[:-1]], axis=0) * w[0:1] + ur * w[1:2]
                + jnp.concatenate([ur[1:], last[None]], axis=0) * w[2:3] + cb_ref[:, col0 + lo:col0 + lo + FF_CW])

    for cidx in range(D_FF // FF_CW):
        lo = cidx * FF_CW
        act = _silu(conv(0, lo)) * conv(D_FF, lo)
        act_scr[:, lo:lo + FF_CW] = act.reshape(TM, FF_CW).astype(BF16)
    ff = _dot(act_scr[...], wdn_ref[...])
    ms = jnp.mean(ff * ff, axis=-1, keepdims=True)
    o_ref[0] = _ungroup_rows(xg + gate_ref[0, 0] * (ff * lax.rsqrt(ms + EPS) * g3_ref[...]))


def _ffn_call(xs, a_ff, s_ff, wup, cw, cb, wdn, g3, gate, nct):
    bsz, rows, _ = xs.shape
    nt = rows // TM
    mod = _mod_spec(nct)
    return pl.pallas_call(
        functools.partial(_ffn_kernel, nct=nct, nt=nt),
        grid=(bsz, nt),
        in_specs=_halo_specs(D, nct, nt) + [
            mod, mod, _const_spec((D, 2 * D_FF)), _const_spec((3, 2 * D_FF)), _const_spec((1, 2 * D_FF)),
            _const_spec((D_FF, D)), _const_spec((1, D)), mod],
        out_specs=pl.BlockSpec((1, TM, D), lambda b, i: (b, i, 0)),
        out_shape=jax.ShapeDtypeStruct((bsz, rows, D), F32),
        scratch_shapes=[pltpu.VMEM((TM + 16, D), BF16), pltpu.VMEM((TM, D_FF), BF16)],
        compiler_params=_cparams(("parallel", "arbitrary")),
    )(xs, xs, xs, a_ff, s_ff, wup, cw, cb, wdn, g3, gate)


def _rope_tables():
    n_freq = HD // 4
    inv = ROPE_THETA ** (-jnp.arange(n_freq, dtype=F32) / n_freq)
    rows = jnp.repeat(jnp.arange(SEQ // GRID_W, dtype=F32), GRID_W)
    cols = jnp.tile(jnp.arange(GRID_W, dtype=F32), SEQ // GRID_W)
    ang = jnp.concatenate([rows[:, None] * inv, cols[:, None] * inv], axis=-1)
    cos, sin = jnp.cos(ang), jnp.sin(ang)
    cos_l = jnp.tile(jnp.concatenate([cos, cos], axis=-1), (1, 2))
    sin_l = jnp.tile(jnp.concatenate([-sin, sin], axis=-1), (1, 2))
    cos_t = jnp.concatenate([jnp.ones((CTX, 128), F32), cos_l], axis=0)
    sin_t = jnp.concatenate([jnp.zeros((CTX, 128), F32), sin_l], axis=0)
    return cos_t, sin_t


def kernel(x, c, ctx, c_ctx, ada_w, ada_b, norm_g, w_in, w_out, da_lambda, da_subln_g, hg_lb_logits, hg_norm_g,
           gd_conv_w, gd_a_log, gd_dt_bias, gd_norm_g, ffn_w_up, ffn_conv_w, ffn_conv_b, ffn_w_down):
    bsz = x.shape[0]
    depth = ada_w.shape[0]
    assert x.shape == (bsz, SEQ, D) and ctx.shape == (bsz, CTX, D) and bsz <= 8
    cos_t, sin_t = _rope_tables()

    cond = jnp.concatenate([jax.nn.silu(c.astype(F32)), jnp.zeros((8 - bsz, D), F32),
                            jax.nn.silu(c_ctx.astype(F32))[None], jnp.zeros((7, D), F32)], axis=0)
    mods = _ada_call(cond, ada_w, ada_b).reshape(depth, 16, 6, D)

    mod = jnp.stack([jnp.broadcast_to(mods[:, 8:9], (depth, bsz, 6, D)), mods[:, :bsz]], axis=2)
    g = norm_g.astype(F32)[:, None, None]
    a_in, s_in = g[..., 0:1, :] * (1.0 + mod[..., 1:2, :]), mod[..., 0:1, :]
    a_ff, s_ff = g[..., 2:3, :] * (1.0 + mod[..., 4:5, :]), mod[..., 3:4, :]
    gate1, gate2 = mod[..., 2:3, :], mod[..., 5:6, :]
    g1, g3 = norm_g[:, 1:2].astype(F32), norm_g[:, 3:4].astype(F32)
    lb_w = jax.nn.softmax(hg_lb_logits.astype(F32), axis=0)
    lb = (jnp.cumsum(lb_w, axis=0) - lb_w[0]).reshape(depth, 1, 2 * HW)
    lb1m, lbm = 1.0 - lb, jnp.maximum(lb, LB_FLOOR)
    lane_pad = ((0, 0), (0, 0), (0, 128 - 2 * NH))
    nega = jnp.pad(-jnp.exp(gd_a_log.astype(F32)).reshape(depth, 1, 2 * NH), lane_pad)
    dtb = jnp.pad(gd_dt_bias.astype(F32).reshape(depth, 1, 2 * NH), lane_pad)
    lam_init = jnp.asarray([0.8 - 0.6 * math.exp(-0.3 * layer) for layer in range(depth)], F32)
    lp = da_lambda.astype(F32)
    lam = jnp.exp(jnp.sum(lp[:, 0] * lp[:, 1], axis=-1)) - jnp.exp(jnp.sum(lp[:, 2] * lp[:, 3], axis=-1)) + lam_init
    lam_arr = jnp.broadcast_to(lam[:, None, None], (depth, 8, 128))
    g_arr = (da_subln_g.astype(F32) * (1.0 - lam_init)[:, None]).reshape(depth, 1, DV)
    hng = jnp.tile(hg_norm_g.astype(F32), (1, NH)).reshape(depth, 1, HW)
    gng = jnp.tile(gd_norm_g.astype(F32), (1, NH)).reshape(depth, 1, HW)
    conv_g, conv_f = gd_conv_w.astype(F32), ffn_conv_w.astype(F32)
    conv_fb = ffn_conv_b.astype(F32).reshape(depth, 1, 2 * D_FF)

    xs = (ctx.astype(F32), x.astype(F32))
    for layer in range(depth):
        need_ctx = layer < depth - 1
        consts = [lb1m[layer], lbm[layer], conv_g[layer], nega[layer], dtb[layer]]
        (q, k, v, hq, hv, hk, hlf, hsg, gq, gk, gv, gab, gsg) = _inproj_call(
            xs, a_in[layer], s_in[layer], cos_t, sin_t, w_in.astype(F32), layer, consts)

        oa_lat = _attn_call(q, k, v, lam_arr[layer], g_arr[layer], CTX, SEQ, T)
        oa_ctx = _attn_call(q, k, v, lam_arr[layer], g_arr[layer], 0, CTX, CTX) if need_ctx else oa_lat

        hof, hob, gof, gob = _scans_call(hq, hv, hk, hlf, gq, gk, gv, gab)

        xs = _outproj_call(xs, oa_lat, oa_ctx, hof, hob, hsg, gof, gob, gsg, w_out[layer].astype(BF16),
                           hng[layer], gng[layer], g1[layer], gate1[layer], need_ctx)
        xs = _ffn_call(xs, a_ff[layer], s_ff[layer], ffn_w_up[layer].astype(BF16), conv_f[layer], conv_fb[layer],
                       ffn_w_down[layer].astype(BF16), g3[layer], gate2[layer], 1 if need_ctx else 0)
    return xs
```

```python
import functools
import math

import jax
import jax.numpy as jnp
import numpy as np
from jax import lax
from jax.experimental import pallas as pl
from jax.experimental.pallas import tpu as pltpu

F32 = jnp.float32
BF16 = jnp.bfloat16

D = 1024
CTX = 256
SEQ = 2048
T = CTX + SEQ
GRID_W = 64
ROPE_THETA = 10000.0
EPS = 1e-6
LB_FLOOR = 1e-30
NH = 4
HD = 64
DV = 128
HW = NH * HD
D_FF = 2816
TM = 256
NT = T // TM
HALO = 16
TQ = 128
ATT_QB = 256
ATT_QIN = 4
Q_SCALE = HD ** -0.5 * math.log2(math.e)
BLK = 16
HG_NV = TM // 8
assert HG_NV == 2 * BLK
HG_SB = 2 * BLK
CH = 64
FF_CW = 256
VMEM_LIMIT = 56 * 1024 * 1024

C_DAQ, C_DAK, C_DAV = 0, 512, 1024
C_HGQ, C_HGI, C_HGF, C_HGG = 1536, 1792, 2048, 2560
C_GDQKV, C_GDA, C_GDB, C_GDG = 2816, 3584, 3592, 3600
IN_COLS = 3856


def _cparams(sem):
    return pltpu.CompilerParams(dimension_semantics=sem, vmem_limit_bytes=VMEM_LIMIT)


def _const_spec(shape):
    n = len(shape)
    return pl.BlockSpec(shape, lambda *_: (0,) * n)


def _layer_spec(stacked, layer):
    n = stacked.ndim - 1
    return pl.BlockSpec((1,) + stacked.shape[1:], lambda *_: (layer,) + (0,) * n)


def _sigmoid(x):
    return 1.0 / (1.0 + jnp.exp(-x))


def _silu(x):
    return x * _sigmoid(x)


def _softplus(x):
    return jnp.maximum(x, 0.0) + jnp.log(1.0 + jnp.exp(-jnp.abs(x)))


def _dot(a, b):
    return jnp.dot(a, b, preferred_element_type=F32)


def _dot_nt(a, b):
    return lax.dot_general(a, b, (((1,), (1,)), ((), ())), preferred_element_type=F32)


def _dot_tn(a, b):
    return lax.dot_general(a, b, (((0,), (0,)), ((), ())), preferred_element_type=F32)


def _dot_hi(a, b):
    return jnp.dot(a, b, preferred_element_type=F32, precision=lax.Precision.HIGHEST)


def _head_sum(x, ones_bd):
    return _dot(x.astype(BF16), ones_bd)


def _ones_bd():
    r = lax.broadcasted_iota(jnp.int32, (HW, HW), 0) // HD
    c = lax.broadcasted_iota(jnp.int32, (HW, HW), 1) // HD
    return jnp.where(r == c, 1.0, 0.0).astype(BF16)


def _bd_mask():
    r = lax.broadcasted_iota(jnp.int32, (HW, HW), 0) // HD
    c = lax.broadcasted_iota(jnp.int32, (HW, HW), 1) // HD
    return r == c


def _normed(x, a, s):
    ms = jnp.mean(x * x, axis=-1, keepdims=True)
    return (x * lax.rsqrt(ms + EPS)) * a + s


def _ada_kernel(c_ref, w_ref, b_ref, o_ref):
    o_ref[0] = _dot(c_ref[...].astype(BF16), w_ref[0].astype(BF16)) + b_ref[0]


def _ada_call(cond, ada_w, ada_b):
    depth = ada_w.shape[0]
    nc = 6 * D
    cw = 1536
    return pl.pallas_call(
        _ada_kernel,
        grid=(depth, nc // cw),
        in_specs=[pl.BlockSpec((16, D), lambda l, j: (0, 0)),
                  pl.BlockSpec((1, D, cw), lambda l, j: (l, 0, j)),
                  pl.BlockSpec((1, 1, cw), lambda l, j: (l, 0, j))],
        out_specs=pl.BlockSpec((1, 16, cw), lambda l, j: (l, 0, j)),
        out_shape=jax.ShapeDtypeStruct((depth, 16, nc), F32),
        compiler_params=_cparams(("arbitrary", "arbitrary")),
    )(cond, ada_w, ada_b.reshape(depth, 1, nc))


def _sel(i, nct):
    return jnp.where(i >= nct, 1, 0)


def _prev_ok(i, nct):
    return i > nct


def _next_ok(i, nct, nt):
    return jnp.logical_and(i >= nct, i <= nt - 2)


def _halo_specs(width, nct, nt):
    per = TM // HALO
    return [pl.BlockSpec((1, HALO, width), lambda b, i: (b, i * per - jnp.where(_prev_ok(i, nct), 1, 0), 0)),
            pl.BlockSpec((1, TM, width), lambda b, i: (b, i, 0)),
            pl.BlockSpec((1, HALO, width), lambda b, i: (b, (i + 1) * per - jnp.where(_next_ok(i, nct, nt), 0, 1), 0))]


def _mod_spec(nct):
    return pl.BlockSpec((1, 1, 1, D), lambda b, i: (b, _sel(i, nct), 0, 0))


def _fill_h(h_scr, xp, xm, xn, a, s, i, nct, nt):
    hp = _normed(xp, a, s)
    hn = _normed(xn, a, s)
    h_scr[0:HALO, :] = jnp.where(_prev_ok(i, nct), hp, 0.0).astype(BF16)
    h_scr[HALO:HALO + TM, :] = _normed(xm, a, s).astype(BF16)
    h_scr[HALO + TM:, :] = jnp.where(_next_ok(i, nct, nt), hn, 0.0).astype(BF16)


def _conv3(u_scr, w_ref):
    return (u_scr[HALO - 1:HALO - 1 + TM, :] * w_ref[0:1, :]
            + u_scr[HALO:HALO + TM, :] * w_ref[1:2, :]
            + u_scr[HALO + 1:HALO + 1 + TM, :] * w_ref[2:3, :])


def _inproj_kernel(*refs, split):
    if split:
        ctx_ref, refs = refs[0], refs[1:]
    (xp_ref, xm_ref, xn_ref, a_ref, s_ref, cos_ref, sin_ref, win_ref,
     lb1m_ref, lbm_ref, gconv_ref, nega_ref, dtb_ref,
     q_ref, k_ref, v_ref, hq_ref, hv_ref, hk_ref, hlf_ref, hsg_ref,
     gq_ref, gk_ref, gv_ref, gab_ref, gsg_ref,
     h_scr, u_scr, wqk_ref, wv_ref, whg_ref, wgq_ref, wab_ref, wgg_ref) = refs
    i = pl.program_id(1)

    @pl.when(jnp.logical_and(pl.program_id(0) == 0, i == 0))
    def _():
        wqk_ref[...] = win_ref[0, C_DAQ:C_DAV, :].T.astype(BF16)
        wv_ref[...] = win_ref[0, C_DAV:C_HGQ, :].T.astype(BF16)
        whg_ref[...] = win_ref[0, C_HGQ:C_GDQKV, :].T.astype(BF16)
        wgq_ref[...] = win_ref[0, C_GDQKV:C_GDA, :].T.astype(BF16)
        ab_lane = lax.broadcasted_iota(jnp.int32, (D, 128), 1)
        wab_ref[...] = jnp.where(ab_lane < 4 * NH, win_ref[0, C_GDA:C_GDA + 128, :].T, 0.0).astype(BF16)
        wgg_ref[...] = win_ref[0, C_GDG:IN_COLS, :].T.astype(BF16)

    xm = jnp.where(i == 0, ctx_ref[0], xm_ref[0]) if split else xm_ref[0]
    _fill_h(h_scr, xp_ref[0], xm, xn_ref[0], a_ref[0, 0], s_ref[0, 0], i, 1, NT)
    h = h_scr[HALO:HALO + TM, :]

    z = _dot(h, wqk_ref[...])
    z_v = _dot(h, wv_ref[...])
    z_hg = _dot(h, whg_ref[...])
    u_scr[...] = _dot(h_scr[...], wgq_ref[...])
    z_ab = _dot(h, wab_ref[...])
    z_gg = _dot(h, wgg_ref[...])

    lane = lax.broadcasted_iota(jnp.int32, (TM, 128), 1)
    first_half = (lane % HD) < (HD // 2)
    cs, sn = cos_ref[...], sin_ref[...]
    for j in range(8):
        xj = z[:, j * 128:(j + 1) * 128]
        sw = jnp.where(first_half, pltpu.roll(xj, 128 - HD // 2, 1), pltpu.roll(xj, HD // 2, 1))
        r = xj * cs + sw * sn
        if j < 4:
            q_ref[0, :, j * 128:(j + 1) * 128] = (r * Q_SCALE).astype(BF16)
        else:
            k_ref[0, :, (j - 4) * 128:(j - 3) * 128] = r.astype(BF16)
    v_ref[0] = z_v.astype(BF16)

    z = z_hg
    hq_ref[0] = _silu(z[:, 0:HW]).astype(BF16)
    hv_ref[0] = z[:, HW:2 * HW].astype(BF16)
    sg = _sigmoid(z[:, 2 * HW:4 * HW])
    hk_ref[0] = (lb1m_ref[...] * (1.0 - sg)).astype(BF16)
    hlf_ref[0] = jnp.log(lbm_ref[...] + lb1m_ref[...] * sg)
    hsg_ref[0] = _silu(z[:, 4 * HW:5 * HW]).astype(BF16)

    y = _silu(_conv3(u_scr, gconv_ref))
    ones_bd = _ones_bd()
    qg, kg = y[:, 0:HW], y[:, HW:2 * HW]
    gq_ref[0] = (qg * lax.rsqrt(_head_sum(qg * qg, ones_bd) + EPS) * (HD ** -0.5)).astype(BF16)
    gk_ref[0] = (kg * lax.rsqrt(_head_sum(kg * kg, ones_bd) + EPS)).astype(BF16)
    gv_ref[0] = y[:, 2 * HW:3 * HW].astype(BF16)
    gab_ref[0] = jnp.where(lane < 2 * NH, nega_ref[...] * _softplus(z_ab + dtb_ref[...]), _sigmoid(z_ab))
    gsg_ref[0] = _silu(z_gg).astype(BF16)


def _split_x_specs():
    per = TM // HALO
    lat = lambda i: jnp.maximum(i - 1, 0)
    return [pl.BlockSpec((1, TM, D), lambda b, i: (b, 0, 0)),
            pl.BlockSpec((1, HALO, D), lambda b, i: (b, lat(i) * per - jnp.where(_prev_ok(i, 1), 1, 0), 0)),
            pl.BlockSpec((1, TM, D), lambda b, i: (b, lat(i), 0)),
            pl.BlockSpec((1, HALO, D), lambda b, i: (b, (lat(i) + 1) * per - jnp.where(_next_ok(i, 1, NT), 0, 1), 0))]


def _inproj_call(xs, a_in, s_in, cos_t, sin_t, w_in, layer, consts):
    split = isinstance(xs, tuple)
    w_spec = pl.BlockSpec((1, IN_COLS, D), lambda b, i: (layer, 0, 0))
    w_groups = [C_DAV - C_DAQ, C_HGQ - C_DAV, C_GDQKV - C_HGQ, C_GDA - C_GDQKV, 128, IN_COLS - C_GDG]
    x_args = (xs[0], xs[1], xs[1], xs[1]) if split else (xs, xs, xs)
    bsz = x_args[0].shape[0]
    row = lambda wd: pl.BlockSpec((1, TM, wd), lambda b, i: (b, i, 0))
    tab = pl.BlockSpec((TM, 128), lambda b, i: (i, 0))
    out_w = [(512, BF16), (512, BF16), (512, BF16), (HW, BF16), (HW, BF16), (2 * HW, BF16), (2 * HW, F32),
             (HW, BF16), (HW, BF16), (HW, BF16), (HW, BF16), (128, F32), (HW, BF16)]
    return pl.pallas_call(
        functools.partial(_inproj_kernel, split=split),
        grid=(bsz, NT),
        in_specs=(_split_x_specs() if split else _halo_specs(D, 1, NT)) + [_mod_spec(1), _mod_spec(1), tab, tab]
        + [w_spec] + [_const_spec(x.shape) for x in consts],
        out_specs=[row(wd) for wd, _ in out_w],
        out_shape=[jax.ShapeDtypeStruct((bsz, T, wd), dt) for wd, dt in out_w],
        scratch_shapes=[pltpu.VMEM((TM + 2 * HALO, D), BF16), pltpu.VMEM((TM + 2 * HALO, 3 * HW), F32)]
        + [pltpu.VMEM((D, wd), BF16) for wd in w_groups],
        compiler_params=_cparams(("arbitrary", "arbitrary")),
    )(*x_args, a_in, s_in, cos_t, sin_t, w_in, *consts)


def _attn_kernel(*refs, n_qin):
    q_refs = refs[:n_qin]
    k_ref, v_ref, lam_ref, g_ref, o_ref = refs[n_qin:]
    k, v = k_ref[0], v_ref[0]
    lane = lax.broadcasted_iota(jnp.int32, (TQ, 128), 1)
    zero = jnp.zeros((TQ, 128), BF16)
    per = ATT_QB // TQ
    ix = range(n_qin * per)
    q = [q_refs[j // per][0, (j % per) * TQ:(j % per + 1) * TQ, :] for j in ix]
    qs = [jnp.concatenate([jnp.where(lane < HD, q[j], zero), jnp.where(lane >= HD, q[j], zero)], axis=0) for j in ix]
    st = [_dot_nt(k, qs[j]) for j in ix]
    m = [jnp.max(st[j], axis=0, keepdims=True) for j in ix]
    p = [jnp.exp2(st[j] - m[j]) for j in ix]
    l = [jnp.sum(p[j], axis=0, keepdims=True) for j in ix]
    ot = [_dot_tn(v, p[j].astype(BF16)) * (1.0 / l[j]) for j in ix]
    for j in ix:
        d = ot[j][:, :TQ] - lam_ref[0:1, :] * ot[j][:, TQ:]
        ms = jnp.mean(d * d, axis=0, keepdims=True)
        dn = d * lax.rsqrt(ms + EPS)
        o_ref[0, j * TQ:(j + 1) * TQ, :] = (dn.T * g_ref[...]).astype(BF16)


def _attn_call(q, k, v, lam_arr, g_arr, q_row0, n_rows, tk):
    bsz = q.shape[0]
    n_qin = min(ATT_QIN, n_rows // ATT_QB)
    tq = n_qin * ATT_QB
    assert q_row0 % ATT_QB == 0 and n_rows % tq == 0
    qoff = q_row0 // ATT_QB

    def q_spec(j):
        return pl.BlockSpec((1, ATT_QB, 128), lambda b, h, i: (b, qoff + i * n_qin + j, h))

    return pl.pallas_call(
        functools.partial(_attn_kernel, n_qin=n_qin),
        grid=(bsz, NH, n_rows // tq),
        in_specs=[q_spec(j) for j in range(n_qin)]
        + [pl.BlockSpec((1, tk, 128), lambda b, h, i: (b, 0, h)),
           pl.BlockSpec((1, tk, 128), lambda b, h, i: (b, 0, h)),
           _const_spec((8, 128)), _const_spec((1, 128))],
        out_specs=pl.BlockSpec((1, tq, 128), lambda b, h, i: (b, i, h)),
        out_shape=jax.ShapeDtypeStruct((bsz, n_rows, 512), BF16),
        compiler_params=_cparams(("parallel", "parallel", "arbitrary")),
    )(*([q] * n_qin), k, v, lam_arr, g_arr)


def _bwd_tile(i):
    return jnp.where(i == 0, 0, NT - i)


def _zero_at_row_start(*scratch):
    @pl.when(pl.program_id(1) == 0)
    def _():
        for s in scratch:
            s[...] = jnp.zeros_like(s)


def _hg_body(qf_ref, vf_ref, kf_ref, lf_ref, qb_ref, vb_ref, kb_ref, lb_ref,
             of_ref, ob_ref, sf_scr, sb_scr):
    q = (qf_ref[0].astype(F32), qb_ref[0].astype(F32))
    k = (kf_ref[0].astype(F32), kb_ref[0].astype(F32))
    v = (vf_ref[0].astype(F32), vb_ref[0].astype(F32))
    logf = (lf_ref[0], lb_ref[0])
    o_refs = (of_ref, ob_ref)
    dirs = (0, 1)

    r = lax.broadcasted_iota(jnp.int32, (TM, TM), 0)
    c = lax.broadcasted_iota(jnp.int32, (TM, TM), 1)
    same = (r // BLK) == (c // BLK)
    tri = [jnp.where(jnp.logical_and(same, (c >= r) if d else (c <= r)), 1.0, 0.0).astype(BF16) for d in dirs]
    blk = jnp.where(same, 1.0, 0.0).astype(BF16)
    pair_of = jnp.where((r // HG_SB) == (c // HG_SB), 1.0, 0.0).astype(BF16)
    bl = [_dot_01(tri[d], logf[d]) for d in dirs]
    tot = [_dot_01(blk, logf[d]) for d in dirs]
    tot2 = [_dot_01(pair_of, logf[d]) for d in dirs]
    qd16 = [q[d] * jnp.exp(bl[d]) for d in dirs]
    kd16 = [k[d] * jnp.exp(tot[d] - bl[d]) for d in dirs]
    f = [jnp.exp(logf[d]) for d in dirs]
    half = (lax.broadcasted_iota(jnp.int32, (TM, HW), 0) // BLK) % 2
    second = [half == (0 if d else 1) for d in dirs]
    e_other = [jnp.exp(tot2[d] - tot[d]) for d in dirs]
    qdb = [(qd16[d] * jnp.where(second[d], e_other[d], 1.0)).astype(BF16) for d in dirs]
    kd = [(kd16[d] * jnp.where(second[d], 1.0, e_other[d])).astype(BF16) for d in dirs]
    e_blk = [jnp.exp(tot2[d]) for d in dirs]

    def regroup(x):
        return jnp.swapaxes(x.reshape(8, HG_NV, HW), 0, 1).reshape(2, BLK, 8, HW)

    q4 = [regroup(q[d]) for d in dirs]
    k4 = [regroup(k[d]) for d in dirs]
    v4 = [regroup(v[d]) for d in dirs]
    f4 = [regroup(f[d]) for d in dirs]
    ones_bd = _ones_bd()
    e = [None, None]
    o4 = [None, None]
    for n in range(BLK):
        ln = BLK - n
        for d in dirs:
            qs, ks = (slice(0, ln), slice(n, BLK)) if d else (slice(n, BLK), slice(0, ln))
            if n == 0:
                pn = q4[d] * k4[d]
            else:
                fs = slice(n - 1, n - 1 + ln) if d else slice(1, 1 + ln)
                e[d] = f4[d][:, fs] if n == 1 else (e[d][:, :ln] if d else e[d][:, 1:]) * f4[d][:, fs]
                pn = q4[d][:, qs] * k4[d][:, ks] * e[d]
            a = _dot(pn.reshape(2 * ln * 8, HW).astype(BF16), ones_bd).reshape(2, ln, 8, HW) * v4[d][:, ks]
            if n == 0:
                o4[d] = a
            else:
                pad = jnp.zeros((2, n, 8, HW), F32)
                o4[d] = o4[d] + jnp.concatenate([a, pad] if d else [pad, a], axis=1)
    o_band = [jnp.swapaxes(o4[d].reshape(HG_NV, 8, HW), 0, 1).reshape(TM, HW) for d in dirs]

    n_sb = TM // HG_SB
    nq = n_sb * BLK
    hr = lax.broadcasted_iota(jnp.int32, (NH * nq, HW), 0) // nq
    hl = lax.broadcasted_iota(jnp.int32, (NH * nq, HW), 1) // HD
    same_head = hr == hl
    pr = lax.broadcasted_iota(jnp.int32, (nq, NH * nq), 0) // BLK
    pc = (lax.broadcasted_iota(jnp.int32, (nq, NH * nq), 1) % nq) // BLK
    same_step = pr == pc
    zero_b = jnp.zeros((), BF16)
    for d in dirs:
        def blocks(x, which):
            return x.reshape(n_sb, 2, BLK, HW)[:, which if d == 0 else 1 - which].reshape(nq, HW)

        k1 = blocks(kd16[d], 0).astype(BF16)
        v1 = blocks(v[d], 0).astype(BF16)
        q2 = blocks(qd16[d], 1).astype(BF16)
        sc = _dot_nt(q2, jnp.where(same_head, jnp.concatenate([k1] * NH, axis=0), zero_b))
        sc = jnp.where(same_step, sc, 0.0).astype(BF16)
        o2 = _dot(sc, jnp.where(same_head, jnp.concatenate([v1] * NH, axis=0), zero_b)).reshape(n_sb, BLK, HW)
        ob4 = o_band[d].reshape(n_sb, 2, BLK, HW)
        parts = [ob4[:, 0], ob4[:, 1]]
        parts[1 if d == 0 else 0] = parts[1 if d == 0 else 0] + o2
        o_band[d] = jnp.stack(parts, axis=1).reshape(TM, HW)

    lane = lax.broadcasted_iota(jnp.int32, (HD, 128), 1)
    own = [(lane >= HD) if h % 2 else (lane < HD) for h in range(NH)]
    tile = [slice(128 * (h // 2), 128 * (h // 2) + 128) for h in range(NH)]
    zero_t = jnp.zeros((HD, 128), BF16)
    vb = [v[d].astype(BF16) for d in dirs]
    nb = n_sb
    s_scr = (sf_scr, sb_scr)
    s = [[s_scr[d][h] for h in range(NH)] for d in dirs]
    for j in range(nb):
        for d in dirs:
            ib = nb - 1 - j if d else j
            rows = slice(ib * HG_SB, (ib + 1) * HG_SB)
            s_bd = jnp.concatenate(
                [jnp.concatenate([s[d][h].astype(BF16) if c == h // 2 else zero_t for c in range(HW // 128)], axis=1)
                 for h in range(NH)], axis=0)
            o_refs[d][0, rows, :] = (o_band[d][rows] + _dot_nt(qdb[d][rows], s_bd)).astype(BF16)
            u = _dot_tn(vb[d][rows], kd[d][rows])
            e_row = e_blk[d][ib * HG_SB:ib * HG_SB + 1, :]
            for h in range(NH):
                s[d][h] = s[d][h] * e_row[:, tile[h]] + jnp.where(own[h], u[h * HD:(h + 1) * HD, tile[h]], 0.0)
    for d in dirs:
        for h in range(NH):
            s_scr[d][h] = s[d][h]


def _bd4(x, bd):
    return jnp.where(bd, jnp.concatenate([x] * NH, axis=0), 0.0).astype(BF16)


def _split3(x):
    h = x.astype(BF16)
    r1 = x - h.astype(F32)
    m = r1.astype(BF16)
    l = (r1 - m.astype(F32)).astype(BF16)
    return jnp.concatenate([h, m, l], axis=0)


def _dot_01(sel01, x):
    return _dot(jnp.concatenate([sel01] * 3, axis=1), _split3(x))


class _GdMasks:
    def __init__(self, reverse):
        r = lax.broadcasted_iota(jnp.int32, (CH, CH), 0)
        c = lax.broadcasted_iota(jnp.int32, (CH, CH), 1)
        self.tri = jnp.where((c >= r) if reverse else (c <= r), 1.0, 0.0).astype(BF16)
        t_i = lax.broadcasted_iota(jnp.int32, (CH, HW), 0)
        s_i = lax.broadcasted_iota(jnp.int32, (CH, HW), 1) % HD
        self.later = (t_i < s_i) if reverse else (t_i > s_i)
        self.valid = (s_i >= t_i) if reverse else (s_i <= t_i)
        self.strict = (s_i > t_i) if reverse else (s_i < t_i)
        self.eye = jnp.where(t_i == s_i, 1.0, 0.0)
        self.last = 0 if reverse else CH - 1
        self.off = {}
        m = 1
        while m < CH:
            t_blk, s_blk = t_i // m, s_i // m
            pair = (t_blk // 2) == (s_blk // 2)
            lo_hi = (t_blk % 2 == 0, s_blk % 2 == 1) if reverse else (t_blk % 2 == 1, s_blk % 2 == 0)
            self.off[m] = jnp.logical_and(pair, jnp.logical_and(*lo_hi))
            m *= 2


def _gd_prep(q, k, v, ab, dirs, masks, bd):
    grp = lax.broadcasted_iota(jnp.int32, (CH, HW), 1) // HD

    def widen(a, col0):
        out = jnp.zeros((CH, HW), F32)
        for h in range(NH):
            out = jnp.where(grp == h, jnp.broadcast_to(a[:, col0 + h:col0 + h + 1], (CH, HW)), out)
        return out

    ix = range(len(q))
    mk = [masks[d] for d in dirs]
    la = [widen(ab[j], dirs[j] * NH) for j in ix]
    beta = [widen(ab[j], 2 * NH + dirs[j] * NH) for j in ix]
    gc = [_dot_01(mk[j].tri, la[j]) for j in ix]
    diff = [_dot_01(mk[j].tri, jnp.where(mk[j].later, la[j], 0.0)) for j in ix]
    kb = [_bd4(k[j], bd) for j in ix]
    kq = [_dot_nt(jnp.concatenate([k[j].astype(BF16), q[j].astype(BF16)], axis=0), kb[j]) for j in ix]
    kk = [t[:CH] for t in kq]
    qk = [t[CH:] for t in kq]
    dm = [jnp.where(mk[j].valid, jnp.exp(jnp.minimum(diff[j], 0.0)), 0.0) for j in ix]
    n = [jnp.where(mk[j].strict, beta[j] * kk[j] * dm[j], 0.0) for j in ix]

    tinv = [mk[j].eye - jnp.where(mk[j].off[1], n[j], 0.0) for j in ix]
    m = 2
    while m < CH:
        y = [_dot(tinv[j].astype(BF16), _bd4(jnp.where(mk[j].off[m], n[j], 0.0), bd)) for j in ix]
        tinv = [tinv[j] - _dot(y[j].astype(BF16), _bd4(tinv[j], bd)) for j in ix]
        m *= 2
    tb = [t.astype(BF16) for t in tinv]
    eg = [jnp.exp(g) for g in gc]
    u = [_dot(tb[j], _bd4(v[j] * beta[j], bd)) for j in ix]
    w = [_dot(tb[j], _bd4(k[j] * beta[j] * eg[j], bd)) for j in ix]
    sc = [(qk[j] * dm[j]).astype(BF16) for j in ix]
    qeff = [(q[j] * eg[j] - _dot(sc[j], _bd4(w[j], bd))).astype(BF16) for j in ix]
    oc = [_dot(sc[j], _bd4(u[j], bd)) for j in ix]
    gl = [gc[j][mk[j].last:mk[j].last + 1, :] for j in ix]
    kd = [(k[j] * jnp.exp(gl[j] - gc[j])).astype(BF16) for j in ix]
    kwu = [_dot_tn(kd[j], jnp.concatenate([w[j].astype(BF16), u[j].astype(BF16)], axis=1)) for j in ix]
    kw = [t[:, :HW].astype(BF16) for t in kwu]
    ku = [t[:, HW:] for t in kwu]
    a = [jnp.exp(g) for g in gl]
    return qeff, oc, kw, ku, a


def _gd_body(qf_ref, kf_ref, vf_ref, abf_ref, qb_ref, kb_ref, vb_ref, abb_ref,
             of_ref, ob_ref, sf_scr, sb_scr):
    bd = _bd_mask()
    n_ch = TM // CH
    refs = ((qf_ref, kf_ref, vf_ref, abf_ref), (qb_ref, kb_ref, vb_ref, abb_ref))
    o_refs = (of_ref, ob_ref)
    items = [(d, (n_ch - 1 - j) if d else j) for j in range(n_ch) for d in range(2)]
    rows = [slice(ic * CH, (ic + 1) * CH) for _, ic in items]
    dirs = [d for d, _ in items]
    load = lambda which: [refs[d][which][0, r, :] for d, r in zip(dirs, rows)]
    qeff, oc, kw, ku, a = _gd_prep(load(0), load(1), load(2), load(3), dirs, (_GdMasks(False), _GdMasks(True)), bd)

    s = [sf_scr[...], sb_scr[...]]
    for j, (d, _) in enumerate(items):
        sb = s[d].astype(BF16)
        o_refs[d][0, rows[j], :] = (_dot(qeff[j], sb) + oc[j]).astype(BF16)
        s[d] = jnp.where(bd, s[d] * a[j] - _dot(kw[j], sb) + ku[j], 0.0)
    sf_scr[...] = s[0]
    sb_scr[...] = s[1]


def _scans_kernel(*refs):
    hg_in, gd_in = refs[0:8], refs[8:16]
    hg_out, gd_out = refs[16:18], refs[18:20]
    hg_scr, gd_scr = refs[20:22], refs[22:24]
    _zero_at_row_start(*hg_scr, *gd_scr)
    _hg_body(*hg_in, *hg_out, *hg_scr)
    _gd_body(*gd_in, *gd_out, *gd_scr)


def _scans_call(hq, hv, hk, hlf, gq, gk, gv, gab):
    bsz = hq.shape[0]
    f = lambda b, i: (b, i, 0)
    g0 = lambda b, i: (b, _bwd_tile(i), 0)
    g1 = lambda b, i: (b, _bwd_tile(i), 1)
    blk = lambda m: pl.BlockSpec((1, TM, HW), m)
    abs_ = lambda m: pl.BlockSpec((1, TM, 128), m)
    return pl.pallas_call(
        _scans_kernel,
        grid=(bsz, NT),
        in_specs=[blk(f), blk(f), blk(f), blk(f), blk(g0), blk(g0), blk(g1), blk(g1),
                  blk(f), blk(f), blk(f), abs_(f), blk(g0), blk(g0), blk(g0), abs_(g0)],
        out_specs=[blk(f), blk(g0), blk(f), blk(g0)],
        out_shape=[jax.ShapeDtypeStruct((bsz, T, HW), BF16)] * 4,
        scratch_shapes=[pltpu.VMEM((NH, HD, 128), F32)] * 2 + [pltpu.VMEM((HW, HW), F32)] * 2,
        compiler_params=_cparams(("parallel", "arbitrary")),
    )(hq, hv, hk, hlf, hq, hv, hk, hlf, gq, gk, gv, gab, gq, gk, gv, gab)


def _outproj_kernel(*refs, with_ctx, split):
    if split:
        ctx_ref, refs = refs[0], refs[1:]
    (x_ref, oal_ref, oac_ref, hof_ref, hob_ref, hsg_ref, gof_ref, gob_ref, gsg_ref,
     wo_ref, hng_ref, gng_ref, g1_ref, gate_ref, o_ref) = refs
    ones_bd = _ones_bd()
    oa = oal_ref[0]
    x = x_ref[0]
    if with_ctx:
        oa = jnp.where(pl.program_id(1) == 0, oac_ref[0], oa)
    if split:
        x = jnp.where(pl.program_id(1) == 0, ctx_ref[0], x)

    def finish(of_ref, ob_ref, sg_ref, ng_ref):
        o = of_ref[0].astype(F32) + ob_ref[0].astype(F32)
        ms = _head_sum(o * o, ones_bd) * (1.0 / HD)
        return (o * lax.rsqrt(ms + EPS) * ng_ref[...] * sg_ref[0]).astype(BF16)

    ob = finish(hof_ref, hob_ref, hsg_ref, hng_ref)
    oc = finish(gof_ref, gob_ref, gsg_ref, gng_ref)
    mix = _dot(jnp.concatenate([oa, ob, oc], axis=1), wo_ref[0])
    ms = jnp.mean(mix * mix, axis=-1, keepdims=True)
    o_ref[0] = x + gate_ref[0, 0] * (mix * lax.rsqrt(ms + EPS) * g1_ref[...])


def _outproj_call(xs, oa_lat, oa_ctx, hof, hob, hsg, gof, gob, gsg, wo, layer, hng, gng, g1, gate, with_ctx):
    split = isinstance(xs, tuple)
    assert with_ctx or not split
    bsz = oa_lat.shape[0]
    t0 = 0 if with_ctx else 1
    row = lambda wd: pl.BlockSpec((1, TM, wd), lambda b, i: (b, i + t0, 0))
    if split:
        x_specs = [pl.BlockSpec((1, TM, D), lambda b, i: (b, 0, 0)),
                   pl.BlockSpec((1, TM, D), lambda b, i: (b, jnp.maximum(i - 1, 0), 0))]
        x_args = list(xs)
    else:
        x_specs, x_args = [row(D)], [xs]
    oal = pl.BlockSpec((1, TM, 512), lambda b, i: (b, jnp.maximum(i + t0 - 1, 0), 0))
    oac = pl.BlockSpec((1, TM, 512), lambda b, i: (b, 0, 0))
    mod = pl.BlockSpec((1, 1, 1, D), lambda b, i: (b, _sel(i + t0, 1), 0, 0))
    return pl.pallas_call(
        functools.partial(_outproj_kernel, with_ctx=with_ctx, split=split),
        grid=(bsz, NT - t0),
        in_specs=x_specs + [oal, oac, row(HW), row(HW), row(HW), row(HW), row(HW), row(HW),
                            _layer_spec(wo, layer), _const_spec((1, HW)), _const_spec((1, HW)), _const_spec((1, D)),
                            mod],
        out_specs=pl.BlockSpec((1, TM, D), lambda b, i: (b, i, 0)),
        out_shape=jax.ShapeDtypeStruct((bsz, (NT - t0) * TM, D), F32),
        compiler_params=_cparams(("parallel", "arbitrary")),
    )(*x_args, oa_lat, oa_ctx, hof, hob, hsg, gof, gob, gsg, wo, hng, gng, g1, gate)


def _regroup_rows(x):
    return jnp.swapaxes(x.reshape(8, TM // 8, x.shape[1]), 0, 1).reshape(TM, x.shape[1])


def _ungroup_rows(x):
    return jnp.swapaxes(x.reshape(TM // 8, 8, x.shape[1]), 0, 1).reshape(TM, x.shape[1])


def _ffn_kernel(xp_ref, xm_ref, xn_ref, a_ref, s_ref, wup_ref, cw_ref, cb_ref, wdn_ref, g3_ref, gate_ref,
                o_ref, h_scr, act_scr, *, nct, nt):
    i = pl.program_id(1)
    a, s = a_ref[0, 0], s_ref[0, 0]
    ng = TM // 8
    xg = _regroup_rows(xm_ref[0])
    h_scr[0:TM, :] = _normed(xg, a, s).astype(BF16)
    hp = jnp.where(_prev_ok(i, nct), _normed(xp_ref[0, HALO - 8:HALO, :], a, s), 0.0)
    hn = jnp.where(_next_ok(i, nct, nt), _normed(xn_ref[0, 0:8, :], a, s), 0.0)
    h_scr[TM:TM + 16, :] = jnp.concatenate([hp, hn], axis=0).astype(BF16)
    h = h_scr[...]
    sub = lax.broadcasted_iota(jnp.int32, (8, FF_CW), 0)

    def conv(col0, lo):
        u = _dot(h, wup_ref[0, :, col0 + lo:col0 + lo + FF_CW])
        ur = u[0:TM].reshape(ng, 8, FF_CW)
        first = jnp.where(sub == 0, u[TM + 7:TM + 8], pltpu.roll(ur[ng - 1], 1, 0))
        last = jnp.where(sub == 7, u[TM + 8:TM + 9], pltpu.roll(ur[0], 7, 0))
        w = cw_ref[:, col0 + lo:col0 + lo + FF_CW]
        return (jnp.concatenate([first[None], ur[:-1]], axis=0) * w[0:1] + ur * w[1:2]
                + jnp.concatenate([ur[1:], last[None]], axis=0) * w[2:3] + cb_ref[:, col0 + lo:col0 + lo + FF_CW])

    for cidx in range(D_FF // FF_CW):
        lo = cidx * FF_CW
        act = _silu(conv(0, lo)) * conv(D_FF, lo)
        act_scr[:, lo:lo + FF_CW] = act.reshape(TM, FF_CW).astype(BF16)
    ff = _dot(act_scr[...], wdn_ref[0])
    ms = jnp.mean(ff * ff, axis=-1, keepdims=True)
    o_ref[0] = _ungroup_rows(xg + gate_ref[0, 0] * (ff * lax.rsqrt(ms + EPS) * g3_ref[...]))


def _ffn_call(xs, a_ff, s_ff, wup, cw, cb, wdn, layer, g3, gate, nct):
    bsz, rows, _ = xs.shape
    nt = rows // TM
    mod = _mod_spec(nct)
    return pl.pallas_call(
        functools.partial(_ffn_kernel, nct=nct, nt=nt),
        grid=(bsz, nt),
        in_specs=_halo_specs(D, nct, nt) + [
            mod, mod, _layer_spec(wup, layer), _const_spec((3, 2 * D_FF)), _const_spec((1, 2 * D_FF)),
            _layer_spec(wdn, layer), _const_spec((1, D)), mod],
        out_specs=pl.BlockSpec((1, TM, D), lambda b, i: (b, i, 0)),
        out_shape=jax.ShapeDtypeStruct((bsz, rows, D), F32),
        scratch_shapes=[pltpu.VMEM((TM + 16, D), BF16), pltpu.VMEM((TM, D_FF), BF16)],
        compiler_params=_cparams(("parallel", "arbitrary")),
    )(xs, xs, xs, a_ff, s_ff, wup, cw, cb, wdn, g3, gate)


def _rope_tables():
    n_freq = HD // 4
    inv = ROPE_THETA ** (-jnp.arange(n_freq, dtype=F32) / n_freq)
    rows = jnp.repeat(jnp.arange(SEQ // GRID_W, dtype=F32), GRID_W)
    cols = jnp.tile(jnp.arange(GRID_W, dtype=F32), SEQ // GRID_W)
    ang = jnp.concatenate([rows[:, None] * inv, cols[:, None] * inv], axis=-1)
    cos, sin = jnp.cos(ang), jnp.sin(ang)
    cos_l = jnp.tile(jnp.concatenate([cos, cos], axis=-1), (1, 2))
    sin_l = jnp.tile(jnp.concatenate([-sin, sin], axis=-1), (1, 2))
    cos_t = jnp.concatenate([jnp.ones((CTX, 128), F32), cos_l], axis=0)
    sin_t = jnp.concatenate([jnp.zeros((CTX, 128), F32), sin_l], axis=0)
    return cos_t, sin_t


def kernel(x, c, ctx, c_ctx, ada_w, ada_b, norm_g, w_in, w_out, da_lambda, da_subln_g, hg_lb_logits, hg_norm_g,
           gd_conv_w, gd_a_log, gd_dt_bias, gd_norm_g, ffn_w_up, ffn_conv_w, ffn_conv_b, ffn_w_down):
    bsz = x.shape[0]
    depth = ada_w.shape[0]
    assert x.shape == (bsz, SEQ, D) and ctx.shape == (bsz, CTX, D) and bsz <= 8
    cos_t, sin_t = _rope_tables()

    cond = jnp.concatenate([jax.nn.silu(c.astype(F32)), jnp.zeros((8 - bsz, D), F32),
                            jax.nn.silu(c_ctx.astype(F32))[None], jnp.zeros((7, D), F32)], axis=0)
    mods = _ada_call(cond, ada_w, ada_b).reshape(depth, 16, 6, D)

    mod = jnp.stack([jnp.broadcast_to(mods[:, 8:9], (depth, bsz, 6, D)), mods[:, :bsz]], axis=2)
    g = norm_g.astype(F32)[:, None, None]
    a_in, s_in = g[..., 0:1, :] * (1.0 + mod[..., 1:2, :]), mod[..., 0:1, :]
    a_ff, s_ff = g[..., 2:3, :] * (1.0 + mod[..., 4:5, :]), mod[..., 3:4, :]
    gate1, gate2 = mod[..., 2:3, :], mod[..., 5:6, :]
    g1, g3 = norm_g[:, 1:2].astype(F32), norm_g[:, 3:4].astype(F32)
    lb_w = jax.nn.softmax(hg_lb_logits.astype(F32), axis=0)
    lb = (jnp.cumsum(lb_w, axis=0) - lb_w[0]).reshape(depth, 1, 2 * HW)
    lb1m, lbm = 1.0 - lb, jnp.maximum(lb, LB_FLOOR)
    lane_pad = ((0, 0), (0, 0), (0, 128 - 2 * NH))
    nega = jnp.pad(-jnp.exp(gd_a_log.astype(F32)).reshape(depth, 1, 2 * NH), lane_pad)
    dtb = jnp.pad(gd_dt_bias.astype(F32).reshape(depth, 1, 2 * NH), lane_pad)
    lam_init = jnp.asarray([0.8 - 0.6 * math.exp(-0.3 * layer) for layer in range(depth)], F32)
    lp = da_lambda.astype(F32)
    lam = jnp.exp(jnp.sum(lp[:, 0] * lp[:, 1], axis=-1)) - jnp.exp(jnp.sum(lp[:, 2] * lp[:, 3], axis=-1)) + lam_init
    lam_arr = jnp.broadcast_to(lam[:, None, None], (depth, 8, 128))
    g_arr = (da_subln_g.astype(F32) * (1.0 - lam_init)[:, None]).reshape(depth, 1, DV)
    hng = jnp.tile(hg_norm_g.astype(F32), (1, NH)).reshape(depth, 1, HW)
    gng = jnp.tile(gd_norm_g.astype(F32), (1, NH)).reshape(depth, 1, HW)
    conv_g, conv_f = gd_conv_w.astype(F32), ffn_conv_w.astype(F32)
    conv_fb = ffn_conv_b.astype(F32).reshape(depth, 1, 2 * D_FF)
    wo_b, wup_b, wdn_b = w_out.astype(BF16), ffn_w_up.astype(BF16), ffn_w_down.astype(BF16)
    w_in_t = jnp.swapaxes(w_in.astype(F32), 1, 2)

    xs = (ctx.astype(F32), x.astype(F32))
    for layer in range(depth):
        need_ctx = layer < depth - 1
        consts = [lb1m[layer], lbm[layer], conv_g[layer], nega[layer], dtb[layer]]
        (q, k, v, hq, hv, hk, hlf, hsg, gq, gk, gv, gab, gsg) = _inproj_call(
            xs, a_in[layer], s_in[layer], cos_t, sin_t, w_in_t, layer, consts)

        oa_lat = _attn_call(q, k, v, lam_arr[layer], g_arr[layer], CTX, SEQ, T)
        oa_ctx = _attn_call(q, k, v, lam_arr[layer], g_arr[layer], 0, CTX, CTX) if need_ctx else oa_lat

        hof, hob, gof, gob = _scans_call(hq, hv, hk, hlf, gq, gk, gv, gab)

        xs = _outproj_call(xs, oa_lat, oa_ctx, hof, hob, hsg, gof, gob, gsg, wo_b, layer,
                           hng[layer], gng[layer], g1[layer], gate1[layer], need_ctx)
        xs = _ffn_call(xs, a_ff[layer], s_ff[layer], wup_b, conv_f[layer], conv_fb[layer], wdn_b, layer,
                       g3[layer], gate2[layer], 1 if need_ctx else 0)
    return xs
```

```python
import functools
import math

import jax
import jax.numpy as jnp
import numpy as np
from jax import lax
from jax.experimental import pallas as pl
from jax.experimental.pallas import tpu as pltpu

F32 = jnp.float32
BF16 = jnp.bfloat16

D = 1024
CTX = 256
SEQ = 2048
T = CTX + SEQ
GRID_W = 64
ROPE_THETA = 10000.0
EPS = 1e-6
LB_FLOOR = 1e-30
NH = 4
HD = 64
DV = 128
HW = NH * HD
D_FF = 2816
TM = 256
NT = T // TM
HALO = 16
TQ = 128
ATT_QB = 256
ATT_QIN = 4
Q_SCALE = HD ** -0.5 * math.log2(math.e)
BLK = 16
HG_NV = TM // 8
assert HG_NV == 2 * BLK
HG_SB = 2 * BLK
CH = 64
FF_CW = 256
VMEM_LIMIT = 56 * 1024 * 1024

C_DAQ, C_DAK, C_DAV = 0, 512, 1024
C_HGQ, C_HGI, C_HGF, C_HGG = 1536, 1792, 2048, 2560
C_GDQKV, C_GDA, C_GDB, C_GDG = 2816, 3584, 3592, 3600
IN_COLS = 3856


def _cparams(sem):
    return pltpu.CompilerParams(dimension_semantics=sem, vmem_limit_bytes=VMEM_LIMIT)


def _const_spec(shape):
    n = len(shape)
    return pl.BlockSpec(shape, lambda *_: (0,) * n)


def _layer_spec(stacked, layer):
    n = stacked.ndim - 1
    return pl.BlockSpec((1,) + stacked.shape[1:], lambda *_: (layer,) + (0,) * n)


def _sigmoid(x):
    return 1.0 / (1.0 + jnp.exp(-x))


def _silu(x):
    return x * _sigmoid(x)


def _softplus(x):
    return jnp.maximum(x, 0.0) + jnp.log(1.0 + jnp.exp(-jnp.abs(x)))


def _dot(a, b):
    return jnp.dot(a, b, preferred_element_type=F32)


def _dot_nt(a, b):
    return lax.dot_general(a, b, (((1,), (1,)), ((), ())), preferred_element_type=F32)


def _dot_tn(a, b):
    return lax.dot_general(a, b, (((0,), (0,)), ((), ())), preferred_element_type=F32)


def _dot_hi(a, b):
    return jnp.dot(a, b, preferred_element_type=F32, precision=lax.Precision.HIGHEST)


def _head_sum(x, ones_bd):
    return _dot(x.astype(BF16), ones_bd)


def _ones_bd():
    r = lax.broadcasted_iota(jnp.int32, (HW, HW), 0) // HD
    c = lax.broadcasted_iota(jnp.int32, (HW, HW), 1) // HD
    return jnp.where(r == c, 1.0, 0.0).astype(BF16)


def _bd_mask():
    r = lax.broadcasted_iota(jnp.int32, (HW, HW), 0) // HD
    c = lax.broadcasted_iota(jnp.int32, (HW, HW), 1) // HD
    return r == c


def _normed(x, a, s):
    ms = jnp.mean(x * x, axis=-1, keepdims=True)
    return (x * lax.rsqrt(ms + EPS)) * a + s


def _ada_kernel(c_ref, w_ref, b_ref, o_ref):
    o_ref[0] = _dot(c_ref[...].astype(BF16), w_ref[0].astype(BF16)) + b_ref[0]


def _ada_call(cond, ada_w, ada_b):
    depth = ada_w.shape[0]
    nc = 6 * D
    cw = 1536
    return pl.pallas_call(
        _ada_kernel,
        grid=(depth, nc // cw),
        in_specs=[pl.BlockSpec((16, D), lambda l, j: (0, 0)),
                  pl.BlockSpec((1, D, cw), lambda l, j: (l, 0, j)),
                  pl.BlockSpec((1, 1, cw), lambda l, j: (l, 0, j))],
        out_specs=pl.BlockSpec((1, 16, cw), lambda l, j: (l, 0, j)),
        out_shape=jax.ShapeDtypeStruct((depth, 16, nc), F32),
        compiler_params=_cparams(("arbitrary", "arbitrary")),
    )(cond, ada_w, ada_b.reshape(depth, 1, nc))


def _sel(i, nct):
    return jnp.where(i >= nct, 1, 0)


def _prev_ok(i, nct):
    return i > nct


def _next_ok(i, nct, nt):
    return jnp.logical_and(i >= nct, i <= nt - 2)


def _halo_specs(width, nct, nt):
    per = TM // HALO
    return [pl.BlockSpec((1, HALO, width), lambda b, i: (b, i * per - jnp.where(_prev_ok(i, nct), 1, 0), 0)),
            pl.BlockSpec((1, TM, width), lambda b, i: (b, i, 0)),
            pl.BlockSpec((1, HALO, width), lambda b, i: (b, (i + 1) * per - jnp.where(_next_ok(i, nct, nt), 0, 1), 0))]


def _mod_spec(nct):
    return pl.BlockSpec((1, 1, 1, D), lambda b, i: (b, _sel(i, nct), 0, 0))


def _fill_h(h_scr, xp, xm, xn, a, s, i, nct, nt):
    hp = _normed(xp, a, s)
    hn = _normed(xn, a, s)
    h_scr[0:HALO, :] = jnp.where(_prev_ok(i, nct), hp, 0.0).astype(BF16)
    h_scr[HALO:HALO + TM, :] = _normed(xm, a, s).astype(BF16)
    h_scr[HALO + TM:, :] = jnp.where(_next_ok(i, nct, nt), hn, 0.0).astype(BF16)


def _conv3(u_scr, w_ref):
    return (u_scr[HALO - 1:HALO - 1 + TM, :] * w_ref[0:1, :]
            + u_scr[HALO:HALO + TM, :] * w_ref[1:2, :]
            + u_scr[HALO + 1:HALO + 1 + TM, :] * w_ref[2:3, :])


def _inproj_kernel(*refs, split):
    if split:
        ctx_ref, refs = refs[0], refs[1:]
    (xp_ref, xm_ref, xn_ref, a_ref, s_ref, cos_ref, sin_ref, win_ref,
     lb1m_ref, lbm_ref, gconv_ref, nega_ref, dtb_ref,
     q_ref, k_ref, v_ref, hq_ref, hv_ref, hk_ref, hlf_ref, hsg_ref,
     gq_ref, gk_ref, gv_ref, gab_ref, gsg_ref,
     h_scr, u_scr, wqk_ref, wv_ref, whg_ref, wgq_ref, wab_ref, wgg_ref) = refs
    i = pl.program_id(1)

    @pl.when(jnp.logical_and(pl.program_id(0) == 0, i == 0))
    def _():
        wqk_ref[...] = win_ref[0, C_DAQ:C_DAV, :].T.astype(BF16)
        wv_ref[...] = win_ref[0, C_DAV:C_HGQ, :].T.astype(BF16)
        whg_ref[...] = win_ref[0, C_HGQ:C_GDQKV, :].T.astype(BF16)
        wgq_ref[...] = win_ref[0, C_GDQKV:C_GDA, :].T.astype(BF16)
        ab_lane = lax.broadcasted_iota(jnp.int32, (D, 128), 1)
        wab_ref[...] = jnp.where(ab_lane < 4 * NH, win_ref[0, C_GDA:C_GDA + 128, :].T, 0.0).astype(BF16)
        wgg_ref[...] = win_ref[0, C_GDG:IN_COLS, :].T.astype(BF16)

    xm = jnp.where(i == 0, ctx_ref[0], xm_ref[0]) if split else xm_ref[0]
    _fill_h(h_scr, xp_ref[0], xm, xn_ref[0], a_ref[0, 0], s_ref[0, 0], i, 1, NT)
    h = h_scr[HALO:HALO + TM, :]

    u_scr[...] = _dot(h_scr[...], wgq_ref[...])
    z_hg = _dot(h, whg_ref[...])
    z_qk = _dot(h, wqk_ref[...])
    z_v = _dot(h, wv_ref[...])
    z_ab = _dot(h, wab_ref[...])
    z_gg = _dot(h, wgg_ref[...])

    lane = lax.broadcasted_iota(jnp.int32, (TM, 128), 1)
    first_half = (lane % HD) < (HD // 2)
    cs, sn = cos_ref[...], sin_ref[...]
    for j in range(8):
        xj = z_qk[:, j * 128:(j + 1) * 128]
        sw = jnp.where(first_half, pltpu.roll(xj, 128 - HD // 2, 1), pltpu.roll(xj, HD // 2, 1))
        r = xj * cs + sw * sn
        if j < 4:
            q_ref[0, :, j * 128:(j + 1) * 128] = (r * Q_SCALE).astype(BF16)
        else:
            k_ref[0, :, (j - 4) * 128:(j - 3) * 128] = r.astype(BF16)
    v_ref[0] = z_v.astype(BF16)

    z = z_hg
    hq_ref[0] = _silu(z[:, 0:HW]).astype(BF16)
    hv_ref[0] = z[:, HW:2 * HW].astype(BF16)
    sg = _sigmoid(z[:, 2 * HW:4 * HW])
    hk_ref[0] = (lb1m_ref[...] * (1.0 - sg)).astype(BF16)
    hlf_ref[0] = jnp.log(lbm_ref[...] + lb1m_ref[...] * sg)
    hsg_ref[0] = _silu(z[:, 4 * HW:5 * HW]).astype(BF16)

    y = _silu(_conv3(u_scr, gconv_ref))
    ones_bd = _ones_bd()
    qg, kg = y[:, 0:HW], y[:, HW:2 * HW]
    gq_ref[0] = (qg * lax.rsqrt(_head_sum(qg * qg, ones_bd) + EPS) * (HD ** -0.5)).astype(BF16)
    gk_ref[0] = (kg * lax.rsqrt(_head_sum(kg * kg, ones_bd) + EPS)).astype(BF16)
    gv_ref[0] = y[:, 2 * HW:3 * HW].astype(BF16)
    gab_ref[0] = jnp.where(lane < 2 * NH, nega_ref[...] * _softplus(z_ab + dtb_ref[...]), _sigmoid(z_ab))
    gsg_ref[0] = _silu(z_gg).astype(BF16)


def _split_x_specs():
    per = TM // HALO
    lat = lambda i: jnp.maximum(i - 1, 0)
    return [pl.BlockSpec((1, TM, D), lambda b, i: (b, 0, 0)),
            pl.BlockSpec((1, HALO, D), lambda b, i: (b, lat(i) * per - jnp.where(_prev_ok(i, 1), 1, 0), 0)),
            pl.BlockSpec((1, TM, D), lambda b, i: (b, lat(i), 0)),
            pl.BlockSpec((1, HALO, D), lambda b, i: (b, (lat(i) + 1) * per - jnp.where(_next_ok(i, 1, NT), 0, 1), 0))]


def _inproj_call(xs, a_in, s_in, cos_t, sin_t, w_in, layer, consts):
    split = isinstance(xs, tuple)
    w_spec = pl.BlockSpec((1, IN_COLS, D), lambda b, i: (layer, 0, 0))
    w_groups = [C_DAV - C_DAQ, C_HGQ - C_DAV, C_GDQKV - C_HGQ, C_GDA - C_GDQKV, 128, IN_COLS - C_GDG]
    x_args = (xs[0], xs[1], xs[1], xs[1]) if split else (xs, xs, xs)
    bsz = x_args[0].shape[0]
    row = lambda wd: pl.BlockSpec((1, TM, wd), lambda b, i: (b, i, 0))
    tab = pl.BlockSpec((TM, 128), lambda b, i: (i, 0))
    out_w = [(512, BF16), (512, BF16), (512, BF16), (HW, BF16), (HW, BF16), (2 * HW, BF16), (2 * HW, F32),
             (HW, BF16), (HW, BF16), (HW, BF16), (HW, BF16), (128, F32), (HW, BF16)]
    return pl.pallas_call(
        functools.partial(_inproj_kernel, split=split),
        grid=(bsz, NT),
        in_specs=(_split_x_specs() if split else _halo_specs(D, 1, NT)) + [_mod_spec(1), _mod_spec(1), tab, tab]
        + [w_spec] + [_const_spec(x.shape) for x in consts],
        out_specs=[row(wd) for wd, _ in out_w],
        out_shape=[jax.ShapeDtypeStruct((bsz, T, wd), dt) for wd, dt in out_w],
        scratch_shapes=[pltpu.VMEM((TM + 2 * HALO, D), BF16), pltpu.VMEM((TM + 2 * HALO, 3 * HW), F32)]
        + [pltpu.VMEM((D, wd), BF16) for wd in w_groups],
        compiler_params=_cparams(("arbitrary", "arbitrary")),
    )(*x_args, a_in, s_in, cos_t, sin_t, w_in, *consts)


def _attn_kernel(*refs, n_qin):
    q_refs = refs[:n_qin]
    k_ref, v_ref, lam_ref, g_ref, o_ref = refs[n_qin:]
    k, v = k_ref[0], v_ref[0]
    lane = lax.broadcasted_iota(jnp.int32, (TQ, 128), 1)
    zero = jnp.zeros((TQ, 128), BF16)
    per = ATT_QB // TQ
    ix = range(n_qin * per)
    q = [q_refs[j // per][0, (j % per) * TQ:(j % per + 1) * TQ, :] for j in ix]
    qs = [jnp.concatenate([jnp.where(lane < HD, q[j], zero), jnp.where(lane >= HD, q[j], zero)], axis=0) for j in ix]
    st = [_dot_nt(k, qs[j]) for j in ix]
    m = [jnp.max(st[j], axis=0, keepdims=True) for j in ix]
    p = [jnp.exp2(st[j] - m[j]) for j in ix]
    l = [jnp.sum(p[j], axis=0, keepdims=True) for j in ix]
    ot = [_dot_tn(v, p[j].astype(BF16)) * (1.0 / l[j]) for j in ix]
    for j in ix:
        d = ot[j][:, :TQ] - lam_ref[0:1, :] * ot[j][:, TQ:]
        ms = jnp.mean(d * d, axis=0, keepdims=True)
        dn = d * lax.rsqrt(ms + EPS)
        o_ref[0, j * TQ:(j + 1) * TQ, :] = (dn.T * g_ref[...]).astype(BF16)


def _attn_call(q, k, v, lam_arr, g_arr, q_row0, n_rows, tk):
    bsz = q.shape[0]
    n_qin = min(ATT_QIN, n_rows // ATT_QB)
    tq = n_qin * ATT_QB
    assert q_row0 % ATT_QB == 0 and n_rows % tq == 0
    qoff = q_row0 // ATT_QB

    def q_spec(j):
        return pl.BlockSpec((1, ATT_QB, 128), lambda b, h, i: (b, qoff + i * n_qin + j, h))

    return pl.pallas_call(
        functools.partial(_attn_kernel, n_qin=n_qin),
        grid=(bsz, NH, n_rows // tq),
        in_specs=[q_spec(j) for j in range(n_qin)]
        + [pl.BlockSpec((1, tk, 128), lambda b, h, i: (b, 0, h)),
           pl.BlockSpec((1, tk, 128), lambda b, h, i: (b, 0, h)),
           _const_spec((8, 128)), _const_spec((1, 128))],
        out_specs=pl.BlockSpec((1, tq, 128), lambda b, h, i: (b, i, h)),
        out_shape=jax.ShapeDtypeStruct((bsz, n_rows, 512), BF16),
        compiler_params=_cparams(("parallel", "parallel", "arbitrary")),
    )(*([q] * n_qin), k, v, lam_arr, g_arr)


def _bwd_tile(i):
    return jnp.where(i == 0, 0, NT - i)


def _zero_at_row_start(*scratch):
    @pl.when(pl.program_id(1) == 0)
    def _():
        for s in scratch:
            s[...] = jnp.zeros_like(s)


def _hg_body(qf_ref, vf_ref, kf_ref, lf_ref, qb_ref, vb_ref, kb_ref, lb_ref,
             of_ref, ob_ref, sf_scr, sb_scr):
    q = (qf_ref[0].astype(F32), qb_ref[0].astype(F32))
    k = (kf_ref[0].astype(F32), kb_ref[0].astype(F32))
    v = (vf_ref[0].astype(F32), vb_ref[0].astype(F32))
    logf = (lf_ref[0], lb_ref[0])
    o_refs = (of_ref, ob_ref)
    dirs = (0, 1)

    r = lax.broadcasted_iota(jnp.int32, (TM, TM), 0)
    c = lax.broadcasted_iota(jnp.int32, (TM, TM), 1)
    same = (r // BLK) == (c // BLK)
    tri = [jnp.where(jnp.logical_and(same, (c >= r) if d else (c <= r)), 1.0, 0.0).astype(BF16) for d in dirs]
    blk = jnp.where(same, 1.0, 0.0).astype(BF16)
    pair_of = jnp.where((r // HG_SB) == (c // HG_SB), 1.0, 0.0).astype(BF16)
    bl = [_dot_01(tri[d], logf[d]) for d in dirs]
    tot = [_dot_01(blk, logf[d]) for d in dirs]
    tot2 = [_dot_01(pair_of, logf[d]) for d in dirs]
    qd16 = [q[d] * jnp.exp(bl[d]) for d in dirs]
    kd16 = [k[d] * jnp.exp(tot[d] - bl[d]) for d in dirs]
    f = [jnp.exp(logf[d]) for d in dirs]
    half = (lax.broadcasted_iota(jnp.int32, (TM, HW), 0) // BLK) % 2
    second = [half == (0 if d else 1) for d in dirs]
    e_other = [jnp.exp(tot2[d] - tot[d]) for d in dirs]
    qdb = [(qd16[d] * jnp.where(second[d], e_other[d], 1.0)).astype(BF16) for d in dirs]
    kd = [(kd16[d] * jnp.where(second[d], 1.0, e_other[d])).astype(BF16) for d in dirs]
    e_blk = [jnp.exp(tot2[d]) for d in dirs]

    def regroup(x):
        return jnp.swapaxes(x.reshape(8, HG_NV, HW), 0, 1).reshape(2, BLK, 8, HW)

    q4 = [regroup(q[d]) for d in dirs]
    k4 = [regroup(k[d]) for d in dirs]
    v4 = [regroup(v[d]) for d in dirs]
    f4 = [regroup(f[d]) for d in dirs]
    ones_bd = _ones_bd()
    e = [None, None]
    o4 = [None, None]
    for n in range(BLK):
        ln = BLK - n
        for d in dirs:
            qs, ks = (slice(0, ln), slice(n, BLK)) if d else (slice(n, BLK), slice(0, ln))
            if n == 0:
                pn = q4[d] * k4[d]
            else:
                fs = slice(n - 1, n - 1 + ln) if d else slice(1, 1 + ln)
                e[d] = f4[d][:, fs] if n == 1 else (e[d][:, :ln] if d else e[d][:, 1:]) * f4[d][:, fs]
                pn = q4[d][:, qs] * k4[d][:, ks] * e[d]
            a = _dot(pn.reshape(2 * ln * 8, HW).astype(BF16), ones_bd).reshape(2, ln, 8, HW) * v4[d][:, ks]
            if n == 0:
                o4[d] = a
            else:
                pad = jnp.zeros((2, n, 8, HW), F32)
                o4[d] = o4[d] + jnp.concatenate([a, pad] if d else [pad, a], axis=1)
    o_band = [jnp.swapaxes(o4[d].reshape(HG_NV, 8, HW), 0, 1).reshape(TM, HW) for d in dirs]

    n_sb = TM // HG_SB
    nq = n_sb * BLK
    hr = lax.broadcasted_iota(jnp.int32, (NH * nq, HW), 0) // nq
    hl = lax.broadcasted_iota(jnp.int32, (NH * nq, HW), 1) // HD
    same_head = hr == hl
    pr = lax.broadcasted_iota(jnp.int32, (nq, NH * nq), 0) // BLK
    pc = (lax.broadcasted_iota(jnp.int32, (nq, NH * nq), 1) % nq) // BLK
    same_step = pr == pc
    zero_b = jnp.zeros((), BF16)
    for d in dirs:
        def blocks(x, which):
            return x.reshape(n_sb, 2, BLK, HW)[:, which if d == 0 else 1 - which].reshape(nq, HW)

        k1 = blocks(kd16[d], 0).astype(BF16)
        v1 = blocks(v[d], 0).astype(BF16)
        q2 = blocks(qd16[d], 1).astype(BF16)
        sc = _dot_nt(q2, jnp.where(same_head, jnp.concatenate([k1] * NH, axis=0), zero_b))
        sc = jnp.where(same_step, sc, 0.0).astype(BF16)
        o2 = _dot(sc, jnp.where(same_head, jnp.concatenate([v1] * NH, axis=0), zero_b)).reshape(n_sb, BLK, HW)
        ob4 = o_band[d].reshape(n_sb, 2, BLK, HW)
        parts = [ob4[:, 0], ob4[:, 1]]
        parts[1 if d == 0 else 0] = parts[1 if d == 0 else 0] + o2
        o_band[d] = jnp.stack(parts, axis=1).reshape(TM, HW)

    lane = lax.broadcasted_iota(jnp.int32, (HD, 128), 1)
    own = [(lane >= HD) if h % 2 else (lane < HD) for h in range(NH)]
    tile = [slice(128 * (h // 2), 128 * (h // 2) + 128) for h in range(NH)]
    zero_t = jnp.zeros((HD, 128), BF16)
    vb = [v[d].astype(BF16) for d in dirs]
    nb = n_sb
    s_scr = (sf_scr, sb_scr)
    s = [[s_scr[d][h] for h in range(NH)] for d in dirs]
    for j in range(nb):
        for d in dirs:
            ib = nb - 1 - j if d else j
            rows = slice(ib * HG_SB, (ib + 1) * HG_SB)
            s_bd = jnp.concatenate(
                [jnp.concatenate([s[d][h].astype(BF16) if c == h // 2 else zero_t for c in range(HW // 128)], axis=1)
                 for h in range(NH)], axis=0)
            o_refs[d][0, rows, :] = (o_band[d][rows] + _dot_nt(qdb[d][rows], s_bd)).astype(BF16)
            u = _dot_tn(vb[d][rows], kd[d][rows])
            e_row = e_blk[d][ib * HG_SB:ib * HG_SB + 1, :]
            for h in range(NH):
                s[d][h] = s[d][h] * e_row[:, tile[h]] + jnp.where(own[h], u[h * HD:(h + 1) * HD, tile[h]], 0.0)
    for d in dirs:
        for h in range(NH):
            s_scr[d][h] = s[d][h]


def _bd4(x, bd):
    return jnp.where(bd, jnp.concatenate([x] * NH, axis=0), 0.0).astype(BF16)


def _split3(x):
    h = x.astype(BF16)
    r1 = x - h.astype(F32)
    m = r1.astype(BF16)
    l = (r1 - m.astype(F32)).astype(BF16)
    return jnp.concatenate([h, m, l], axis=0)


def _dot_01(sel01, x):
    return _dot(jnp.concatenate([sel01] * 3, axis=1), _split3(x))


class _GdMasks:
    def __init__(self, reverse):
        r = lax.broadcasted_iota(jnp.int32, (CH, CH), 0)
        c = lax.broadcasted_iota(jnp.int32, (CH, CH), 1)
        self.tri = jnp.where((c >= r) if reverse else (c <= r), 1.0, 0.0).astype(BF16)
        t_i = lax.broadcasted_iota(jnp.int32, (CH, HW), 0)
        s_i = lax.broadcasted_iota(jnp.int32, (CH, HW), 1) % HD
        self.later = (t_i < s_i) if reverse else (t_i > s_i)
        self.valid = (s_i >= t_i) if reverse else (s_i <= t_i)
        self.strict = (s_i > t_i) if reverse else (s_i < t_i)
        self.eye = jnp.where(t_i == s_i, 1.0, 0.0)
        self.last = 0 if reverse else CH - 1
        self.off = {}
        m = 1
        while m < CH:
            t_blk, s_blk = t_i // m, s_i // m
            pair = (t_blk // 2) == (s_blk // 2)
            lo_hi = (t_blk % 2 == 0, s_blk % 2 == 1) if reverse else (t_blk % 2 == 1, s_blk % 2 == 0)
            self.off[m] = jnp.logical_and(pair, jnp.logical_and(*lo_hi))
            m *= 2


def _gd_prep(q, k, v, ab, dirs, masks, bd):
    grp = lax.broadcasted_iota(jnp.int32, (CH, HW), 1) // HD

    def widen(a, col0):
        out = jnp.zeros((CH, HW), F32)
        for h in range(NH):
            out = jnp.where(grp == h, jnp.broadcast_to(a[:, col0 + h:col0 + h + 1], (CH, HW)), out)
        return out

    ix = range(len(q))
    mk = [masks[d] for d in dirs]
    la = [widen(ab[j], dirs[j] * NH) for j in ix]
    beta = [widen(ab[j], 2 * NH + dirs[j] * NH) for j in ix]
    gc = [_dot_01(mk[j].tri, la[j]) for j in ix]
    diff = [_dot_01(mk[j].tri, jnp.where(mk[j].later, la[j], 0.0)) for j in ix]
    kb = [_bd4(k[j], bd) for j in ix]
    kq = [_dot_nt(jnp.concatenate([k[j].astype(BF16), q[j].astype(BF16)], axis=0), kb[j]) for j in ix]
    kk = [t[:CH] for t in kq]
    qk = [t[CH:] for t in kq]
    dm = [jnp.where(mk[j].valid, jnp.exp(jnp.minimum(diff[j], 0.0)), 0.0) for j in ix]
    n = [jnp.where(mk[j].strict, beta[j] * kk[j] * dm[j], 0.0) for j in ix]

    tinv = [mk[j].eye - jnp.where(mk[j].off[1], n[j], 0.0) for j in ix]
    m = 2
    while m < CH:
        y = [_dot(tinv[j].astype(BF16), _bd4(jnp.where(mk[j].off[m], n[j], 0.0), bd)) for j in ix]
        tinv = [tinv[j] - _dot(y[j].astype(BF16), _bd4(tinv[j], bd)) for j in ix]
        m *= 2
    tb = [t.astype(BF16) for t in tinv]
    eg = [jnp.exp(g) for g in gc]
    u = [_dot(tb[j], _bd4(v[j] * beta[j], bd)) for j in ix]
    w = [_dot(tb[j], _bd4(k[j] * beta[j] * eg[j], bd)) for j in ix]
    sc = [(qk[j] * dm[j]).astype(BF16) for j in ix]
    qeff = [(q[j] * eg[j] - _dot(sc[j], _bd4(w[j], bd))).astype(BF16) for j in ix]
    oc = [_dot(sc[j], _bd4(u[j], bd)) for j in ix]
    gl = [gc[j][mk[j].last:mk[j].last + 1, :] for j in ix]
    kd = [(k[j] * jnp.exp(gl[j] - gc[j])).astype(BF16) for j in ix]
    kwu = [_dot_tn(kd[j], jnp.concatenate([w[j].astype(BF16), u[j].astype(BF16)], axis=1)) for j in ix]
    kw = [t[:, :HW].astype(BF16) for t in kwu]
    ku = [t[:, HW:] for t in kwu]
    a = [jnp.exp(g) for g in gl]
    return qeff, oc, kw, ku, a


def _gd_body(qf_ref, kf_ref, vf_ref, abf_ref, qb_ref, kb_ref, vb_ref, abb_ref,
             of_ref, ob_ref, sf_scr, sb_scr):
    bd = _bd_mask()
    n_ch = TM // CH
    refs = ((qf_ref, kf_ref, vf_ref, abf_ref), (qb_ref, kb_ref, vb_ref, abb_ref))
    o_refs = (of_ref, ob_ref)
    items = [(d, (n_ch - 1 - j) if d else j) for j in range(n_ch) for d in range(2)]
    rows = [slice(ic * CH, (ic + 1) * CH) for _, ic in items]
    dirs = [d for d, _ in items]
    load = lambda which: [refs[d][which][0, r, :] for d, r in zip(dirs, rows)]
    qeff, oc, kw, ku, a = _gd_prep(load(0), load(1), load(2), load(3), dirs, (_GdMasks(False), _GdMasks(True)), bd)

    s = [sf_scr[...], sb_scr[...]]
    for j, (d, _) in enumerate(items):
        sb = s[d].astype(BF16)
        o_refs[d][0, rows[j], :] = (_dot(qeff[j], sb) + oc[j]).astype(BF16)
        s[d] = jnp.where(bd, s[d] * a[j] - _dot(kw[j], sb) + ku[j], 0.0)
    sf_scr[...] = s[0]
    sb_scr[...] = s[1]


def _scans_kernel(*refs):
    hg_in, gd_in = refs[0:8], refs[8:16]
    hg_out, gd_out = refs[16:18], refs[18:20]
    hg_scr, gd_scr = refs[20:22], refs[22:24]
    _zero_at_row_start(*hg_scr, *gd_scr)
    _hg_body(*hg_in, *hg_out, *hg_scr)
    _gd_body(*gd_in, *gd_out, *gd_scr)


def _scans_call(hq, hv, hk, hlf, gq, gk, gv, gab):
    bsz = hq.shape[0]
    f = lambda b, i: (b, i, 0)
    g0 = lambda b, i: (b, _bwd_tile(i), 0)
    g1 = lambda b, i: (b, _bwd_tile(i), 1)
    blk = lambda m: pl.BlockSpec((1, TM, HW), m)
    abs_ = lambda m: pl.BlockSpec((1, TM, 128), m)
    return pl.pallas_call(
        _scans_kernel,
        grid=(bsz, NT),
        in_specs=[blk(f), blk(f), blk(f), blk(f), blk(g0), blk(g0), blk(g1), blk(g1),
                  blk(f), blk(f), blk(f), abs_(f), blk(g0), blk(g0), blk(g0), abs_(g0)],
        out_specs=[blk(f), blk(g0), blk(f), blk(g0)],
        out_shape=[jax.ShapeDtypeStruct((bsz, T, HW), BF16)] * 4,
        scratch_shapes=[pltpu.VMEM((NH, HD, 128), F32)] * 2 + [pltpu.VMEM((HW, HW), F32)] * 2,
        compiler_params=_cparams(("parallel", "arbitrary")),
    )(hq, hv, hk, hlf, hq, hv, hk, hlf, gq, gk, gv, gab, gq, gk, gv, gab)


def _outproj_kernel(*refs, with_ctx, split):
    if split:
        ctx_ref, refs = refs[0], refs[1:]
    (x_ref, oal_ref, oac_ref, hof_ref, hob_ref, hsg_ref, gof_ref, gob_ref, gsg_ref,
     wo_ref, hng_ref, gng_ref, g1_ref, gate_ref, o_ref) = refs
    ones_bd = _ones_bd()
    oa = oal_ref[0]
    x = x_ref[0]
    if with_ctx:
        oa = jnp.where(pl.program_id(1) == 0, oac_ref[0], oa)
    if split:
        x = jnp.where(pl.program_id(1) == 0, ctx_ref[0], x)

    def finish(of_ref, ob_ref, sg_ref, ng_ref):
        o = of_ref[0].astype(F32) + ob_ref[0].astype(F32)
        ms = _head_sum(o * o, ones_bd) * (1.0 / HD)
        return (o * lax.rsqrt(ms + EPS) * ng_ref[...] * sg_ref[0]).astype(BF16)

    ob = finish(hof_ref, hob_ref, hsg_ref, hng_ref)
    oc = finish(gof_ref, gob_ref, gsg_ref, gng_ref)
    mix = _dot(jnp.concatenate([oa, ob, oc], axis=1), wo_ref[0])
    ms = jnp.mean(mix * mix, axis=-1, keepdims=True)
    o_ref[0] = x + gate_ref[0, 0] * (mix * lax.rsqrt(ms + EPS) * g1_ref[...])


def _outproj_call(xs, oa_lat, oa_ctx, hof, hob, hsg, gof, gob, gsg, wo, layer, hng, gng, g1, gate, with_ctx):
    split = isinstance(xs, tuple)
    assert with_ctx or not split
    bsz = oa_lat.shape[0]
    t0 = 0 if with_ctx else 1
    row = lambda wd: pl.BlockSpec((1, TM, wd), lambda b, i: (b, i + t0, 0))
    if split:
        x_specs = [pl.BlockSpec((1, TM, D), lambda b, i: (b, 0, 0)),
                   pl.BlockSpec((1, TM, D), lambda b, i: (b, jnp.maximum(i - 1, 0), 0))]
        x_args = list(xs)
    else:
        x_specs, x_args = [row(D)], [xs]
    oal = pl.BlockSpec((1, TM, 512), lambda b, i: (b, jnp.maximum(i + t0 - 1, 0), 0))
    oac = pl.BlockSpec((1, TM, 512), lambda b, i: (b, 0, 0))
    mod = pl.BlockSpec((1, 1, 1, D), lambda b, i: (b, _sel(i + t0, 1), 0, 0))
    return pl.pallas_call(
        functools.partial(_outproj_kernel, with_ctx=with_ctx, split=split),
        grid=(bsz, NT - t0),
        in_specs=x_specs + [oal, oac, row(HW), row(HW), row(HW), row(HW), row(HW), row(HW),
                            _layer_spec(wo, layer), _const_spec((1, HW)), _const_spec((1, HW)), _const_spec((1, D)),
                            mod],
        out_specs=pl.BlockSpec((1, TM, D), lambda b, i: (b, i, 0)),
        out_shape=jax.ShapeDtypeStruct((bsz, (NT - t0) * TM, D), F32),
        compiler_params=_cparams(("parallel", "arbitrary")),
    )(*x_args, oa_lat, oa_ctx, hof, hob, hsg, gof, gob, gsg, wo, hng, gng, g1, gate)


def _regroup_rows(x):
    return jnp.swapaxes(x.reshape(8, TM // 8, x.shape[1]), 0, 1).reshape(TM, x.shape[1])


def _ungroup_rows(x):
    return jnp.swapaxes(x.reshape(TM // 8, 8, x.shape[1]), 0, 1).reshape(TM, x.shape[1])


def _ffn_kernel(xp_ref, xm_ref, xn_ref, a_ref, s_ref, wup_ref, cw_ref, cb_ref, wdn_ref, g3_ref, gate_ref,
                o_ref, h_scr, act_scr, *, nct, nt):
    i = pl.program_id(1)
    a, s = a_ref[0, 0], s_ref[0, 0]
    ng = TM // 8
    xg = _regroup_rows(xm_ref[0])
    h_scr[0:TM, :] = _normed(xg, a, s).astype(BF16)
    hp = jnp.where(_prev_ok(i, nct), _normed(xp_ref[0, HALO - 8:HALO, :], a, s), 0.0)
    hn = jnp.where(_next_ok(i, nct, nt), _normed(xn_ref[0, 0:8, :], a, s), 0.0)
    h_scr[TM:TM + 16, :] = jnp.concatenate([hp, hn], axis=0).astype(BF16)
    h = h_scr[...]
    sub = lax.broadcasted_iota(jnp.int32, (8, FF_CW), 0)

    def conv(col0, lo):
        u = _dot(h, wup_ref[0, :, col0 + lo:col0 + lo + FF_CW])
        ur = u[0:TM].reshape(ng, 8, FF_CW)
        first = jnp.where(sub == 0, u[TM + 7:TM + 8], pltpu.roll(ur[ng - 1], 1, 0))
        last = jnp.where(sub == 7, u[TM + 8:TM + 9], pltpu.roll(ur[0], 7, 0))
        w = cw_ref[:, col0 + lo:col0 + lo + FF_CW]
        return (jnp.concatenate([first[None], ur[:-1]], axis=0) * w[0:1] + ur * w[1:2]
                + jnp.concatenate([ur[1:], last[None]], axis=0) * w[2:3] + cb_ref[:, col0 + lo:col0 + lo + FF_CW])

    for cidx in range(D_FF // FF_CW):
        lo = cidx * FF_CW
        act = _silu(conv(0, lo)) * conv(D_FF, lo)
        act_scr[:, lo:lo + FF_CW] = act.reshape(TM, FF_CW).astype(BF16)
    ff = _dot(act_scr[...], wdn_ref[0])
    ms = jnp.mean(ff * ff, axis=-1, keepdims=True)
    o_ref[0] = _ungroup_rows(xg + gate_ref[0, 0] * (ff * lax.rsqrt(ms + EPS) * g3_ref[...]))


def _ffn_call(xs, a_ff, s_ff, wup, cw, cb, wdn, layer, g3, gate, nct):
    bsz, rows, _ = xs.shape
    nt = rows // TM
    mod = _mod_spec(nct)
    return pl.pallas_call(
        functools.partial(_ffn_kernel, nct=nct, nt=nt),
        grid=(bsz, nt),
        in_specs=_halo_specs(D, nct, nt) + [
            mod, mod, _layer_spec(wup, layer), _const_spec((3, 2 * D_FF)), _const_spec((1, 2 * D_FF)),
            _layer_spec(wdn, layer), _const_spec((1, D)), mod],
        out_specs=pl.BlockSpec((1, TM, D), lambda b, i: (b, i, 0)),
        out_shape=jax.ShapeDtypeStruct((bsz, rows, D), F32),
        scratch_shapes=[pltpu.VMEM((TM + 16, D), BF16), pltpu.VMEM((TM, D_FF), BF16)],
        compiler_params=_cparams(("parallel", "arbitrary")),
    )(xs, xs, xs, a_ff, s_ff, wup, cw, cb, wdn, g3, gate)


def _rope_tables():
    n_freq = HD // 4
    inv = ROPE_THETA ** (-jnp.arange(n_freq, dtype=F32) / n_freq)
    rows = jnp.repeat(jnp.arange(SEQ // GRID_W, dtype=F32), GRID_W)
    cols = jnp.tile(jnp.arange(GRID_W, dtype=F32), SEQ // GRID_W)
    ang = jnp.concatenate([rows[:, None] * inv, cols[:, None] * inv], axis=-1)
    cos, sin = jnp.cos(ang), jnp.sin(ang)
    cos_l = jnp.tile(jnp.concatenate([cos, cos], axis=-1), (1, 2))
    sin_l = jnp.tile(jnp.concatenate([-sin, sin], axis=-1), (1, 2))
    cos_t = jnp.concatenate([jnp.ones((CTX, 128), F32), cos_l], axis=0)
    sin_t = jnp.concatenate([jnp.zeros((CTX, 128), F32), sin_l], axis=0)
    return cos_t, sin_t


def kernel(x, c, ctx, c_ctx, ada_w, ada_b, norm_g, w_in, w_out, da_lambda, da_subln_g, hg_lb_logits, hg_norm_g,
           gd_conv_w, gd_a_log, gd_dt_bias, gd_norm_g, ffn_w_up, ffn_conv_w, ffn_conv_b, ffn_w_down):
    bsz = x.shape[0]
    depth = ada_w.shape[0]
    assert x.shape == (bsz, SEQ, D) and ctx.shape == (bsz, CTX, D) and bsz <= 8
    cos_t, sin_t = _rope_tables()

    cond = jnp.concatenate([jax.nn.silu(c.astype(F32)), jnp.zeros((8 - bsz, D), F32),
                            jax.nn.silu(c_ctx.astype(F32))[None], jnp.zeros((7, D), F32)], axis=0)
    mods = _ada_call(cond, ada_w, ada_b).reshape(depth, 16, 6, D)

    mod = jnp.stack([jnp.broadcast_to(mods[:, 8:9], (depth, bsz, 6, D)), mods[:, :bsz]], axis=2)
    g = norm_g.astype(F32)[:, None, None]
    a_in, s_in = g[..., 0:1, :] * (1.0 + mod[..., 1:2, :]), mod[..., 0:1, :]
    a_ff, s_ff = g[..., 2:3, :] * (1.0 + mod[..., 4:5, :]), mod[..., 3:4, :]
    gate1, gate2 = mod[..., 2:3, :], mod[..., 5:6, :]
    g1, g3 = norm_g[:, 1:2].astype(F32), norm_g[:, 3:4].astype(F32)
    lb_w = jax.nn.softmax(hg_lb_logits.astype(F32), axis=0)
    lb = (jnp.cumsum(lb_w, axis=0) - lb_w[0]).reshape(depth, 1, 2 * HW)
    lb1m, lbm = 1.0 - lb, jnp.maximum(lb, LB_FLOOR)
    lane_pad = ((0, 0), (0, 0), (0, 128 - 2 * NH))
    nega = jnp.pad(-jnp.exp(gd_a_log.astype(F32)).reshape(depth, 1, 2 * NH), lane_pad)
    dtb = jnp.pad(gd_dt_bias.astype(F32).reshape(depth, 1, 2 * NH), lane_pad)
    lam_init = jnp.asarray([0.8 - 0.6 * math.exp(-0.3 * layer) for layer in range(depth)], F32)
    lp = da_lambda.astype(F32)
    lam = jnp.exp(jnp.sum(lp[:, 0] * lp[:, 1], axis=-1)) - jnp.exp(jnp.sum(lp[:, 2] * lp[:, 3], axis=-1)) + lam_init
    lam_arr = jnp.broadcast_to(lam[:, None, None], (depth, 8, 128))
    g_arr = (da_subln_g.astype(F32) * (1.0 - lam_init)[:, None]).reshape(depth, 1, DV)
    hng = jnp.tile(hg_norm_g.astype(F32), (1, NH)).reshape(depth, 1, HW)
    gng = jnp.tile(gd_norm_g.astype(F32), (1, NH)).reshape(depth, 1, HW)
    conv_g, conv_f = gd_conv_w.astype(F32), ffn_conv_w.astype(F32)
    conv_fb = ffn_conv_b.astype(F32).reshape(depth, 1, 2 * D_FF)
    wo_b, wup_b, wdn_b = w_out.astype(BF16), ffn_w_up.astype(BF16), ffn_w_down.astype(BF16)
    w_in_t = jnp.swapaxes(w_in.astype(F32), 1, 2)

    xs = (ctx.astype(F32), x.astype(F32))
    for layer in range(depth):
        need_ctx = layer < depth - 1
        consts = [lb1m[layer], lbm[layer], conv_g[layer], nega[layer], dtb[layer]]
        (q, k, v, hq, hv, hk, hlf, hsg, gq, gk, gv, gab, gsg) = _inproj_call(
            xs, a_in[layer], s_in[layer], cos_t, sin_t, w_in_t, layer, consts)

        oa_lat = _attn_call(q, k, v, lam_arr[layer], g_arr[layer], CTX, SEQ, T)
        oa_ctx = _attn_call(q, k, v, lam_arr[layer], g_arr[layer], 0, CTX, CTX) if need_ctx else oa_lat

        hof, hob, gof, gob = _scans_call(hq, hv, hk, hlf, gq, gk, gv, gab)

        xs = _outproj_call(xs, oa_lat, oa_ctx, hof, hob, hsg, gof, gob, gsg, wo_b, layer,
                           hng[layer], gng[layer], g1[layer], gate1[layer], need_ctx)
        xs = _ffn_call(xs, a_ff[layer], s_ff[layer], wup_b, conv_f[layer], conv_fb[layer], wdn_b, layer,
                       g3[layer], gate2[layer], 1 if need_ctx else 0)
    return xs
```

```python
import functools
import math

import jax
import jax.numpy as jnp
import numpy as np
from jax import lax
from jax.experimental import pallas as pl
from jax.experimental.pallas import tpu as pltpu

F32 = jnp.float32
BF16 = jnp.bfloat16

D = 1024
CTX = 256
SEQ = 2048
T = CTX + SEQ
GRID_W = 64
ROPE_THETA = 10000.0
EPS = 1e-6
LB_FLOOR = 1e-30
NH = 4
HD = 64
DV = 128
HW = NH * HD
D_FF = 2816
TM = 256
NT = T // TM
HALO = 16
TQ = 256
ATT_QB = 256
ATT_QIN = 4
Q_SCALE = HD ** -0.5 * math.log2(math.e)
BLK = 16
HG_NV = TM // 8
assert HG_NV == 2 * BLK
HG_SB = 2 * BLK
CH = 64
FF_CW = 256
VMEM_LIMIT = 56 * 1024 * 1024

C_DAQ, C_DAK, C_DAV = 0, 512, 1024
C_HGQ, C_HGI, C_HGF, C_HGG = 1536, 1792, 2048, 2560
C_GDQKV, C_GDA, C_GDB, C_GDG = 2816, 3584, 3592, 3600
IN_COLS = 3856


def _cparams(sem):
    return pltpu.CompilerParams(dimension_semantics=sem, vmem_limit_bytes=VMEM_LIMIT)


def _const_spec(shape):
    n = len(shape)
    return pl.BlockSpec(shape, lambda *_: (0,) * n)


def _layer_spec(stacked, layer):
    n = stacked.ndim - 1
    return pl.BlockSpec((1,) + stacked.shape[1:], lambda *_: (layer,) + (0,) * n)


def _sigmoid(x):
    return 1.0 / (1.0 + jnp.exp(-x))


def _silu(x):
    return x * _sigmoid(x)


def _softplus(x):
    return jnp.maximum(x, 0.0) + jnp.log(1.0 + jnp.exp(-jnp.abs(x)))


def _dot(a, b):
    return jnp.dot(a, b, preferred_element_type=F32)


def _dot_nt(a, b):
    return lax.dot_general(a, b, (((1,), (1,)), ((), ())), preferred_element_type=F32)


def _dot_tn(a, b):
    return lax.dot_general(a, b, (((0,), (0,)), ((), ())), preferred_element_type=F32)


def _dot_hi(a, b):
    return jnp.dot(a, b, preferred_element_type=F32, precision=lax.Precision.HIGHEST)


def _head_sum(x, ones_bd):
    return _dot(x.astype(BF16), ones_bd)


def _ones_bd():
    r = lax.broadcasted_iota(jnp.int32, (HW, HW), 0) // HD
    c = lax.broadcasted_iota(jnp.int32, (HW, HW), 1) // HD
    return jnp.where(r == c, 1.0, 0.0).astype(BF16)


def _bd_mask():
    r = lax.broadcasted_iota(jnp.int32, (HW, HW), 0) // HD
    c = lax.broadcasted_iota(jnp.int32, (HW, HW), 1) // HD
    return r == c


def _normed(x, a, s):
    ms = jnp.mean(x * x, axis=-1, keepdims=True)
    return (x * lax.rsqrt(ms + EPS)) * a + s


def _ada_kernel(c_ref, w_ref, b_ref, o_ref):
    o_ref[0] = _dot(c_ref[...].astype(BF16), w_ref[0].astype(BF16)) + b_ref[0]


def _ada_call(cond, ada_w, ada_b):
    depth = ada_w.shape[0]
    nc = 6 * D
    cw = 1536
    return pl.pallas_call(
        _ada_kernel,
        grid=(depth, nc // cw),
        in_specs=[pl.BlockSpec((16, D), lambda l, j: (0, 0)),
                  pl.BlockSpec((1, D, cw), lambda l, j: (l, 0, j)),
                  pl.BlockSpec((1, 1, cw), lambda l, j: (l, 0, j))],
        out_specs=pl.BlockSpec((1, 16, cw), lambda l, j: (l, 0, j)),
        out_shape=jax.ShapeDtypeStruct((depth, 16, nc), F32),
        compiler_params=_cparams(("arbitrary", "arbitrary")),
    )(cond, ada_w, ada_b.reshape(depth, 1, nc))


def _sel(i, nct):
    return jnp.where(i >= nct, 1, 0)


def _prev_ok(i, nct):
    return i > nct


def _next_ok(i, nct, nt):
    return jnp.logical_and(i >= nct, i <= nt - 2)


def _halo_specs(width, nct, nt):
    per = TM // HALO
    return [pl.BlockSpec((1, HALO, width), lambda b, i: (b, i * per - jnp.where(_prev_ok(i, nct), 1, 0), 0)),
            pl.BlockSpec((1, TM, width), lambda b, i: (b, i, 0)),
            pl.BlockSpec((1, HALO, width), lambda b, i: (b, (i + 1) * per - jnp.where(_next_ok(i, nct, nt), 0, 1), 0))]


def _mod_spec(nct):
    return pl.BlockSpec((1, 1, 1, D), lambda b, i: (b, _sel(i, nct), 0, 0))


def _fill_h(h_scr, xp, xm, xn, a, s, i, nct, nt):
    hp = _normed(xp, a, s)
    hn = _normed(xn, a, s)
    h_scr[0:HALO, :] = jnp.where(_prev_ok(i, nct), hp, 0.0).astype(BF16)
    h_scr[HALO:HALO + TM, :] = _normed(xm, a, s).astype(BF16)
    h_scr[HALO + TM:, :] = jnp.where(_next_ok(i, nct, nt), hn, 0.0).astype(BF16)


def _conv3(u_scr, w_ref):
    return (u_scr[HALO - 1:HALO - 1 + TM, :] * w_ref[0:1, :]
            + u_scr[HALO:HALO + TM, :] * w_ref[1:2, :]
            + u_scr[HALO + 1:HALO + 1 + TM, :] * w_ref[2:3, :])


def _inproj_kernel(*refs, split):
    if split:
        ctx_ref, refs = refs[0], refs[1:]
    (xp_ref, xm_ref, xn_ref, a_ref, s_ref, cos_ref, sin_ref, win_ref,
     lb1m_ref, lbm_ref, gconv_ref, nega_ref, dtb_ref,
     q_ref, k_ref, v_ref, hq_ref, hv_ref, hk_ref, hlf_ref, hsg_ref,
     gq_ref, gk_ref, gv_ref, gab_ref, gsg_ref,
     h_scr, u_scr, wqk_ref, wv_ref, whg_ref, wgq_ref, wab_ref, wgg_ref) = refs
    i = pl.program_id(1)

    @pl.when(jnp.logical_and(pl.program_id(0) == 0, i == 0))
    def _():
        wqk_ref[...] = win_ref[0, C_DAQ:C_DAV, :].T.astype(BF16)
        wv_ref[...] = win_ref[0, C_DAV:C_HGQ, :].T.astype(BF16)
        whg_ref[...] = win_ref[0, C_HGQ:C_GDQKV, :].T.astype(BF16)
        wgq_ref[...] = win_ref[0, C_GDQKV:C_GDA, :].T.astype(BF16)
        ab_lane = lax.broadcasted_iota(jnp.int32, (D, 128), 1)
        wab_ref[...] = jnp.where(ab_lane < 4 * NH, win_ref[0, C_GDA:C_GDA + 128, :].T, 0.0).astype(BF16)
        wgg_ref[...] = win_ref[0, C_GDG:IN_COLS, :].T.astype(BF16)

    xm = jnp.where(i == 0, ctx_ref[0], xm_ref[0]) if split else xm_ref[0]
    _fill_h(h_scr, xp_ref[0], xm, xn_ref[0], a_ref[0, 0], s_ref[0, 0], i, 1, NT)
    h = h_scr[HALO:HALO + TM, :]

    u_scr[...] = _dot(h_scr[...], wgq_ref[...])
    z_hg = _dot(h, whg_ref[...])
    z_qk = _dot(h, wqk_ref[...])
    z_v = _dot(h, wv_ref[...])
    z_ab = _dot(h, wab_ref[...])
    z_gg = _dot(h, wgg_ref[...])

    lane = lax.broadcasted_iota(jnp.int32, (TM, 128), 1)
    first_half = (lane % HD) < (HD // 2)
    cs, sn = cos_ref[...], sin_ref[...]
    for j in range(8):
        xj = z_qk[:, j * 128:(j + 1) * 128]
        sw = jnp.where(first_half, pltpu.roll(xj, 128 - HD // 2, 1), pltpu.roll(xj, HD // 2, 1))
        r = xj * cs + sw * sn
        if j < 4:
            q_ref[0, :, j * 128:(j + 1) * 128] = (r * Q_SCALE).astype(BF16)
        else:
            k_ref[0, :, (j - 4) * 128:(j - 3) * 128] = r.astype(BF16)
    v_ref[0] = z_v.astype(BF16)

    z = z_hg
    hq_ref[0] = _silu(z[:, 0:HW]).astype(BF16)
    hv_ref[0] = z[:, HW:2 * HW].astype(BF16)
    sg = _sigmoid(z[:, 2 * HW:4 * HW])
    hk_ref[0] = (lb1m_ref[...] * (1.0 - sg)).astype(BF16)
    hlf_ref[0] = jnp.log(lbm_ref[...] + lb1m_ref[...] * sg)
    hsg_ref[0] = _silu(z[:, 4 * HW:5 * HW]).astype(BF16)

    y = _silu(_conv3(u_scr, gconv_ref))
    ones_bd = _ones_bd()
    qg, kg = y[:, 0:HW], y[:, HW:2 * HW]
    gq_ref[0] = (qg * lax.rsqrt(_head_sum(qg * qg, ones_bd) + EPS) * (HD ** -0.5)).astype(BF16)
    gk_ref[0] = (kg * lax.rsqrt(_head_sum(kg * kg, ones_bd) + EPS)).astype(BF16)
    gv_ref[0] = y[:, 2 * HW:3 * HW].astype(BF16)
    gab_ref[0] = jnp.where(lane < 2 * NH, nega_ref[...] * _softplus(z_ab + dtb_ref[...]), _sigmoid(z_ab))
    gsg_ref[0] = _silu(z_gg).astype(BF16)


def _split_x_specs():
    per = TM // HALO
    lat = lambda i: jnp.maximum(i - 1, 0)
    return [pl.BlockSpec((1, TM, D), lambda b, i: (b, 0, 0)),
            pl.BlockSpec((1, HALO, D), lambda b, i: (b, lat(i) * per - jnp.where(_prev_ok(i, 1), 1, 0), 0)),
            pl.BlockSpec((1, TM, D), lambda b, i: (b, lat(i), 0)),
            pl.BlockSpec((1, HALO, D), lambda b, i: (b, (lat(i) + 1) * per - jnp.where(_next_ok(i, 1, NT), 0, 1), 0))]


def _inproj_call(xs, a_in, s_in, cos_t, sin_t, w_in, layer, consts):
    split = isinstance(xs, tuple)
    w_spec = pl.BlockSpec((1, IN_COLS, D), lambda b, i: (layer, 0, 0))
    w_groups = [C_DAV - C_DAQ, C_HGQ - C_DAV, C_GDQKV - C_HGQ, C_GDA - C_GDQKV, 128, IN_COLS - C_GDG]
    x_args = (xs[0], xs[1], xs[1], xs[1]) if split else (xs, xs, xs)
    bsz = x_args[0].shape[0]
    row = lambda wd: pl.BlockSpec((1, TM, wd), lambda b, i: (b, i, 0))
    tab = pl.BlockSpec((TM, 128), lambda b, i: (i, 0))
    out_w = [(512, BF16), (512, BF16), (512, BF16), (HW, BF16), (HW, BF16), (2 * HW, BF16), (2 * HW, F32),
             (HW, BF16), (HW, BF16), (HW, BF16), (HW, BF16), (128, F32), (HW, BF16)]
    return pl.pallas_call(
        functools.partial(_inproj_kernel, split=split),
        grid=(bsz, NT),
        in_specs=(_split_x_specs() if split else _halo_specs(D, 1, NT)) + [_mod_spec(1), _mod_spec(1), tab, tab]
        + [w_spec] + [_const_spec(x.shape) for x in consts],
        out_specs=[row(wd) for wd, _ in out_w],
        out_shape=[jax.ShapeDtypeStruct((bsz, T, wd), dt) for wd, dt in out_w],
        scratch_shapes=[pltpu.VMEM((TM + 2 * HALO, D), BF16), pltpu.VMEM((TM + 2 * HALO, 3 * HW), F32)]
        + [pltpu.VMEM((D, wd), BF16) for wd in w_groups],
        compiler_params=_cparams(("arbitrary", "arbitrary")),
    )(*x_args, a_in, s_in, cos_t, sin_t, w_in, *consts)


def _attn_kernel(*refs, n_qin):
    q_refs = refs[:n_qin]
    k_ref, v_ref, lam_ref, g_ref, o_ref = refs[n_qin:]
    k, v = k_ref[0], v_ref[0]
    lane = lax.broadcasted_iota(jnp.int32, (TQ, 128), 1)
    zero = jnp.zeros((TQ, 128), BF16)
    per = ATT_QB // TQ
    ix = range(n_qin * per)
    q = [q_refs[j // per][0, (j % per) * TQ:(j % per + 1) * TQ, :] for j in ix]
    qs = [jnp.concatenate([jnp.where(lane < HD, q[j], zero), jnp.where(lane >= HD, q[j], zero)], axis=0) for j in ix]
    st = [_dot_nt(k, qs[j]) for j in ix]
    m = [jnp.max(st[j], axis=0, keepdims=True) for j in ix]
    p = [jnp.exp2(st[j] - m[j]) for j in ix]
    l = [jnp.sum(p[j], axis=0, keepdims=True) for j in ix]
    ot = [_dot_tn(v, p[j].astype(BF16)) * (1.0 / l[j]) for j in ix]
    for j in ix:
        d = ot[j][:, :TQ] - lam_ref[0:1, :] * ot[j][:, TQ:]
        ms = jnp.mean(d * d, axis=0, keepdims=True)
        dn = d * lax.rsqrt(ms + EPS)
        o_ref[0, j * TQ:(j + 1) * TQ, :] = (dn.T * g_ref[...]).astype(BF16)


def _attn_call(q, k, v, lam_arr, g_arr, q_row0, n_rows, tk):
    bsz = q.shape[0]
    n_qin = min(ATT_QIN, n_rows // ATT_QB)
    tq = n_qin * ATT_QB
    assert q_row0 % ATT_QB == 0 and n_rows % tq == 0
    qoff = q_row0 // ATT_QB

    def q_spec(j):
        return pl.BlockSpec((1, ATT_QB, 128), lambda b, h, i: (b, qoff + i * n_qin + j, h))

    return pl.pallas_call(
        functools.partial(_attn_kernel, n_qin=n_qin),
        grid=(bsz, NH, n_rows // tq),
        in_specs=[q_spec(j) for j in range(n_qin)]
        + [pl.BlockSpec((1, tk, 128), lambda b, h, i: (b, 0, h)),
           pl.BlockSpec((1, tk, 128), lambda b, h, i: (b, 0, h)),
           _const_spec((8, TQ)), _const_spec((1, DV))],
        out_specs=pl.BlockSpec((1, tq, 128), lambda b, h, i: (b, i, h)),
        out_shape=jax.ShapeDtypeStruct((bsz, n_rows, 512), BF16),
        compiler_params=_cparams(("parallel", "parallel", "arbitrary")),
    )(*([q] * n_qin), k, v, lam_arr, g_arr)


def _bwd_tile(i):
    return jnp.where(i == 0, 0, NT - i)


def _zero_at_row_start(*scratch):
    @pl.when(pl.program_id(1) == 0)
    def _():
        for s in scratch:
            s[...] = jnp.zeros_like(s)


def _hg_body(qf_ref, vf_ref, kf_ref, lf_ref, qb_ref, vb_ref, kb_ref, lb_ref,
             of_ref, ob_ref, sf_scr, sb_scr):
    q = (qf_ref[0].astype(F32), qb_ref[0].astype(F32))
    k = (kf_ref[0].astype(F32), kb_ref[0].astype(F32))
    v = (vf_ref[0].astype(F32), vb_ref[0].astype(F32))
    logf = (lf_ref[0], lb_ref[0])
    o_refs = (of_ref, ob_ref)
    dirs = (0, 1)

    r = lax.broadcasted_iota(jnp.int32, (TM, TM), 0)
    c = lax.broadcasted_iota(jnp.int32, (TM, TM), 1)
    same = (r // BLK) == (c // BLK)
    tri = [jnp.where(jnp.logical_and(same, (c >= r) if d else (c <= r)), 1.0, 0.0).astype(BF16) for d in dirs]
    blk = jnp.where(same, 1.0, 0.0).astype(BF16)
    pair_of = jnp.where((r // HG_SB) == (c // HG_SB), 1.0, 0.0).astype(BF16)
    bl = [_dot_01(tri[d], logf[d]) for d in dirs]
    tot = [_dot_01(blk, logf[d]) for d in dirs]
    tot2 = [_dot_01(pair_of, logf[d]) for d in dirs]
    qd16 = [q[d] * jnp.exp(bl[d]) for d in dirs]
    kd16 = [k[d] * jnp.exp(tot[d] - bl[d]) for d in dirs]
    f = [jnp.exp(logf[d]) for d in dirs]
    half = (lax.broadcasted_iota(jnp.int32, (TM, HW), 0) // BLK) % 2
    second = [half == (0 if d else 1) for d in dirs]
    e_other = [jnp.exp(tot2[d] - tot[d]) for d in dirs]
    qdb = [(qd16[d] * jnp.where(second[d], e_other[d], 1.0)).astype(BF16) for d in dirs]
    kd = [(kd16[d] * jnp.where(second[d], 1.0, e_other[d])).astype(BF16) for d in dirs]
    e_blk = [jnp.exp(tot2[d]) for d in dirs]

    def regroup(x):
        return jnp.swapaxes(x.reshape(8, HG_NV, HW), 0, 1).reshape(2, BLK, 8, HW)

    q4 = [regroup(q[d]) for d in dirs]
    k4 = [regroup(k[d]) for d in dirs]
    v4 = [regroup(v[d]) for d in dirs]
    f4 = [regroup(f[d]) for d in dirs]
    ones_bd = _ones_bd()
    e = [None, None]
    o4 = [None, None]
    for n in range(BLK):
        ln = BLK - n
        for d in dirs:
            qs, ks = (slice(0, ln), slice(n, BLK)) if d else (slice(n, BLK), slice(0, ln))
            if n == 0:
                pn = q4[d] * k4[d]
            else:
                fs = slice(n - 1, n - 1 + ln) if d else slice(1, 1 + ln)
                e[d] = f4[d][:, fs] if n == 1 else (e[d][:, :ln] if d else e[d][:, 1:]) * f4[d][:, fs]
                pn = q4[d][:, qs] * k4[d][:, ks] * e[d]
            a = _dot(pn.reshape(2 * ln * 8, HW).astype(BF16), ones_bd).reshape(2, ln, 8, HW) * v4[d][:, ks]
            if n == 0:
                o4[d] = a
            else:
                pad = jnp.zeros((2, n, 8, HW), F32)
                o4[d] = o4[d] + jnp.concatenate([a, pad] if d else [pad, a], axis=1)
    o_band = [jnp.swapaxes(o4[d].reshape(HG_NV, 8, HW), 0, 1).reshape(TM, HW) for d in dirs]

    n_sb = TM // HG_SB
    nq = n_sb * BLK
    hr = lax.broadcasted_iota(jnp.int32, (NH * nq, HW), 0) // nq
    hl = lax.broadcasted_iota(jnp.int32, (NH * nq, HW), 1) // HD
    same_head = hr == hl
    pr = lax.broadcasted_iota(jnp.int32, (nq, NH * nq), 0) // BLK
    pc = (lax.broadcasted_iota(jnp.int32, (nq, NH * nq), 1) % nq) // BLK
    same_step = pr == pc
    zero_b = jnp.zeros((), BF16)
    for d in dirs:
        def blocks(x, which):
            return x.reshape(n_sb, 2, BLK, HW)[:, which if d == 0 else 1 - which].reshape(nq, HW)

        k1 = blocks(kd16[d], 0).astype(BF16)
        v1 = blocks(v[d], 0).astype(BF16)
        q2 = blocks(qd16[d], 1).astype(BF16)
        sc = _dot_nt(q2, jnp.where(same_head, jnp.concatenate([k1] * NH, axis=0), zero_b))
        sc = jnp.where(same_step, sc, 0.0).astype(BF16)
        o2 = _dot(sc, jnp.where(same_head, jnp.concatenate([v1] * NH, axis=0), zero_b)).reshape(n_sb, BLK, HW)
        ob4 = o_band[d].reshape(n_sb, 2, BLK, HW)
        parts = [ob4[:, 0], ob4[:, 1]]
        parts[1 if d == 0 else 0] = parts[1 if d == 0 else 0] + o2
        o_band[d] = jnp.stack(parts, axis=1).reshape(TM, HW)

    lane = lax.broadcasted_iota(jnp.int32, (HD, 128), 1)
    own = [(lane >= HD) if h % 2 else (lane < HD) for h in range(NH)]
    tile = [slice(128 * (h // 2), 128 * (h // 2) + 128) for h in range(NH)]
    zero_t = jnp.zeros((HD, 128), BF16)
    vb = [v[d].astype(BF16) for d in dirs]
    nb = n_sb
    s_scr = (sf_scr, sb_scr)
    s = [[s_scr[d][h] for h in range(NH)] for d in dirs]
    for j in range(nb):
        for d in dirs:
            ib = nb - 1 - j if d else j
            rows = slice(ib * HG_SB, (ib + 1) * HG_SB)
            s_bd = jnp.concatenate(
                [jnp.concatenate([s[d][h].astype(BF16) if c == h // 2 else zero_t for c in range(HW // 128)], axis=1)
                 for h in range(NH)], axis=0)
            o_refs[d][0, rows, :] = (o_band[d][rows] + _dot_nt(qdb[d][rows], s_bd)).astype(BF16)
            u = _dot_tn(vb[d][rows], kd[d][rows])
            e_row = e_blk[d][ib * HG_SB:ib * HG_SB + 1, :]
            for h in range(NH):
                s[d][h] = s[d][h] * e_row[:, tile[h]] + jnp.where(own[h], u[h * HD:(h + 1) * HD, tile[h]], 0.0)
    for d in dirs:
        for h in range(NH):
            s_scr[d][h] = s[d][h]


def _bd4(x, bd):
    return jnp.where(bd, jnp.concatenate([x] * NH, axis=0), 0.0).astype(BF16)


def _split3(x):
    h = x.astype(BF16)
    r1 = x - h.astype(F32)
    m = r1.astype(BF16)
    l = (r1 - m.astype(F32)).astype(BF16)
    return jnp.concatenate([h, m, l], axis=0)


def _dot_01(sel01, x):
    return _dot(jnp.concatenate([sel01] * 3, axis=1), _split3(x))


class _GdMasks:
    def __init__(self, reverse):
        r = lax.broadcasted_iota(jnp.int32, (CH, CH), 0)
        c = lax.broadcasted_iota(jnp.int32, (CH, CH), 1)
        self.tri = jnp.where((c >= r) if reverse else (c <= r), 1.0, 0.0).astype(BF16)
        t_i = lax.broadcasted_iota(jnp.int32, (CH, HW), 0)
        s_i = lax.broadcasted_iota(jnp.int32, (CH, HW), 1) % HD
        self.later = (t_i < s_i) if reverse else (t_i > s_i)
        self.valid = (s_i >= t_i) if reverse else (s_i <= t_i)
        self.strict = (s_i > t_i) if reverse else (s_i < t_i)
        self.eye = jnp.where(t_i == s_i, 1.0, 0.0)
        self.last = 0 if reverse else CH - 1
        self.off = {}
        m = 1
        while m < CH:
            t_blk, s_blk = t_i // m, s_i // m
            pair = (t_blk // 2) == (s_blk // 2)
            lo_hi = (t_blk % 2 == 0, s_blk % 2 == 1) if reverse else (t_blk % 2 == 1, s_blk % 2 == 0)
            self.off[m] = jnp.logical_and(pair, jnp.logical_and(*lo_hi))
            m *= 2


def _gd_prep(q, k, v, ab, dirs, masks, bd):
    grp = lax.broadcasted_iota(jnp.int32, (CH, HW), 1) // HD

    def widen(a, col0):
        out = jnp.zeros((CH, HW), F32)
        for h in range(NH):
            out = jnp.where(grp == h, jnp.broadcast_to(a[:, col0 + h:col0 + h + 1], (CH, HW)), out)
        return out

    ix = range(len(q))
    mk = [masks[d] for d in dirs]
    la = [widen(ab[j], dirs[j] * NH) for j in ix]
    beta = [widen(ab[j], 2 * NH + dirs[j] * NH) for j in ix]
    gc = [_dot_01(mk[j].tri, la[j]) for j in ix]
    diff = [_dot_01(mk[j].tri, jnp.where(mk[j].later, la[j], 0.0)) for j in ix]
    kb = [_bd4(k[j], bd) for j in ix]
    kq = [_dot_nt(jnp.concatenate([k[j].astype(BF16), q[j].astype(BF16)], axis=0), kb[j]) for j in ix]
    kk = [t[:CH] for t in kq]
    qk = [t[CH:] for t in kq]
    dm = [jnp.where(mk[j].valid, jnp.exp(jnp.minimum(diff[j], 0.0)), 0.0) for j in ix]
    n = [jnp.where(mk[j].strict, beta[j] * kk[j] * dm[j], 0.0) for j in ix]

    tinv = [mk[j].eye - jnp.where(mk[j].off[1], n[j], 0.0) for j in ix]
    m = 2
    while m < CH:
        y = [_dot(tinv[j].astype(BF16), _bd4(jnp.where(mk[j].off[m], n[j], 0.0), bd)) for j in ix]
        tinv = [tinv[j] - _dot(y[j].astype(BF16), _bd4(tinv[j], bd)) for j in ix]
        m *= 2
    tb = [t.astype(BF16) for t in tinv]
    eg = [jnp.exp(g) for g in gc]
    u = [_dot(tb[j], _bd4(v[j] * beta[j], bd)) for j in ix]
    w = [_dot(tb[j], _bd4(k[j] * beta[j] * eg[j], bd)) for j in ix]
    sc = [(qk[j] * dm[j]).astype(BF16) for j in ix]
    qeff = [(q[j] * eg[j] - _dot(sc[j], _bd4(w[j], bd))).astype(BF16) for j in ix]
    oc = [_dot(sc[j], _bd4(u[j], bd)) for j in ix]
    gl = [gc[j][mk[j].last:mk[j].last + 1, :] for j in ix]
    kd = [(k[j] * jnp.exp(gl[j] - gc[j])).astype(BF16) for j in ix]
    kwu = [_dot_tn(kd[j], jnp.concatenate([w[j].astype(BF16), u[j].astype(BF16)], axis=1)) for j in ix]
    kw = [t[:, :HW].astype(BF16) for t in kwu]
    ku = [t[:, HW:] for t in kwu]
    a = [jnp.exp(g) for g in gl]
    return qeff, oc, kw, ku, a


def _gd_body(qf_ref, kf_ref, vf_ref, abf_ref, qb_ref, kb_ref, vb_ref, abb_ref,
             of_ref, ob_ref, sf_scr, sb_scr):
    bd = _bd_mask()
    n_ch = TM // CH
    refs = ((qf_ref, kf_ref, vf_ref, abf_ref), (qb_ref, kb_ref, vb_ref, abb_ref))
    o_refs = (of_ref, ob_ref)
    items = [(d, (n_ch - 1 - j) if d else j) for j in range(n_ch) for d in range(2)]
    rows = [slice(ic * CH, (ic + 1) * CH) for _, ic in items]
    dirs = [d for d, _ in items]
    load = lambda which: [refs[d][which][0, r, :] for d, r in zip(dirs, rows)]
    qeff, oc, kw, ku, a = _gd_prep(load(0), load(1), load(2), load(3), dirs, (_GdMasks(False), _GdMasks(True)), bd)

    s = [sf_scr[...], sb_scr[...]]
    for j, (d, _) in enumerate(items):
        sb = s[d].astype(BF16)
        o_refs[d][0, rows[j], :] = (_dot(qeff[j], sb) + oc[j]).astype(BF16)
        s[d] = jnp.where(bd, s[d] * a[j] - _dot(kw[j], sb) + ku[j], 0.0)
    sf_scr[...] = s[0]
    sb_scr[...] = s[1]


def _scans_kernel(*refs):
    hg_in, gd_in = refs[0:8], refs[8:16]
    hg_out, gd_out = refs[16:18], refs[18:20]
    hg_scr, gd_scr = refs[20:22], refs[22:24]
    _zero_at_row_start(*hg_scr, *gd_scr)
    _hg_body(*hg_in, *hg_out, *hg_scr)
    _gd_body(*gd_in, *gd_out, *gd_scr)


def _scans_call(hq, hv, hk, hlf, gq, gk, gv, gab):
    bsz = hq.shape[0]
    f = lambda b, i: (b, i, 0)
    g0 = lambda b, i: (b, _bwd_tile(i), 0)
    g1 = lambda b, i: (b, _bwd_tile(i), 1)
    blk = lambda m: pl.BlockSpec((1, TM, HW), m)
    abs_ = lambda m: pl.BlockSpec((1, TM, 128), m)
    return pl.pallas_call(
        _scans_kernel,
        grid=(bsz, NT),
        in_specs=[blk(f), blk(f), blk(f), blk(f), blk(g0), blk(g0), blk(g1), blk(g1),
                  blk(f), blk(f), blk(f), abs_(f), blk(g0), blk(g0), blk(g0), abs_(g0)],
        out_specs=[blk(f), blk(g0), blk(f), blk(g0)],
        out_shape=[jax.ShapeDtypeStruct((bsz, T, HW), BF16)] * 4,
        scratch_shapes=[pltpu.VMEM((NH, HD, 128), F32)] * 2 + [pltpu.VMEM((HW, HW), F32)] * 2,
        compiler_params=_cparams(("parallel", "arbitrary")),
    )(hq, hv, hk, hlf, hq, hv, hk, hlf, gq, gk, gv, gab, gq, gk, gv, gab)


def _outproj_kernel(*refs, with_ctx, split):
    if split:
        ctx_ref, refs = refs[0], refs[1:]
    (x_ref, oal_ref, oac_ref, hof_ref, hob_ref, hsg_ref, gof_ref, gob_ref, gsg_ref,
     wo_ref, hng_ref, gng_ref, gate_ref, o_ref) = refs
    ones_bd = _ones_bd()
    oa = oal_ref[0]
    x = x_ref[0]
    if with_ctx:
        oa = jnp.where(pl.program_id(1) == 0, oac_ref[0], oa)
    if split:
        x = jnp.where(pl.program_id(1) == 0, ctx_ref[0], x)

    def finish(of_ref, ob_ref, sg_ref, ng_ref):
        o = of_ref[0].astype(F32) + ob_ref[0].astype(F32)
        ms = _head_sum(o * o, ones_bd) * (1.0 / HD)
        return (o * lax.rsqrt(ms + EPS) * ng_ref[...] * sg_ref[0]).astype(BF16)

    ob = finish(hof_ref, hob_ref, hsg_ref, hng_ref)
    oc = finish(gof_ref, gob_ref, gsg_ref, gng_ref)
    mix = _dot(jnp.concatenate([oa, ob, oc], axis=1), wo_ref[0])
    ms = jnp.mean(mix * mix, axis=-1, keepdims=True)
    o_ref[0] = x + gate_ref[0, 0] * (mix * lax.rsqrt(ms + EPS))


def _outproj_call(xs, oa_lat, oa_ctx, hof, hob, hsg, gof, gob, gsg, wo, layer, hng, gng, gate, with_ctx):
    split = isinstance(xs, tuple)
    assert with_ctx or not split
    bsz = oa_lat.shape[0]
    t0 = 0 if with_ctx else 1
    row = lambda wd: pl.BlockSpec((1, TM, wd), lambda b, i: (b, i + t0, 0))
    if split:
        x_specs = [pl.BlockSpec((1, TM, D), lambda b, i: (b, 0, 0)),
                   pl.BlockSpec((1, TM, D), lambda b, i: (b, jnp.maximum(i - 1, 0), 0))]
        x_args = list(xs)
    else:
        x_specs, x_args = [row(D)], [xs]
    oal = pl.BlockSpec((1, TM, 512), lambda b, i: (b, jnp.maximum(i + t0 - 1, 0), 0))
    oac = pl.BlockSpec((1, TM, 512), lambda b, i: (b, 0, 0))
    mod = pl.BlockSpec((1, 1, 1, D), lambda b, i: (b, _sel(i + t0, 1), 0, 0))
    return pl.pallas_call(
        functools.partial(_outproj_kernel, with_ctx=with_ctx, split=split),
        grid=(bsz, NT - t0),
        in_specs=x_specs + [oal, oac, row(HW), row(HW), row(HW), row(HW), row(HW), row(HW),
                            _layer_spec(wo, layer), _const_spec((1, HW)), _const_spec((1, HW)), mod],
        out_specs=pl.BlockSpec((1, TM, D), lambda b, i: (b, i, 0)),
        out_shape=jax.ShapeDtypeStruct((bsz, (NT - t0) * TM, D), F32),
        compiler_params=_cparams(("parallel", "arbitrary")),
    )(*x_args, oa_lat, oa_ctx, hof, hob, hsg, gof, gob, gsg, wo, hng, gng, gate)


def _regroup_rows(x):
    return jnp.swapaxes(x.reshape(8, TM // 8, x.shape[1]), 0, 1).reshape(TM, x.shape[1])


def _ungroup_rows(x):
    return jnp.swapaxes(x.reshape(TM // 8, 8, x.shape[1]), 0, 1).reshape(TM, x.shape[1])


def _ffn_kernel(xp_ref, xm_ref, xn_ref, a_ref, s_ref, wup_ref, cw_ref, cb_ref, wdn_ref, gate_ref,
                o_ref, h_scr, act_scr, *, nct, nt):
    i = pl.program_id(1)
    a, s = a_ref[0, 0], s_ref[0, 0]
    ng = TM // 8
    xg = _regroup_rows(xm_ref[0])
    h_scr[0:TM, :] = _normed(xg, a, s).astype(BF16)
    hp = jnp.where(_prev_ok(i, nct), _normed(xp_ref[0, HALO - 8:HALO, :], a, s), 0.0)
    hn = jnp.where(_next_ok(i, nct, nt), _normed(xn_ref[0, 0:8, :], a, s), 0.0)
    h_scr[TM:TM + 16, :] = jnp.concatenate([hp, hn], axis=0).astype(BF16)
    h = h_scr[...]
    sub = lax.broadcasted_iota(jnp.int32, (8, FF_CW), 0)

    def conv(col0, lo):
        u = _dot(h, wup_ref[0, :, col0 + lo:col0 + lo + FF_CW])
        ur = u[0:TM].reshape(ng, 8, FF_CW)
        first = jnp.where(sub == 0, u[TM + 7:TM + 8], pltpu.roll(ur[ng - 1], 1, 0))
        last = jnp.where(sub == 7, u[TM + 8:TM + 9], pltpu.roll(ur[0], 7, 0))
        w = cw_ref[:, col0 + lo:col0 + lo + FF_CW]
        return (jnp.concatenate([first[None], ur[:-1]], axis=0) * w[0:1] + ur * w[1:2]
                + jnp.concatenate([ur[1:], last[None]], axis=0) * w[2:3] + cb_ref[:, col0 + lo:col0 + lo + FF_CW])

    for cidx in range(D_FF // FF_CW):
        lo = cidx * FF_CW
        act = _silu(conv(0, lo)) * conv(D_FF, lo)
        act_scr[:, lo:lo + FF_CW] = act.reshape(TM, FF_CW).astype(BF16)
    ff = _dot(act_scr[...], wdn_ref[0])
    ms = jnp.mean(ff * ff, axis=-1, keepdims=True)
    o_ref[0] = _ungroup_rows(xg + gate_ref[0, 0] * (ff * lax.rsqrt(ms + EPS)))


def _ffn_call(xs, a_ff, s_ff, wup, cw, cb, wdn, layer, gate, nct):
    bsz, rows, _ = xs.shape
    nt = rows // TM
    mod = _mod_spec(nct)
    return pl.pallas_call(
        functools.partial(_ffn_kernel, nct=nct, nt=nt),
        grid=(bsz, nt),
        in_specs=_halo_specs(D, nct, nt) + [
            mod, mod, _layer_spec(wup, layer), _const_spec((3, 2 * D_FF)), _const_spec((1, 2 * D_FF)),
            _layer_spec(wdn, layer), mod],
        out_specs=pl.BlockSpec((1, TM, D), lambda b, i: (b, i, 0)),
        out_shape=jax.ShapeDtypeStruct((bsz, rows, D), F32),
        scratch_shapes=[pltpu.VMEM((TM + 16, D), BF16), pltpu.VMEM((TM, D_FF), BF16)],
        compiler_params=_cparams(("parallel", "arbitrary")),
    )(xs, xs, xs, a_ff, s_ff, wup, cw, cb, wdn, gate)


def _rope_tables():
    n_freq = HD // 4
    inv = ROPE_THETA ** (-jnp.arange(n_freq, dtype=F32) / n_freq)
    rows = jnp.repeat(jnp.arange(SEQ // GRID_W, dtype=F32), GRID_W)
    cols = jnp.tile(jnp.arange(GRID_W, dtype=F32), SEQ // GRID_W)
    ang = jnp.concatenate([rows[:, None] * inv, cols[:, None] * inv], axis=-1)
    cos, sin = jnp.cos(ang), jnp.sin(ang)
    cos_l = jnp.tile(jnp.concatenate([cos, cos], axis=-1), (1, 2))
    sin_l = jnp.tile(jnp.concatenate([-sin, sin], axis=-1), (1, 2))
    cos_t = jnp.concatenate([jnp.ones((CTX, 128), F32), cos_l], axis=0)
    sin_t = jnp.concatenate([jnp.zeros((CTX, 128), F32), sin_l], axis=0)
    return cos_t, sin_t


def kernel(x, c, ctx, c_ctx, ada_w, ada_b, norm_g, w_in, w_out, da_lambda, da_subln_g, hg_lb_logits, hg_norm_g,
           gd_conv_w, gd_a_log, gd_dt_bias, gd_norm_g, ffn_w_up, ffn_conv_w, ffn_conv_b, ffn_w_down):
    bsz = x.shape[0]
    depth = ada_w.shape[0]
    assert x.shape == (bsz, SEQ, D) and ctx.shape == (bsz, CTX, D) and bsz <= 8
    cos_t, sin_t = _rope_tables()

    cond = jnp.concatenate([jax.nn.silu(c.astype(F32)), jnp.zeros((8 - bsz, D), F32),
                            jax.nn.silu(c_ctx.astype(F32))[None], jnp.zeros((7, D), F32)], axis=0)
    mods = _ada_call(cond, ada_w, ada_b).reshape(depth, 16, 6, D)

    mod = jnp.stack([jnp.broadcast_to(mods[:, 8:9], (depth, bsz, 6, D)), mods[:, :bsz]], axis=2)
    g = norm_g.astype(F32)[:, None, None]
    a_in, s_in = g[..., 0:1, :] * (1.0 + mod[..., 1:2, :]), mod[..., 0:1, :]
    a_ff, s_ff = g[..., 2:3, :] * (1.0 + mod[..., 4:5, :]), mod[..., 3:4, :]
    gate1, gate2 = mod[..., 2:3, :] * g[..., 1:2, :], mod[..., 5:6, :] * g[..., 3:4, :]
    lb_w = jax.nn.softmax(hg_lb_logits.astype(F32), axis=0)
    lb = (jnp.cumsum(lb_w, axis=0) - lb_w[0]).reshape(depth, 1, 2 * HW)
    lb1m, lbm = 1.0 - lb, jnp.maximum(lb, LB_FLOOR)
    lane_pad = ((0, 0), (0, 0), (0, 128 - 2 * NH))
    nega = jnp.pad(-jnp.exp(gd_a_log.astype(F32)).reshape(depth, 1, 2 * NH), lane_pad)
    dtb = jnp.pad(gd_dt_bias.astype(F32).reshape(depth, 1, 2 * NH), lane_pad)
    lam_init = jnp.asarray([0.8 - 0.6 * math.exp(-0.3 * layer) for layer in range(depth)], F32)
    lp = da_lambda.astype(F32)
    lam = jnp.exp(jnp.sum(lp[:, 0] * lp[:, 1], axis=-1)) - jnp.exp(jnp.sum(lp[:, 2] * lp[:, 3], axis=-1)) + lam_init
    lam_arr = jnp.broadcast_to(lam[:, None, None], (depth, 8, TQ))
    g_arr = (da_subln_g.astype(F32) * (1.0 - lam_init)[:, None]).reshape(depth, 1, DV)
    hng = jnp.tile(hg_norm_g.astype(F32), (1, NH)).reshape(depth, 1, HW)
    gng = jnp.tile(gd_norm_g.astype(F32), (1, NH)).reshape(depth, 1, HW)
    conv_g, conv_f = gd_conv_w.astype(F32), ffn_conv_w.astype(F32)
    conv_fb = ffn_conv_b.astype(F32).reshape(depth, 1, 2 * D_FF)
    wo_b, wup_b, wdn_b = w_out.astype(BF16), ffn_w_up.astype(BF16), ffn_w_down.astype(BF16)
    w_in_t = jnp.swapaxes(w_in.astype(F32), 1, 2)

    xs = (ctx.astype(F32), x.astype(F32))
    for layer in range(depth):
        need_ctx = layer < depth - 1
        consts = [lb1m[layer], lbm[layer], conv_g[layer], nega[layer], dtb[layer]]
        (q, k, v, hq, hv, hk, hlf, hsg, gq, gk, gv, gab, gsg) = _inproj_call(
            xs, a_in[layer], s_in[layer], cos_t, sin_t, w_in_t, layer, consts)

        oa_lat = _attn_call(q, k, v, lam_arr[layer], g_arr[layer], CTX, SEQ, T)
        oa_ctx = _attn_call(q, k, v, lam_arr[layer], g_arr[layer], 0, CTX, CTX) if need_ctx else oa_lat

        hof, hob, gof, gob = _scans_call(hq, hv, hk, hlf, gq, gk, gv, gab)

        xs = _outproj_call(xs, oa_lat, oa_ctx, hof, hob, hsg, gof, gob, gsg, wo_b, layer,
                           hng[layer], gng[layer], gate1[layer], need_ctx)
        xs = _ffn_call(xs, a_ff[layer], s_ff[layer], wup_b, conv_f[layer], conv_fb[layer], wdn_b, layer,
                       gate2[layer], 1 if need_ctx else 0)
    return xs
```

```python
import functools
import math

import jax
import jax.numpy as jnp
import numpy as np
from jax import lax
from jax.experimental import pallas as pl
from jax.experimental.pallas import tpu as pltpu

F32 = jnp.float32
BF16 = jnp.bfloat16

D = 1024
CTX = 256
SEQ = 2048
T = CTX + SEQ
GRID_W = 64
ROPE_THETA = 10000.0
EPS = 1e-6
LB_FLOOR = 1e-30
NH = 4
HD = 64
DV = 128
HW = NH * HD
D_FF = 2816
TM = 256
NT = T // TM
HALO = 16
TQ = 256
ATT_QB = 256
ATT_QIN = 4
Q_SCALE = HD ** -0.5 * math.log2(math.e)
BLK = 16
HG_NV = TM // 8
assert HG_NV == 2 * BLK
HG_SB = 2 * BLK
CH = 64
FF_CW = 256
VMEM_LIMIT = 56 * 1024 * 1024

C_DAQ, C_DAK, C_DAV = 0, 512, 1024
C_HGQ, C_HGI, C_HGF, C_HGG = 1536, 1792, 2048, 2560
C_GDQKV, C_GDA, C_GDB, C_GDG = 2816, 3584, 3592, 3600
IN_COLS = 3856


def _cparams(sem):
    return pltpu.CompilerParams(dimension_semantics=sem, vmem_limit_bytes=VMEM_LIMIT)


def _const_spec(shape):
    n = len(shape)
    return pl.BlockSpec(shape, lambda *_: (0,) * n)


def _layer_spec(stacked, layer):
    n = stacked.ndim - 1
    return pl.BlockSpec((1,) + stacked.shape[1:], lambda *_: (layer,) + (0,) * n)


def _sigmoid(x):
    return 1.0 / (1.0 + jnp.exp(-x))


def _silu(x):
    return x * _sigmoid(x)


def _softplus(x):
    return jnp.maximum(x, 0.0) + jnp.log(1.0 + jnp.exp(-jnp.abs(x)))


def _dot(a, b):
    return jnp.dot(a, b, preferred_element_type=F32)


def _dot_nt(a, b):
    return lax.dot_general(a, b, (((1,), (1,)), ((), ())), preferred_element_type=F32)


def _dot_tn(a, b):
    return lax.dot_general(a, b, (((0,), (0,)), ((), ())), preferred_element_type=F32)


def _dot_hi(a, b):
    return jnp.dot(a, b, preferred_element_type=F32, precision=lax.Precision.HIGHEST)


def _head_sum(x, ones_bd):
    return _dot(x.astype(BF16), ones_bd)


def _ones_bd():
    r = lax.broadcasted_iota(jnp.int32, (HW, HW), 0) // HD
    c = lax.broadcasted_iota(jnp.int32, (HW, HW), 1) // HD
    return jnp.where(r == c, 1.0, 0.0).astype(BF16)


def _bd_mask():
    r = lax.broadcasted_iota(jnp.int32, (HW, HW), 0) // HD
    c = lax.broadcasted_iota(jnp.int32, (HW, HW), 1) // HD
    return r == c


def _normed(x, a, s):
    ms = jnp.mean(x * x, axis=-1, keepdims=True)
    return (x * lax.rsqrt(ms + EPS)) * a + s


def _ada_kernel(c_ref, w_ref, b_ref, o_ref):
    o_ref[0] = _dot(c_ref[...].astype(BF16), w_ref[0].astype(BF16)) + b_ref[0]


def _ada_call(cond, ada_w, ada_b):
    depth = ada_w.shape[0]
    nc = 6 * D
    cw = 1536
    return pl.pallas_call(
        _ada_kernel,
        grid=(depth, nc // cw),
        in_specs=[pl.BlockSpec((16, D), lambda l, j: (0, 0)),
                  pl.BlockSpec((1, D, cw), lambda l, j: (l, 0, j)),
                  pl.BlockSpec((1, 1, cw), lambda l, j: (l, 0, j))],
        out_specs=pl.BlockSpec((1, 16, cw), lambda l, j: (l, 0, j)),
        out_shape=jax.ShapeDtypeStruct((depth, 16, nc), F32),
        compiler_params=_cparams(("arbitrary", "arbitrary")),
    )(cond, ada_w, ada_b.reshape(depth, 1, nc))


def _sel(i, nct):
    return jnp.where(i >= nct, 1, 0)


def _prev_ok(i, nct):
    return i > nct


def _next_ok(i, nct, nt):
    return jnp.logical_and(i >= nct, i <= nt - 2)


def _halo_specs(width, nct, nt):
    per = TM // HALO
    return [pl.BlockSpec((1, HALO, width), lambda b, i: (b, i * per - jnp.where(_prev_ok(i, nct), 1, 0), 0)),
            pl.BlockSpec((1, TM, width), lambda b, i: (b, i, 0)),
            pl.BlockSpec((1, HALO, width), lambda b, i: (b, (i + 1) * per - jnp.where(_next_ok(i, nct, nt), 0, 1), 0))]


def _mod_spec(nct):
    return pl.BlockSpec((1, 1, 1, D), lambda b, i: (b, _sel(i, nct), 0, 0))


def _fill_h(h_scr, xp, xm, xn, a, s, i, nct, nt):
    hp = _normed(xp, a, s)
    hn = _normed(xn, a, s)
    h_scr[0:HALO, :] = jnp.where(_prev_ok(i, nct), hp, 0.0).astype(BF16)
    h_scr[HALO:HALO + TM, :] = _normed(xm, a, s).astype(BF16)
    h_scr[HALO + TM:, :] = jnp.where(_next_ok(i, nct, nt), hn, 0.0).astype(BF16)


def _conv3(u, w_ref):
    return (u[HALO - 1:HALO - 1 + TM, :] * w_ref[0:1, :]
            + u[HALO:HALO + TM, :] * w_ref[1:2, :]
            + u[HALO + 1:HALO + 1 + TM, :] * w_ref[2:3, :])


def _inproj_kernel(*refs, split):
    if split:
        ctx_ref, refs = refs[0], refs[1:]
    (xp_ref, xm_ref, xn_ref, a_ref, s_ref, cos_ref, sin_ref, win_ref,
     lb1m_ref, lbm_ref, gconv_ref, nega_ref, dtb_ref,
     q_ref, k_ref, v_ref, hq_ref, hv_ref, hk_ref, hlf_ref, hsg_ref,
     gq_ref, gk_ref, gv_ref, gab_ref, gsg_ref,
     h_scr, wqk_ref, wv_ref, whg_ref, wgq_ref, wab_ref, wgg_ref) = refs
    i = pl.program_id(1)

    @pl.when(jnp.logical_and(pl.program_id(0) == 0, i == 0))
    def _():
        wqk_ref[...] = win_ref[0, C_DAQ:C_DAV, :].T.astype(BF16)
        wv_ref[...] = win_ref[0, C_DAV:C_HGQ, :].T.astype(BF16)
        whg_ref[...] = win_ref[0, C_HGQ:C_GDQKV, :].T.astype(BF16)
        wgq_ref[...] = win_ref[0, C_GDQKV:C_GDA, :].T.astype(BF16)
        ab_lane = lax.broadcasted_iota(jnp.int32, (D, 128), 1)
        wab_ref[...] = jnp.where(ab_lane < 4 * NH, win_ref[0, C_GDA:C_GDA + 128, :].T, 0.0).astype(BF16)
        wgg_ref[...] = win_ref[0, C_GDG:IN_COLS, :].T.astype(BF16)

    xm = jnp.where(i == 0, ctx_ref[0], xm_ref[0]) if split else xm_ref[0]
    _fill_h(h_scr, xp_ref[0], xm, xn_ref[0], a_ref[0, 0], s_ref[0, 0], i, 1, NT)
    h = h_scr[HALO:HALO + TM, :]

    u_gd = _dot(h_scr[...], wgq_ref[...])
    z_hg = _dot(h, whg_ref[...])
    z_qk = _dot(h, wqk_ref[...])
    z_v = _dot(h, wv_ref[...])
    z_ab = _dot(h, wab_ref[...])
    z_gg = _dot(h, wgg_ref[...])

    lane = lax.broadcasted_iota(jnp.int32, (TM, 128), 1)
    first_half = (lane % HD) < (HD // 2)
    cs, sn = cos_ref[...], sin_ref[...]
    for j in range(8):
        xj = z_qk[:, j * 128:(j + 1) * 128]
        sw = jnp.where(first_half, pltpu.roll(xj, 128 - HD // 2, 1), pltpu.roll(xj, HD // 2, 1))
        r = xj * cs + sw * sn
        if j < 4:
            q_ref[0, :, j * 128:(j + 1) * 128] = (r * Q_SCALE).astype(BF16)
        else:
            k_ref[0, :, (j - 4) * 128:(j - 3) * 128] = r.astype(BF16)
    v_ref[0] = z_v.astype(BF16)

    z = z_hg
    hq_ref[0] = _silu(z[:, 0:HW]).astype(BF16)
    hv_ref[0] = z[:, HW:2 * HW].astype(BF16)
    sg = _sigmoid(z[:, 2 * HW:4 * HW])
    hk_ref[0] = (lb1m_ref[...] * (1.0 - sg)).astype(BF16)
    hlf_ref[0] = jnp.log(lbm_ref[...] + lb1m_ref[...] * sg)
    hsg_ref[0] = _silu(z[:, 4 * HW:5 * HW]).astype(BF16)

    y = _silu(_conv3(u_gd, gconv_ref))
    ones_bd = _ones_bd()
    qg, kg = y[:, 0:HW], y[:, HW:2 * HW]
    gq_ref[0] = (qg * lax.rsqrt(_head_sum(qg * qg, ones_bd) + EPS) * (HD ** -0.5)).astype(BF16)
    gk_ref[0] = (kg * lax.rsqrt(_head_sum(kg * kg, ones_bd) + EPS)).astype(BF16)
    gv_ref[0] = y[:, 2 * HW:3 * HW].astype(BF16)
    gab_ref[0] = jnp.where(lane < 2 * NH, nega_ref[...] * _softplus(z_ab + dtb_ref[...]), _sigmoid(z_ab))
    gsg_ref[0] = _silu(z_gg).astype(BF16)


def _split_x_specs():
    per = TM // HALO
    lat = lambda i: jnp.maximum(i - 1, 0)
    return [pl.BlockSpec((1, TM, D), lambda b, i: (b, 0, 0)),
            pl.BlockSpec((1, HALO, D), lambda b, i: (b, lat(i) * per - jnp.where(_prev_ok(i, 1), 1, 0), 0)),
            pl.BlockSpec((1, TM, D), lambda b, i: (b, lat(i), 0)),
            pl.BlockSpec((1, HALO, D), lambda b, i: (b, (lat(i) + 1) * per - jnp.where(_next_ok(i, 1, NT), 0, 1), 0))]


def _inproj_call(xs, a_in, s_in, cos_t, sin_t, w_in, layer, consts):
    split = isinstance(xs, tuple)
    w_spec = pl.BlockSpec((1, IN_COLS, D), lambda b, i: (layer, 0, 0))
    w_groups = [C_DAV - C_DAQ, C_HGQ - C_DAV, C_GDQKV - C_HGQ, C_GDA - C_GDQKV, 128, IN_COLS - C_GDG]
    x_args = (xs[0], xs[1], xs[1], xs[1]) if split else (xs, xs, xs)
    bsz = x_args[0].shape[0]
    row = lambda wd: pl.BlockSpec((1, TM, wd), lambda b, i: (b, i, 0))
    tab = pl.BlockSpec((TM, 128), lambda b, i: (i, 0))
    out_w = [(512, BF16), (512, BF16), (512, BF16), (HW, BF16), (HW, BF16), (2 * HW, BF16), (2 * HW, F32),
             (HW, BF16), (HW, BF16), (HW, BF16), (HW, BF16), (128, F32), (HW, BF16)]
    return pl.pallas_call(
        functools.partial(_inproj_kernel, split=split),
        grid=(bsz, NT),
        in_specs=(_split_x_specs() if split else _halo_specs(D, 1, NT)) + [_mod_spec(1), _mod_spec(1), tab, tab]
        + [w_spec] + [_const_spec(x.shape) for x in consts],
        out_specs=[row(wd) for wd, _ in out_w],
        out_shape=[jax.ShapeDtypeStruct((bsz, T, wd), dt) for wd, dt in out_w],
        scratch_shapes=[pltpu.VMEM((TM + 2 * HALO, D), BF16)]
        + [pltpu.VMEM((D, wd), BF16) for wd in w_groups],
        compiler_params=_cparams(("arbitrary", "arbitrary")),
    )(*x_args, a_in, s_in, cos_t, sin_t, w_in, *consts)


def _attn_kernel(*refs, n_qin):
    q_refs = refs[:n_qin]
    k_ref, v_ref, lam_ref, g_ref, o_ref = refs[n_qin:]
    k, v = k_ref[0], v_ref[0]
    lane = lax.broadcasted_iota(jnp.int32, (TQ, 128), 1)
    zero = jnp.zeros((TQ, 128), BF16)
    per = ATT_QB // TQ
    ix = range(n_qin * per)
    q = [q_refs[j // per][0, (j % per) * TQ:(j % per + 1) * TQ, :] for j in ix]
    qs = [jnp.concatenate([jnp.where(lane < HD, q[j], zero), jnp.where(lane >= HD, q[j], zero)], axis=0) for j in ix]
    st = [_dot_nt(k, qs[j]) for j in ix]
    m = [jnp.max(st[j], axis=0, keepdims=True) for j in ix]
    p = [jnp.exp2(st[j] - m[j]) for j in ix]
    l = [jnp.sum(p[j], axis=0, keepdims=True) for j in ix]
    ot = [_dot_tn(v, p[j].astype(BF16)) * (1.0 / l[j]) for j in ix]
    for j in ix:
        d = ot[j][:, :TQ] - lam_ref[0:1, :] * ot[j][:, TQ:]
        ms = jnp.mean(d * d, axis=0, keepdims=True)
        dn = d * lax.rsqrt(ms + EPS)
        o_ref[0, j * TQ:(j + 1) * TQ, :] = (dn.T * g_ref[...]).astype(BF16)


def _attn_call(q, k, v, lam_arr, g_arr, q_row0, n_rows, tk):
    bsz = q.shape[0]
    n_qin = min(ATT_QIN, n_rows // ATT_QB)
    tq = n_qin * ATT_QB
    assert q_row0 % ATT_QB == 0 and n_rows % tq == 0
    qoff = q_row0 // ATT_QB

    def q_spec(j):
        return pl.BlockSpec((1, ATT_QB, 128), lambda b, h, i: (b, qoff + i * n_qin + j, h))

    return pl.pallas_call(
        functools.partial(_attn_kernel, n_qin=n_qin),
        grid=(bsz, NH, n_rows // tq),
        in_specs=[q_spec(j) for j in range(n_qin)]
        + [pl.BlockSpec((1, tk, 128), lambda b, h, i: (b, 0, h)),
           pl.BlockSpec((1, tk, 128), lambda b, h, i: (b, 0, h)),
           _const_spec((8, TQ)), _const_spec((1, DV))],
        out_specs=pl.BlockSpec((1, tq, 128), lambda b, h, i: (b, i, h)),
        out_shape=jax.ShapeDtypeStruct((bsz, n_rows, 512), BF16),
        compiler_params=_cparams(("parallel", "parallel", "arbitrary")),
    )(*([q] * n_qin), k, v, lam_arr, g_arr)


def _bwd_tile(i):
    return jnp.where(i == 0, 0, NT - i)


def _zero_at_row_start(*scratch):
    @pl.when(pl.program_id(1) == 0)
    def _():
        for s in scratch:
            s[...] = jnp.zeros_like(s)


def _hg_body(qf_ref, vf_ref, kf_ref, lf_ref, qb_ref, vb_ref, kb_ref, lb_ref,
             of_ref, ob_ref, sf_scr, sb_scr):
    q = (qf_ref[0].astype(F32), qb_ref[0].astype(F32))
    k = (kf_ref[0].astype(F32), kb_ref[0].astype(F32))
    v = (vf_ref[0].astype(F32), vb_ref[0].astype(F32))
    logf = (lf_ref[0], lb_ref[0])
    o_refs = (of_ref, ob_ref)
    dirs = (0, 1)

    r = lax.broadcasted_iota(jnp.int32, (TM, TM), 0)
    c = lax.broadcasted_iota(jnp.int32, (TM, TM), 1)
    same = (r // BLK) == (c // BLK)
    tri = [jnp.where(jnp.logical_and(same, (c >= r) if d else (c <= r)), 1.0, 0.0).astype(BF16) for d in dirs]
    blk = jnp.where(same, 1.0, 0.0).astype(BF16)
    pair_of = jnp.where((r // HG_SB) == (c // HG_SB), 1.0, 0.0).astype(BF16)
    bl = [_dot_01(tri[d], logf[d]) for d in dirs]
    tot = [_dot_01(blk, logf[d]) for d in dirs]
    tot2 = [_dot_01(pair_of, logf[d]) for d in dirs]
    qd16 = [q[d] * jnp.exp(bl[d]) for d in dirs]
    kd16 = [k[d] * jnp.exp(tot[d] - bl[d]) for d in dirs]
    f = [jnp.exp(logf[d]) for d in dirs]
    half = (lax.broadcasted_iota(jnp.int32, (TM, HW), 0) // BLK) % 2
    second = [half == (0 if d else 1) for d in dirs]
    e_other = [jnp.exp(tot2[d] - tot[d]) for d in dirs]
    qdb = [(qd16[d] * jnp.where(second[d], e_other[d], 1.0)).astype(BF16) for d in dirs]
    kd = [(kd16[d] * jnp.where(second[d], 1.0, e_other[d])).astype(BF16) for d in dirs]
    e_blk = [jnp.exp(tot2[d]) for d in dirs]

    def regroup(x):
        return jnp.swapaxes(x.reshape(8, HG_NV, HW), 0, 1).reshape(2, BLK, 8, HW)

    q4 = [regroup(q[d]) for d in dirs]
    k4 = [regroup(k[d]) for d in dirs]
    v4 = [regroup(v[d]) for d in dirs]
    f4 = [regroup(f[d]) for d in dirs]
    ones_bd = _ones_bd()
    e = [None, None]
    o4 = [None, None]
    for n in range(BLK):
        ln = BLK - n
        for d in dirs:
            qs, ks = (slice(0, ln), slice(n, BLK)) if d else (slice(n, BLK), slice(0, ln))
            if n == 0:
                pn = q4[d] * k4[d]
            else:
                fs = slice(n - 1, n - 1 + ln) if d else slice(1, 1 + ln)
                e[d] = f4[d][:, fs] if n == 1 else (e[d][:, :ln] if d else e[d][:, 1:]) * f4[d][:, fs]
                pn = q4[d][:, qs] * k4[d][:, ks] * e[d]
            a = _dot(pn.reshape(2 * ln * 8, HW).astype(BF16), ones_bd).reshape(2, ln, 8, HW) * v4[d][:, ks]
            if n == 0:
                o4[d] = a
            else:
                pad = jnp.zeros((2, n, 8, HW), F32)
                o4[d] = o4[d] + jnp.concatenate([a, pad] if d else [pad, a], axis=1)
    o_band = [jnp.swapaxes(o4[d].reshape(HG_NV, 8, HW), 0, 1).reshape(TM, HW) for d in dirs]

    n_sb = TM // HG_SB
    nq = n_sb * BLK
    hr = lax.broadcasted_iota(jnp.int32, (NH * nq, HW), 0) // nq
    hl = lax.broadcasted_iota(jnp.int32, (NH * nq, HW), 1) // HD
    same_head = hr == hl
    pr = lax.broadcasted_iota(jnp.int32, (nq, NH * nq), 0) // BLK
    pc = (lax.broadcasted_iota(jnp.int32, (nq, NH * nq), 1) % nq) // BLK
    same_step = pr == pc
    zero_b = jnp.zeros((), BF16)
    for d in dirs:
        def blocks(x, which):
            return x.reshape(n_sb, 2, BLK, HW)[:, which if d == 0 else 1 - which].reshape(nq, HW)

        k1 = blocks(kd16[d], 0).astype(BF16)
        v1 = blocks(v[d], 0).astype(BF16)
        q2 = blocks(qd16[d], 1).astype(BF16)
        sc = _dot_nt(q2, jnp.where(same_head, jnp.concatenate([k1] * NH, axis=0), zero_b))
        sc = jnp.where(same_step, sc, 0.0).astype(BF16)
        o2 = _dot(sc, jnp.where(same_head, jnp.concatenate([v1] * NH, axis=0), zero_b)).reshape(n_sb, BLK, HW)
        ob4 = o_band[d].reshape(n_sb, 2, BLK, HW)
        parts = [ob4[:, 0], ob4[:, 1]]
        parts[1 if d == 0 else 0] = parts[1 if d == 0 else 0] + o2
        o_band[d] = jnp.stack(parts, axis=1).reshape(TM, HW)

    lane = lax.broadcasted_iota(jnp.int32, (HD, 128), 1)
    own = [(lane >= HD) if h % 2 else (lane < HD) for h in range(NH)]
    tile = [slice(128 * (h // 2), 128 * (h // 2) + 128) for h in range(NH)]
    zero_t = jnp.zeros((HD, 128), BF16)
    vb = [v[d].astype(BF16) for d in dirs]
    nb = n_sb
    s_scr = (sf_scr, sb_scr)
    s = [[s_scr[d][h] for h in range(NH)] for d in dirs]
    for j in range(nb):
        for d in dirs:
            ib = nb - 1 - j if d else j
            rows = slice(ib * HG_SB, (ib + 1) * HG_SB)
            s_bd = jnp.concatenate(
                [jnp.concatenate([s[d][h].astype(BF16) if c == h // 2 else zero_t for c in range(HW // 128)], axis=1)
                 for h in range(NH)], axis=0)
            o_refs[d][0, rows, :] = (o_band[d][rows] + _dot_nt(qdb[d][rows], s_bd)).astype(BF16)
            u = _dot_tn(vb[d][rows], kd[d][rows])
            e_row = e_blk[d][ib * HG_SB:ib * HG_SB + 1, :]
            for h in range(NH):
                s[d][h] = s[d][h] * e_row[:, tile[h]] + jnp.where(own[h], u[h * HD:(h + 1) * HD, tile[h]], 0.0)
    for d in dirs:
        for h in range(NH):
            s_scr[d][h] = s[d][h]


def _bd4(x, bd):
    return jnp.where(bd, jnp.concatenate([x] * NH, axis=0), 0.0).astype(BF16)


def _split3(x):
    h = x.astype(BF16)
    r1 = x - h.astype(F32)
    m = r1.astype(BF16)
    l = (r1 - m.astype(F32)).astype(BF16)
    return jnp.concatenate([h, m, l], axis=0)


def _dot_01(sel01, x):
    return _dot(jnp.concatenate([sel01] * 3, axis=1), _split3(x))


class _GdMasks:
    def __init__(self, reverse):
        r = lax.broadcasted_iota(jnp.int32, (CH, CH), 0)
        c = lax.broadcasted_iota(jnp.int32, (CH, CH), 1)
        self.tri = jnp.where((c >= r) if reverse else (c <= r), 1.0, 0.0).astype(BF16)
        t_i = lax.broadcasted_iota(jnp.int32, (CH, HW), 0)
        s_i = lax.broadcasted_iota(jnp.int32, (CH, HW), 1) % HD
        self.later = (t_i < s_i) if reverse else (t_i > s_i)
        self.valid = (s_i >= t_i) if reverse else (s_i <= t_i)
        self.strict = (s_i > t_i) if reverse else (s_i < t_i)
        self.eye = jnp.where(t_i == s_i, 1.0, 0.0)
        self.last = 0 if reverse else CH - 1
        self.off = {}
        m = 1
        while m < CH:
            t_blk, s_blk = t_i // m, s_i // m
            pair = (t_blk // 2) == (s_blk // 2)
            lo_hi = (t_blk % 2 == 0, s_blk % 2 == 1) if reverse else (t_blk % 2 == 1, s_blk % 2 == 0)
            self.off[m] = jnp.logical_and(pair, jnp.logical_and(*lo_hi))
            m *= 2


def _gd_prep(q, k, v, ab, dirs, masks, bd):
    grp = lax.broadcasted_iota(jnp.int32, (CH, HW), 1) // HD

    def widen(a, col0):
        out = jnp.zeros((CH, HW), F32)
        for h in range(NH):
            out = jnp.where(grp == h, jnp.broadcast_to(a[:, col0 + h:col0 + h + 1], (CH, HW)), out)
        return out

    ix = range(len(q))
    mk = [masks[d] for d in dirs]
    la = [widen(ab[j], dirs[j] * NH) for j in ix]
    beta = [widen(ab[j], 2 * NH + dirs[j] * NH) for j in ix]
    gc = [_dot_01(mk[j].tri, la[j]) for j in ix]
    diff = [_dot_01(mk[j].tri, jnp.where(mk[j].later, la[j], 0.0)) for j in ix]
    kb = [_bd4(k[j], bd) for j in ix]
    kq = [_dot_nt(jnp.concatenate([k[j].astype(BF16), q[j].astype(BF16)], axis=0), kb[j]) for j in ix]
    kk = [t[:CH] for t in kq]
    qk = [t[CH:] for t in kq]
    dm = [jnp.where(mk[j].valid, jnp.exp(jnp.minimum(diff[j], 0.0)), 0.0) for j in ix]
    n = [jnp.where(mk[j].strict, beta[j] * kk[j] * dm[j], 0.0) for j in ix]

    tinv = [mk[j].eye - jnp.where(mk[j].off[1], n[j], 0.0) for j in ix]
    m = 2
    while m < CH:
        y = [_dot(tinv[j].astype(BF16), _bd4(jnp.where(mk[j].off[m], n[j], 0.0), bd)) for j in ix]
        tinv = [tinv[j] - _dot(y[j].astype(BF16), _bd4(tinv[j], bd)) for j in ix]
        m *= 2
    tb = [t.astype(BF16) for t in tinv]
    eg = [jnp.exp(g) for g in gc]
    u = [_dot(tb[j], _bd4(v[j] * beta[j], bd)) for j in ix]
    w = [_dot(tb[j], _bd4(k[j] * beta[j] * eg[j], bd)) for j in ix]
    sc = [(qk[j] * dm[j]).astype(BF16) for j in ix]
    qeff = [(q[j] * eg[j] - _dot(sc[j], _bd4(w[j], bd))).astype(BF16) for j in ix]
    oc = [_dot(sc[j], _bd4(u[j], bd)) for j in ix]
    gl = [gc[j][mk[j].last:mk[j].last + 1, :] for j in ix]
    kd = [(k[j] * jnp.exp(gl[j] - gc[j])).astype(BF16) for j in ix]
    kwu = [_dot_tn(kd[j], jnp.concatenate([w[j].astype(BF16), u[j].astype(BF16)], axis=1)) for j in ix]
    kw = [t[:, :HW].astype(BF16) for t in kwu]
    ku = [t[:, HW:] for t in kwu]
    a = [jnp.exp(g) for g in gl]
    return qeff, oc, kw, ku, a


def _gd_body(qf_ref, kf_ref, vf_ref, abf_ref, qb_ref, kb_ref, vb_ref, abb_ref,
             of_ref, ob_ref, sf_scr, sb_scr):
    bd = _bd_mask()
    n_ch = TM // CH
    refs = ((qf_ref, kf_ref, vf_ref, abf_ref), (qb_ref, kb_ref, vb_ref, abb_ref))
    o_refs = (of_ref, ob_ref)
    items = [(d, (n_ch - 1 - j) if d else j) for j in range(n_ch) for d in range(2)]
    rows = [slice(ic * CH, (ic + 1) * CH) for _, ic in items]
    dirs = [d for d, _ in items]
    load = lambda which: [refs[d][which][0, r, :] for d, r in zip(dirs, rows)]
    qeff, oc, kw, ku, a = _gd_prep(load(0), load(1), load(2), load(3), dirs, (_GdMasks(False), _GdMasks(True)), bd)

    s = [sf_scr[...], sb_scr[...]]
    for j, (d, _) in enumerate(items):
        sb = s[d].astype(BF16)
        o_refs[d][0, rows[j], :] = (_dot(qeff[j], sb) + oc[j]).astype(BF16)
        s[d] = jnp.where(bd, s[d] * a[j] - _dot(kw[j], sb) + ku[j], 0.0)
    sf_scr[...] = s[0]
    sb_scr[...] = s[1]


def _scans_kernel(*refs):
    hg_in, gd_in = refs[0:8], refs[8:16]
    hg_out, gd_out = refs[16:18], refs[18:20]
    hg_scr, gd_scr = refs[20:22], refs[22:24]
    _zero_at_row_start(*hg_scr, *gd_scr)
    _hg_body(*hg_in, *hg_out, *hg_scr)
    _gd_body(*gd_in, *gd_out, *gd_scr)


def _scans_call(hq, hv, hk, hlf, gq, gk, gv, gab):
    bsz = hq.shape[0]
    f = lambda b, i: (b, i, 0)
    g0 = lambda b, i: (b, _bwd_tile(i), 0)
    g1 = lambda b, i: (b, _bwd_tile(i), 1)
    blk = lambda m: pl.BlockSpec((1, TM, HW), m)
    abs_ = lambda m: pl.BlockSpec((1, TM, 128), m)
    return pl.pallas_call(
        _scans_kernel,
        grid=(bsz, NT),
        in_specs=[blk(f), blk(f), blk(f), blk(f), blk(g0), blk(g0), blk(g1), blk(g1),
                  blk(f), blk(f), blk(f), abs_(f), blk(g0), blk(g0), blk(g0), abs_(g0)],
        out_specs=[blk(f), blk(g0), blk(f), blk(g0)],
        out_shape=[jax.ShapeDtypeStruct((bsz, T, HW), BF16)] * 4,
        scratch_shapes=[pltpu.VMEM((NH, HD, 128), F32)] * 2 + [pltpu.VMEM((HW, HW), F32)] * 2,
        compiler_params=_cparams(("parallel", "arbitrary")),
    )(hq, hv, hk, hlf, hq, hv, hk, hlf, gq, gk, gv, gab, gq, gk, gv, gab)


def _outproj_kernel(*refs, with_ctx, split):
    if split:
        ctx_ref, refs = refs[0], refs[1:]
    (x_ref, oal_ref, oac_ref, hof_ref, hob_ref, hsg_ref, gof_ref, gob_ref, gsg_ref,
     wo_ref, hng_ref, gng_ref, gate_ref, o_ref) = refs
    ones_bd = _ones_bd()
    oa = oal_ref[0]
    x = x_ref[0]
    if with_ctx:
        oa = jnp.where(pl.program_id(1) == 0, oac_ref[0], oa)
    if split:
        x = jnp.where(pl.program_id(1) == 0, ctx_ref[0], x)

    def finish(of_ref, ob_ref, sg_ref, ng_ref):
        o = of_ref[0].astype(F32) + ob_ref[0].astype(F32)
        ms = _head_sum(o * o, ones_bd) * (1.0 / HD)
        return (o * lax.rsqrt(ms + EPS) * ng_ref[...] * sg_ref[0]).astype(BF16)

    ob = finish(hof_ref, hob_ref, hsg_ref, hng_ref)
    oc = finish(gof_ref, gob_ref, gsg_ref, gng_ref)
    mix = _dot(jnp.concatenate([oa, ob, oc], axis=1), wo_ref[0])
    ms = jnp.mean(mix * mix, axis=-1, keepdims=True)
    o_ref[0] = x + gate_ref[0, 0] * (mix * lax.rsqrt(ms + EPS))


def _outproj_call(xs, oa_lat, oa_ctx, hof, hob, hsg, gof, gob, gsg, wo, layer, hng, gng, gate, with_ctx):
    split = isinstance(xs, tuple)
    assert with_ctx or not split
    bsz = oa_lat.shape[0]
    t0 = 0 if with_ctx else 1
    row = lambda wd: pl.BlockSpec((1, TM, wd), lambda b, i: (b, i + t0, 0))
    if split:
        x_specs = [pl.BlockSpec((1, TM, D), lambda b, i: (b, 0, 0)),
                   pl.BlockSpec((1, TM, D), lambda b, i: (b, jnp.maximum(i - 1, 0), 0))]
        x_args = list(xs)
    else:
        x_specs, x_args = [row(D)], [xs]
    oal = pl.BlockSpec((1, TM, 512), lambda b, i: (b, jnp.maximum(i + t0 - 1, 0), 0))
    oac = pl.BlockSpec((1, TM, 512), lambda b, i: (b, 0, 0))
    mod = pl.BlockSpec((1, 1, 1, D), lambda b, i: (b, _sel(i + t0, 1), 0, 0))
    return pl.pallas_call(
        functools.partial(_outproj_kernel, with_ctx=with_ctx, split=split),
        grid=(bsz, NT - t0),
        in_specs=x_specs + [oal, oac, row(HW), row(HW), row(HW), row(HW), row(HW), row(HW),
                            _layer_spec(wo, layer), _const_spec((1, HW)), _const_spec((1, HW)), mod],
        out_specs=pl.BlockSpec((1, TM, D), lambda b, i: (b, i, 0)),
        out_shape=jax.ShapeDtypeStruct((bsz, (NT - t0) * TM, D), F32),
        compiler_params=_cparams(("parallel", "arbitrary")),
    )(*x_args, oa_lat, oa_ctx, hof, hob, hsg, gof, gob, gsg, wo, hng, gng, gate)


def _regroup_rows(x):
    return jnp.swapaxes(x.reshape(8, TM // 8, x.shape[1]), 0, 1).reshape(TM, x.shape[1])


def _ungroup_rows(x):
    return jnp.swapaxes(x.reshape(TM // 8, 8, x.shape[1]), 0, 1).reshape(TM, x.shape[1])


def _ffn_kernel(xp_ref, xm_ref, xn_ref, a_ref, s_ref, wup_ref, cw_ref, cb_ref, wdn_ref, gate_ref,
                o_ref, h_scr, act_scr, *, nct, nt):
    i = pl.program_id(1)
    a, s = a_ref[0, 0], s_ref[0, 0]
    ng = TM // 8
    xg = _regroup_rows(xm_ref[0])
    h_scr[0:TM, :] = _normed(xg, a, s).astype(BF16)
    hp = jnp.where(_prev_ok(i, nct), _normed(xp_ref[0, HALO - 8:HALO, :], a, s), 0.0)
    hn = jnp.where(_next_ok(i, nct, nt), _normed(xn_ref[0, 0:8, :], a, s), 0.0)
    h_scr[TM:TM + 16, :] = jnp.concatenate([hp, hn], axis=0).astype(BF16)
    h = h_scr[...]
    sub = lax.broadcasted_iota(jnp.int32, (8, FF_CW), 0)

    def conv(col0, lo):
        u = _dot(h, wup_ref[0, :, col0 + lo:col0 + lo + FF_CW])
        ur = u[0:TM].reshape(ng, 8, FF_CW)
        first = jnp.where(sub == 0, u[TM + 7:TM + 8], pltpu.roll(ur[ng - 1], 1, 0))
        last = jnp.where(sub == 7, u[TM + 8:TM + 9], pltpu.roll(ur[0], 7, 0))
        w = cw_ref[:, col0 + lo:col0 + lo + FF_CW]
        return (jnp.concatenate([first[None], ur[:-1]], axis=0) * w[0:1] + ur * w[1:2]
                + jnp.concatenate([ur[1:], last[None]], axis=0) * w[2:3] + cb_ref[:, col0 + lo:col0 + lo + FF_CW])

    for cidx in range(D_FF // FF_CW):
        lo = cidx * FF_CW
        act = _silu(conv(0, lo)) * conv(D_FF, lo)
        act_scr[:, lo:lo + FF_CW] = act.reshape(TM, FF_CW).astype(BF16)
    ff = _dot(act_scr[...], wdn_ref[0])
    ms = jnp.mean(ff * ff, axis=-1, keepdims=True)
    o_ref[0] = _ungroup_rows(xg + gate_ref[0, 0] * (ff * lax.rsqrt(ms + EPS)))


def _ffn_call(xs, a_ff, s_ff, wup, cw, cb, wdn, layer, gate, nct):
    bsz, rows, _ = xs.shape
    nt = rows // TM
    mod = _mod_spec(nct)
    return pl.pallas_call(
        functools.partial(_ffn_kernel, nct=nct, nt=nt),
        grid=(bsz, nt),
        in_specs=_halo_specs(D, nct, nt) + [
            mod, mod, _layer_spec(wup, layer), _const_spec((3, 2 * D_FF)), _const_spec((1, 2 * D_FF)),
            _layer_spec(wdn, layer), mod],
        out_specs=pl.BlockSpec((1, TM, D), lambda b, i: (b, i, 0)),
        out_shape=jax.ShapeDtypeStruct((bsz, rows, D), F32),
        scratch_shapes=[pltpu.VMEM((TM + 16, D), BF16), pltpu.VMEM((TM, D_FF), BF16)],
        compiler_params=_cparams(("parallel", "arbitrary")),
    )(xs, xs, xs, a_ff, s_ff, wup, cw, cb, wdn, gate)


def _rope_tables():
    n_freq = HD // 4
    inv = ROPE_THETA ** (-jnp.arange(n_freq, dtype=F32) / n_freq)
    rows = jnp.repeat(jnp.arange(SEQ // GRID_W, dtype=F32), GRID_W)
    cols = jnp.tile(jnp.arange(GRID_W, dtype=F32), SEQ // GRID_W)
    ang = jnp.concatenate([rows[:, None] * inv, cols[:, None] * inv], axis=-1)
    cos, sin = jnp.cos(ang), jnp.sin(ang)
    cos_l = jnp.tile(jnp.concatenate([cos, cos], axis=-1), (1, 2))
    sin_l = jnp.tile(jnp.concatenate([-sin, sin], axis=-1), (1, 2))
    cos_t = jnp.concatenate([jnp.ones((CTX, 128), F32), cos_l], axis=0)
    sin_t = jnp.concatenate([jnp.zeros((CTX, 128), F32), sin_l], axis=0)
    return cos_t, sin_t


def kernel(x, c, ctx, c_ctx, ada_w, ada_b, norm_g, w_in, w_out, da_lambda, da_subln_g, hg_lb_logits, hg_norm_g,
           gd_conv_w, gd_a_log, gd_dt_bias, gd_norm_g, ffn_w_up, ffn_conv_w, ffn_conv_b, ffn_w_down):
    bsz = x.shape[0]
    depth = ada_w.shape[0]
    assert x.shape == (bsz, SEQ, D) and ctx.shape == (bsz, CTX, D) and bsz <= 8
    cos_t, sin_t = _rope_tables()

    cond = jnp.concatenate([jax.nn.silu(c.astype(F32)), jnp.zeros((8 - bsz, D), F32),
                            jax.nn.silu(c_ctx.astype(F32))[None], jnp.zeros((7, D), F32)], axis=0)
    mods = _ada_call(cond, ada_w, ada_b).reshape(depth, 16, 6, D)

    mod = jnp.stack([jnp.broadcast_to(mods[:, 8:9], (depth, bsz, 6, D)), mods[:, :bsz]], axis=2)
    g = norm_g.astype(F32)[:, None, None]
    a_in, s_in = g[..., 0:1, :] * (1.0 + mod[..., 1:2, :]), mod[..., 0:1, :]
    a_ff, s_ff = g[..., 2:3, :] * (1.0 + mod[..., 4:5, :]), mod[..., 3:4, :]
    gate1, gate2 = mod[..., 2:3, :] * g[..., 1:2, :], mod[..., 5:6, :] * g[..., 3:4, :]
    lb_w = jax.nn.softmax(hg_lb_logits.astype(F32), axis=0)
    lb = (jnp.cumsum(lb_w, axis=0) - lb_w[0]).reshape(depth, 1, 2 * HW)
    lb1m, lbm = 1.0 - lb, jnp.maximum(lb, LB_FLOOR)
    lane_pad = ((0, 0), (0, 0), (0, 128 - 2 * NH))
    nega = jnp.pad(-jnp.exp(gd_a_log.astype(F32)).reshape(depth, 1, 2 * NH), lane_pad)
    dtb = jnp.pad(gd_dt_bias.astype(F32).reshape(depth, 1, 2 * NH), lane_pad)
    lam_init = jnp.asarray([0.8 - 0.6 * math.exp(-0.3 * layer) for layer in range(depth)], F32)
    lp = da_lambda.astype(F32)
    lam = jnp.exp(jnp.sum(lp[:, 0] * lp[:, 1], axis=-1)) - jnp.exp(jnp.sum(lp[:, 2] * lp[:, 3], axis=-1)) + lam_init
    lam_arr = jnp.broadcast_to(lam[:, None, None], (depth, 8, TQ))
    g_arr = (da_subln_g.astype(F32) * (1.0 - lam_init)[:, None]).reshape(depth, 1, DV)
    hng = jnp.tile(hg_norm_g.astype(F32), (1, NH)).reshape(depth, 1, HW)
    gng = jnp.tile(gd_norm_g.astype(F32), (1, NH)).reshape(depth, 1, HW)
    conv_g, conv_f = gd_conv_w.astype(F32), ffn_conv_w.astype(F32)
    conv_fb = ffn_conv_b.astype(F32).reshape(depth, 1, 2 * D_FF)
    wo_b, wup_b, wdn_b = w_out.astype(BF16), ffn_w_up.astype(BF16), ffn_w_down.astype(BF16)
    w_in_t = jnp.swapaxes(w_in.astype(F32), 1, 2)

    xs = (ctx.astype(F32), x.astype(F32))
    for layer in range(depth):
        need_ctx = layer < depth - 1
        consts = [lb1m[layer], lbm[layer], conv_g[layer], nega[layer], dtb[layer]]
        (q, k, v, hq, hv, hk, hlf, hsg, gq, gk, gv, gab, gsg) = _inproj_call(
            xs, a_in[layer], s_in[layer], cos_t, sin_t, w_in_t, layer, consts)

        oa_lat = _attn_call(q, k, v, lam_arr[layer], g_arr[layer], CTX, SEQ, T)
        oa_ctx = _attn_call(q, k, v, lam_arr[layer], g_arr[layer], 0, CTX, CTX) if need_ctx else oa_lat

        hof, hob, gof, gob = _scans_call(hq, hv, hk, hlf, gq, gk, gv, gab)

        xs = _outproj_call(xs, oa_lat, oa_ctx, hof, hob, hsg, gof, gob, gsg, wo_b, layer,
                           hng[layer], gng[layer], gate1[layer], need_ctx)
        xs = _ffn_call(xs, a_ff[layer], s_ff[layer], wup_b, conv_f[layer], conv_fb[layer], wdn_b, layer,
                       gate2[layer], 1 if need_ctx else 0)
    return xs
```
